```python
import jax
import jax.numpy as jnp
from jax import lax
import numpy as np

D_MODEL = 2048
BATCH = 16
SEQ = 2048
DEPTH = 4

GRID_W = 64
CTX_LEN = 256
NORM_EPS = 1e-6

RWKV_HEADS = 8
RWKV_HEAD_DIM = 64
D_RWKV = RWKV_HEADS * RWKV_HEAD_DIM
DECAY_LORA = max(32, int(round(1.8 * D_MODEL ** 0.5 / 32)) * 32)
ICL_LORA = max(32, int(round(1.8 * D_MODEL ** 0.5 / 32)) * 32)
GATE_LORA = max(32, int(round(0.6 * D_MODEL ** 0.8 / 32)) * 32)
SHORT_CONV = 3
GN_EPS = 64e-5

FOURIER_GROUPS = 4
FOURIER_GROUP_DIM = 128
D_FOURIER = FOURIER_GROUPS * FOURIER_GROUP_DIM

ATTN_HEADS = 8
ATTN_KV_HEADS = 2
ATTN_HEAD_DIM = 128
GQA_GROUP = ATTN_HEADS // ATTN_KV_HEADS
D_ATTN = ATTN_HEADS * ATTN_HEAD_DIM
D_ATTN_KV = ATTN_KV_HEADS * ATTN_HEAD_DIM
ATTN_SCALE = ATTN_HEAD_DIM ** -0.5
Q_BLOCK = 128
ROPE_THETA = 10000.0
AXIS_FREQS = ATTN_HEAD_DIM // 4

N_BRANCH = 3

N_EXPERTS = 32
TOP_K = 4
D_EXPERT = 512
SWIGLU_LIMIT = 7.0
SWIGLU_ALPHA = 1.702

COL_SIZES = (D_RWKV, D_RWKV, DECAY_LORA, DECAY_LORA, ICL_LORA, ICL_LORA, D_ATTN_KV, D_ATTN_KV,
             D_RWKV, GATE_LORA, D_ATTN, D_FOURIER, N_BRANCH * D_MODEL)
N_STATE_GROUPS = 8
N_IN = sum(COL_SIZES)

kernel_name = 'hybrid_rwkv7_fourier_gqa_moe_dit'


def _split_cols(z, sizes):
    idx = np.cumsum(sizes)[:-1].tolist()
    return jnp.split(z, idx, axis=-1)


def _rwkv_heads(z):
    return z.reshape(z.shape[:-1] + (RWKV_HEADS, RWKV_HEAD_DIM))


def _q_heads(z):
    return z.reshape(z.shape[:-1] + (ATTN_HEADS, ATTN_HEAD_DIM))


def _kv_heads(z):
    return z.reshape(z.shape[:-1] + (ATTN_KV_HEADS, ATTN_HEAD_DIM))


def rms_norm(x, gain):
    xf = x.astype(jnp.float32)
    y = xf * lax.rsqrt(jnp.mean(xf * xf, axis=-1, keepdims=True) + NORM_EPS)
    return (y * gain.astype(jnp.float32)).astype(x.dtype)


def modulate(x, gain, shift, scale):
    return rms_norm(x, gain) * (1 + scale) + shift


def short_conv(z, w):
    L = z.shape[1]
    pad = SHORT_CONV // 2
    zp = jnp.pad(z, ((0, 0), (pad, pad), (0, 0)))
    return sum(zp[:, i:i + L] * w[i] for i in range(SHORT_CONV))


def axial_rope(n_tokens):
    rows = n_tokens // GRID_W
    row = jnp.repeat(jnp.arange(rows), GRID_W)
    col = jnp.tile(jnp.arange(GRID_W), rows)
    pos = jnp.stack([row, col], axis=-1).astype(jnp.float32)
    inv_freq = ROPE_THETA ** (-jnp.arange(AXIS_FREQS, dtype=jnp.float32) / AXIS_FREQS)
    ang = pos[..., None] * inv_freq
    return jnp.cos(ang), jnp.sin(ang)


def apply_axial_rope(x, cos, sin):
    xs = x.reshape(x.shape[:-1] + (2, 2, AXIS_FREQS))
    x1, x2 = xs[..., 0, :], xs[..., 1, :]
    cs = cos[:, None].astype(x.dtype)
    sn = sin[:, None].astype(x.dtype)
    return jnp.stack([x1 * cs - x2 * sn, x1 * sn + x2 * cs], axis=-2).reshape(x.shape)


def gqa_softmax(q, k, v):
    s = jnp.einsum('bqhgd,bkhd->bhgqk', q, k, preferred_element_type=jnp.float32) * ATTN_SCALE
    p = jax.nn.softmax(s, axis=-1).astype(v.dtype)
    return jnp.einsum('bhgqk,bkhd->bqhgd', p, v)


def latent_attention(q, k_lat, v_lat, k_ctx, v_ctx):
    B, S = q.shape[:2]
    keys = jnp.concatenate([k_ctx, k_lat], axis=1)
    vals = jnp.concatenate([v_ctx, v_lat], axis=1)
    nb = S // Q_BLOCK
    qb = q.reshape(B, nb, Q_BLOCK, ATTN_KV_HEADS, GQA_GROUP, ATTN_HEAD_DIM).transpose(1, 0, 2, 3, 4, 5)
    out = lax.map(lambda qblk: gqa_softmax(qblk, keys, vals), qb)
    return out.transpose(1, 0, 2, 3, 4, 5).reshape(B, S, D_ATTN)


def wkv_scan(decay, k, v, a, b, state0, reverse, r):
    seqs = (decay, k, v, a, b) if r is None else (decay, k, v, a, b, r)
    xs = tuple(jnp.swapaxes(z, 0, 1).astype(jnp.float32) for z in seqs)

    def step(state, inp):
        w_t, k_t, v_t, a_t, b_t = inp[:5]
        sa = jnp.einsum('bhvk,bhk->bhv', state, a_t)
        state = (state * w_t[:, :, None, :] + sa[..., None] * b_t[:, :, None, :]
                 + v_t[..., None] * k_t[:, :, None, :])
        y = None if r is None else jnp.einsum('bhvk,bhk->bhv', state, inp[5])
        return state, y

    state, ys = lax.scan(step, state0, xs, reverse=reverse)
    return state, (None if r is None else jnp.swapaxes(ys, 0, 1))


def rwkv_prepare(z, p):
    k, v, wd_f, wd_b, ad_f, ad_b = z
    k = short_conv(k, p['rwkv_conv'][0])
    v = short_conv(v, p['rwkv_conv'][1])
    kk = _rwkv_heads(k * p['k_k']).astype(jnp.float32)
    kk = kk / jnp.maximum(jnp.linalg.norm(kk, axis=-1, keepdims=True), 1e-12)
    dirs = []
    for d, (wd, ad) in enumerate(((wd_f, ad_f), (wd_b, ad_b))):
        w_raw = (p['w0'][d] + jnp.tanh(wd) @ p['w_lora'][d]).astype(jnp.float32)
        decay = jnp.exp(-jnp.exp(-jax.nn.softplus(-w_raw) - 0.5))
        a = jax.nn.sigmoid((p['a0'][d] + ad @ p['a_lora'][d]).astype(jnp.float32))
        k_d = k.astype(jnp.float32) * (1 + (a - 1) * p['k_a'])
        dirs.append((_rwkv_heads(decay), _rwkv_heads(k_d), _rwkv_heads(a)))
    return _rwkv_heads(v.astype(jnp.float32)), kk, dirs


def rwkv_scans(prep, init_states, r):
    v, kk, dirs = prep
    finals, ys = [], []
    for d, (decay, k_d, a) in enumerate(dirs):
        s, y = wkv_scan(decay, k_d, v, -kk, kk * a, init_states[d], d == 1, r)
        finals.append(s)
        ys.append(y)
    return finals, ys


def rwkv_output(prep, ys, r, gd, p):
    v, _, dirs = prep
    y = ys[0] + ys[1]
    mu = jnp.mean(y, axis=-1, keepdims=True)
    var = jnp.mean(jnp.square(y - mu), axis=-1, keepdims=True)
    y = (y - mu) * lax.rsqrt(var + GN_EPS)
    B, L = y.shape[:2]
    y = y.reshape(B, L, D_RWKV) * p['lnx_g'] + p['lnx_b']
    k_bonus = 0.5 * (dirs[0][1] + dirs[1][1])
    bonus = jnp.sum(r * k_bonus * _rwkv_heads(p['r_k']), axis=-1, keepdims=True) * v
    g = jax.nn.sigmoid(gd) @ p['g_lora']
    return ((y + bonus.reshape(B, L, D_RWKV)) * g).astype(gd.dtype)


def fourier_mix(f):
    B, L = f.shape[:2]
    fg = f.reshape(B, L, FOURIER_GROUPS, FOURIER_GROUP_DIM).astype(jnp.float32)
    out = jnp.fft.fft2(fg, axes=(1, 3), norm='ortho').real
    return out.reshape(B, L, D_FOURIER).astype(f.dtype)


def merge(rw, four, att, gate_logits, p):
    g_a, g_f, g_c = jnp.split(jax.nn.sigmoid(gate_logits), N_BRANCH, axis=-1)
    y = (g_a * (rw @ p['w_br_rwkv']) + g_f * (four @ p['w_br_fourier'])
         + g_c * (att @ p['w_br_attn']))
    return y @ p['w_out']


def token_mixer(h_lat, h_ctx, p, cos, sin, ctx_out):
    B, S, _ = h_lat.shape
    n_ctx = h_ctx.shape[1]
    ctx_sizes = COL_SIZES if ctx_out else COL_SIZES[:N_STATE_GROUPS]
    zc = _split_cols(h_ctx @ p['w_in'][:, :sum(ctx_sizes)], ctx_sizes)
    zl = _split_cols(h_lat @ p['w_in'], COL_SIZES)

    zero = jnp.zeros((B, RWKV_HEADS, RWKV_HEAD_DIM, RWKV_HEAD_DIM), jnp.float32)
    prep_c = rwkv_prepare(zc[0:6], p)
    r_c = _rwkv_heads(short_conv(zc[8], p['rwkv_conv'][2])) if ctx_out else None
    states_c, ys_c = rwkv_scans(prep_c, (zero, zero), r_c)
    prep_l = rwkv_prepare(zl[0:6], p)
    r_l = _rwkv_heads(short_conv(zl[8], p['rwkv_conv'][2]))
    _, ys_l = rwkv_scans(prep_l, states_c, r_l)
    rw_l = rwkv_output(prep_l, ys_l, r_l, zl[9], p)

    k_c = rms_norm(_kv_heads(zc[6]), p['k_norm_g'])
    v_c = _kv_heads(zc[7])
    q_l = apply_axial_rope(rms_norm(_q_heads(zl[10]), p['q_norm_g']), cos, sin)
    k_l = apply_axial_rope(rms_norm(_kv_heads(zl[6]), p['k_norm_g']), cos, sin)
    att_l = latent_attention(q_l, k_l, _kv_heads(zl[7]), k_c, v_c)

    out_l = merge(rw_l, fourier_mix(zl[11]), att_l, zl[12], p)
    if not ctx_out:
        return out_l, None
    rw_c = rwkv_output(prep_c, ys_c, r_c, zc[9], p)
    q_c = rms_norm(_q_heads(zc[10]), p['q_norm_g'])
    att_c = gqa_softmax(q_c.reshape(B, n_ctx, ATTN_KV_HEADS, GQA_GROUP, ATTN_HEAD_DIM),
                        k_c, v_c).reshape(B, n_ctx, D_ATTN)
    out_c = merge(rw_c, fourier_mix(zc[11]), att_c, zc[12], p)
    return out_l, out_c


def moe_ffn(h, router_w, router_b, w_gu, b_gu, w_down, b_down):
    logits = (h @ router_w + router_b).astype(jnp.float32)
    top_v, top_i = lax.top_k(logits, TOP_K)
    top_w = jax.nn.softmax(top_v, axis=-1)
    gate_full = jnp.sum(jax.nn.one_hot(top_i, N_EXPERTS, dtype=jnp.float32) * top_w[..., None],
                        axis=1).astype(h.dtype)

    def expert(acc, e_inp):
        wgu, bgu, wd, bd, ge = e_inp
        gu = h @ wgu + bgu
        gate = jnp.minimum(gu[:, :D_EXPERT], SWIGLU_LIMIT)
        up = jnp.clip(gu[:, D_EXPERT:], -SWIGLU_LIMIT, SWIGLU_LIMIT)
        act = (up + 1) * gate * jax.nn.sigmoid(SWIGLU_ALPHA * gate)
        return acc + ge[:, None] * (act @ wd + bd), None

    out, _ = lax.scan(expert, jnp.zeros_like(h), (w_gu, b_gu, w_down, b_down, gate_full.T))
    return out


def setup_inputs(seed: int = 0) -> dict:
    key = jax.random.key(seed)
    ks = iter(jax.random.split(key, 48))
    f32 = jnp.float32

    def nrm(shape, scale):
        return jax.random.normal(next(ks), shape, f32) * scale

    L, D, E, F = DEPTH, D_MODEL, N_EXPERTS, D_EXPERT
    centre = (jnp.arange(SHORT_CONV) == SHORT_CONV // 2).astype(f32)[None, None, :, None]
    return {
        'x': nrm((BATCH, SEQ, D), 1.0),
        'c': nrm((BATCH, D), 1.0),
        'ctx': nrm((BATCH, CTX_LEN, D), 1.0),
        'c_ctx': nrm((D,), 1.0),
        'ada_w': nrm((L, D, 6 * D), 0.5 * D ** -0.5),
        'ada_b': nrm((L, 6 * D), 0.02),
        'norm1_g': 1.0 + nrm((L, D), 0.02),
        'norm2_g': 1.0 + nrm((L, D), 0.02),
        'w_in': nrm((L, D, N_IN), D ** -0.5),
        'rwkv_conv': centre + nrm((L, 3, SHORT_CONV, D_RWKV), 0.2),
        'w0': jax.random.uniform(next(ks), (L, 2, D_RWKV), f32, -3.0, 1.0),
        'w_lora': nrm((L, 2, DECAY_LORA, D_RWKV), DECAY_LORA ** -0.5),
        'a0': nrm((L, 2, D_RWKV), 0.1),
        'a_lora': nrm((L, 2, ICL_LORA, D_RWKV), ICL_LORA ** -0.5),
        'g_lora': nrm((L, GATE_LORA, D_RWKV), GATE_LORA ** -0.5),
        'k_k': 0.85 + nrm((L, D_RWKV), 0.05),
        'k_a': 1.0 + nrm((L, D_RWKV), 0.05),
        'r_k': -0.04 + nrm((L, D_RWKV), 0.02),
        'lnx_g': 1.0 + nrm((L, D_RWKV), 0.02),
        'lnx_b': nrm((L, D_RWKV), 0.02),
        'q_norm_g': 1.0 + nrm((L, ATTN_HEAD_DIM), 0.02),
        'k_norm_g': 1.0 + nrm((L, ATTN_HEAD_DIM), 0.02),
        'w_br_rwkv': nrm((L, D_RWKV, D), D_RWKV ** -0.5),
        'w_br_fourier': nrm((L, D_FOURIER, D), D_FOURIER ** -0.5),
        'w_br_attn': nrm((L, D_ATTN, D), D_ATTN ** -0.5),
        'w_out': nrm((L, D, D), D ** -0.5),
        'router_w': nrm((L, D, E), D ** -0.5),
        'router_b': nrm((L, E), 0.01),
        'exp_w_gu': nrm((L, E, D, 2 * F), D ** -0.5),
        'exp_b_gu': nrm((L, E, 2 * F), 0.02),
        'exp_w_down': nrm((L, E, F, D), F ** -0.5),
        'exp_b_down': nrm((L, E, D), 0.02),
    }


def reference(x, c, ctx, c_ctx, ada_w, ada_b, norm1_g, norm2_g, w_in, rwkv_conv, w0, w_lora, a0,
              a_lora, g_lora, k_k, k_a, r_k, lnx_g, lnx_b, q_norm_g, k_norm_g, w_br_rwkv,
              w_br_fourier, w_br_attn, w_out, router_w, router_b, exp_w_gu, exp_b_gu, exp_w_down,
              exp_b_down):
    B, S, D = x.shape
    n_lat = B * S
    cos, sin = axial_rope(S)
    for l in range(DEPTH):
        last = l == DEPTH - 1
        p = dict(w_in=w_in[l], rwkv_conv=rwkv_conv[l], w0=w0[l], w_lora=w_lora[l], a0=a0[l],
                 a_lora=a_lora[l], g_lora=g_lora[l], k_k=k_k[l], k_a=k_a[l], r_k=r_k[l],
                 lnx_g=lnx_g[l], lnx_b=lnx_b[l], q_norm_g=q_norm_g[l], k_norm_g=k_norm_g[l],
                 w_br_rwkv=w_br_rwkv[l], w_br_fourier=w_br_fourier[l], w_br_attn=w_br_attn[l],
                 w_out=w_out[l])
        mod = jax.nn.silu(c) @ ada_w[l] + ada_b[l]
        sh1, sc1, gt1, sh2, sc2, gt2 = [m[:, None, :] for m in jnp.split(mod, 6, axis=-1)]
        n_ctx_mod = 2 if last else 6
        mod_c = jax.nn.silu(c_ctx) @ ada_w[l][:, :n_ctx_mod * D] + ada_b[l][:n_ctx_mod * D]
        mods_c = jnp.split(mod_c, n_ctx_mod, axis=-1)

        h_lat = modulate(x, norm1_g[l], sh1, sc1)
        h_ctx = modulate(ctx, norm1_g[l], mods_c[0], mods_c[1])
        m_lat, m_ctx = token_mixer(h_lat, h_ctx, p, cos, sin, not last)
        x = x + gt1 * m_lat

        h2 = modulate(x, norm2_g[l], sh2, sc2).reshape(n_lat, D)
        moe_p = (router_w[l], router_b[l], exp_w_gu[l], exp_b_gu[l], exp_w_down[l], exp_b_down[l])
        if last:
            x = x + gt2 * moe_ffn(h2, *moe_p).reshape(B, S, D)
        else:
            ctx = ctx + mods_c[2] * m_ctx
            h2c = modulate(ctx, norm2_g[l], mods_c[3], mods_c[4]).reshape(-1, D)
            y = moe_ffn(jnp.concatenate([h2, h2c], axis=0), *moe_p)
            x = x + gt2 * y[:n_lat].reshape(B, S, D)
            ctx = ctx + mods_c[5] * y[n_lat:].reshape(ctx.shape)
    return x
```

```python
import functools

import numpy as np
import jax
import jax.numpy as jnp
from jax import lax
from jax.experimental import pallas as pl
from jax.experimental.pallas import tpu as pltpu

F32 = jnp.float32
BF16 = jnp.bfloat16

NORM_EPS = 1e-6
GN_EPS = 64e-5
RWKV_HEADS = 8
RWKV_HEAD_DIM = 64
D_RWKV = RWKV_HEADS * RWKV_HEAD_DIM
LORA = 96
GATE_LORA = 256
D_FOURIER = 512
FOURIER_GROUP_DIM = 128
ATTN_HEADS = 8
ATTN_KV_HEADS = 2
ATTN_HEAD_DIM = 128
GQA_GROUP = ATTN_HEADS // ATTN_KV_HEADS
D_ATTN = ATTN_HEADS * ATTN_HEAD_DIM
D_ATTN_KV = ATTN_KV_HEADS * ATTN_HEAD_DIM
ATTN_SCALE = ATTN_HEAD_DIM ** -0.5
GRID_W = 64
ROPE_THETA = 10000.0
AXIS_FREQS = ATTN_HEAD_DIM // 4
TOP_K = 4
SWIGLU_LIMIT = 7.0
SWIGLU_ALPHA = 1.702

LANES = 128
BF16_SUBLANES = 16
VMEM_LIMIT = 56 * 1024 * 1024

LORA_PAD = LANES
W_IN_TN = 1536
WKV_CHUNK = 64
PAIR = 2 * RWKV_HEAD_DIM
ROUTER_PAD = LANES
NEG_BIG = -1e30

OFF_K, OFF_V, OFF_LORA, OFF_R, OFF_Q, OFF_FOUR, OFF_KA, OFF_VA, OFF_GD, OFF_END = (
    0, 512, 1024, 1536, 2048, 3072, 3584, 3840, 4096, 4352)


def _cparams(*sem):
    return pltpu.CompilerParams(dimension_semantics=sem, vmem_limit_bytes=VMEM_LIMIT)


def _dot(a, b):
    return jnp.dot(a, b, preferred_element_type=F32)


def _dot_nt(a, b):
    return lax.dot_general(a, b, (((1,), (1,)), ((), ())), preferred_element_type=F32)


def _dot_tn(a, b):
    return lax.dot_general(a, b, (((0,), (0,)), ((), ())), preferred_element_type=F32)


def _split_dot(x, g):
    hi = x.astype(BF16)
    lo = (x - hi.astype(F32)).astype(BF16)
    return _dot(hi, g) + _dot(lo, g)


def _sigmoid(x):
    return 1.0 / (1.0 + jnp.exp(-x))


def _adaln_kernel(c_ref, w_ref, b_ref, o_ref):
    c = c_ref[...]
    s = (c * _sigmoid(c)).astype(BF16)
    o_ref[...] = _dot(s, w_ref[...].astype(BF16)) + b_ref[...]


def _adaln(c_all, ada_w, ada_b):
    depth, d, n = ada_w.shape
    rows = c_all.shape[0]
    tn = 1024 if n % 1024 == 0 else n
    return pl.pallas_call(
        _adaln_kernel,
        out_shape=jax.ShapeDtypeStruct((depth, rows, n), F32),
        grid=(depth, n // tn),
        in_specs=[
            pl.BlockSpec((rows, d), lambda l, j: (0, 0)),
            pl.BlockSpec((None, d, tn), lambda l, j: (l, 0, j)),
            pl.BlockSpec((None, 1, tn), lambda l, j: (l, 0, j)),
        ],
        out_specs=pl.BlockSpec((None, rows, tn), lambda l, j: (l, 0, j)),
        compiler_params=_cparams("parallel", "parallel"),
        name="adaln",
    )(c_all, ada_w, ada_b.reshape(depth, 1, n))


def _modulated_norm(x, g, sc, sh):
    ms = jnp.mean(x * x, axis=-1, keepdims=True)
    return x * lax.rsqrt(ms + NORM_EPS) * g * (1.0 + sc) + sh


def _normmm_kernel(x_ref, g_ref, sc_ref, sh_ref, w_ref, o_ref, h_ref):
    @pl.when(pl.program_id(1) == 0)
    def _():
        h_ref[...] = _modulated_norm(x_ref[...], g_ref[...], sc_ref[...], sh_ref[...]).astype(BF16)

    o_ref[...] = _dot(h_ref[...], w_ref[...]).astype(o_ref.dtype)


def _norm_in_proj(xs, gain, mod4, w, geo):
    rows, d = xs.shape
    n = w.shape[1]
    tm, tn = geo.tm, W_IN_TN
    bidx = geo.batch_of_tile(tm)
    return pl.pallas_call(
        _normmm_kernel,
        out_shape=jax.ShapeDtypeStruct((rows, n), BF16),
        grid=(rows // tm, n // tn),
        in_specs=[
            pl.BlockSpec((tm, d), lambda i, j: (i, 0)),
            pl.BlockSpec((1, d), lambda i, j: (0, 0)),
            pl.BlockSpec((None, None, 1, d), lambda i, j: (bidx(i), 1, 0, 0)),
            pl.BlockSpec((None, None, 1, d), lambda i, j: (bidx(i), 0, 0, 0)),
            pl.BlockSpec((d, tn), lambda i, j: (0, j)),
        ],
        out_specs=pl.BlockSpec((tm, tn), lambda i, j: (i, j)),
        scratch_shapes=[pltpu.VMEM((tm, d), BF16)],
        compiler_params=_cparams("parallel", "arbitrary"),
        name="norm_in_proj",
    )(xs, gain.reshape(1, d), mod4, mod4, w)


def _norm_router_kernel(x_ref, g_ref, sc_ref, sh_ref, wh_ref, wl_ref, rb_ref, h_ref, gate_ref):
    h = _modulated_norm(x_ref[...], g_ref[...], sc_ref[...], sh_ref[...])
    hi = h.astype(BF16)
    h_ref[...] = hi
    lo = (h - hi.astype(F32)).astype(BF16)
    logits = _dot(hi, wh_ref[...]) + _dot(lo, wh_ref[...]) + _dot(hi, wl_ref[...]) + rb_ref[...]
    lane = lax.broadcasted_iota(jnp.int32, logits.shape, 1).astype(F32)
    work = logits
    vals, sels = [], []
    for _ in range(TOP_K):
        m = jnp.max(work, axis=-1, keepdims=True)
        first = jnp.min(jnp.where(work == m, lane, float(ROUTER_PAD)), axis=-1, keepdims=True)
        sel = lane == first
        vals.append(m)
        sels.append(sel)
        work = jnp.where(sel, 2.0 * NEG_BIG, work)
    exps = [jnp.exp(v - vals[0]) for v in vals]
    denom = exps[0] + exps[1] + exps[2] + exps[3]
    gates = jnp.zeros_like(logits)
    for e, sel in zip(exps, sels):
        gates = jnp.where(sel, e / denom, gates)
    gate_ref[...] = gates


def _norm_router(xs, gain, mod4, rw_hi, rw_lo, rb, geo, n_rows):
    d = xs.shape[1]
    tm = geo.tm
    bidx = geo.batch_of_tile(tm)
    return pl.pallas_call(
        _norm_router_kernel,
        out_shape=(jax.ShapeDtypeStruct((n_rows, d), BF16),
                   jax.ShapeDtypeStruct((n_rows, ROUTER_PAD), F32)),
        grid=(n_rows // tm,),
        in_specs=[
            pl.BlockSpec((tm, d), lambda i: (i, 0)),
            pl.BlockSpec((1, d), lambda i: (0, 0)),
            pl.BlockSpec((None, None, 1, d), lambda i: (bidx(i), 4, 0, 0)),
            pl.BlockSpec((None, None, 1, d), lambda i: (bidx(i), 3, 0, 0)),
            pl.BlockSpec((d, ROUTER_PAD), lambda i: (0, 0)),
            pl.BlockSpec((d, ROUTER_PAD), lambda i: (0, 0)),
            pl.BlockSpec((1, ROUTER_PAD), lambda i: (0, 0)),
        ],
        out_specs=(pl.BlockSpec((tm, d), lambda i: (i, 0)),
                   pl.BlockSpec((tm, ROUTER_PAD), lambda i: (i, 0))),
        compiler_params=_cparams("parallel"),
        name="norm_router",
    )(xs, gain.reshape(1, d), mod4, mod4, rw_hi, rw_lo, rb)


def _prep_kernel(zk_ref, zv_ref, zl_ref, zr_ref, pk_ref, pv_ref, pr_ref, nk_ref, nv_ref, nr_ref,
                 conv_ref, w0_ref, wl_ref, a0_ref, al_ref, kk_ref, ka_ref, hsum_ref,
                 v_out, r_out, kkn_out, lw_out, kd_out, bb_out, *, t, lat_tiles, lat_per, ctx_per):
    i = pl.program_id(0)
    is_lat = i < lat_tiles
    per = jnp.where(is_lat, lat_per, ctx_per)
    j = jnp.where(is_lat, i, i - lat_tiles) % per
    first = j == 0
    last = j == per - 1
    row = lax.broadcasted_iota(jnp.int32, (t, 1), 0)

    def conv(z_ref, p_ref, n_ref, which):
        z = z_ref[...].astype(F32)
        prev_row = p_ref[...].astype(F32)[BF16_SUBLANES - 1:BF16_SUBLANES, :]
        next_row = n_ref[...].astype(F32)[0:1, :]
        prev_row = jnp.where(first, 0.0, prev_row)
        next_row = jnp.where(last, 0.0, next_row)
        zm = jnp.where(row == 0, prev_row, pltpu.roll(z, 1, 0))
        zp = jnp.where(row == t - 1, next_row, pltpu.roll(z, t - 1, 0))
        w = conv_ref[which]
        return zm * w[0:1] + z * w[1:2] + zp * w[2:3]

    k = conv(zk_ref, pk_ref, nk_ref, 0)
    v = conv(zv_ref, pv_ref, nv_ref, 1)
    r = conv(zr_ref, pr_ref, nr_ref, 2)
    v_out[...] = v.astype(BF16)
    r_out[...] = r.astype(BF16)
    kkv = k * kk_ref[...]
    ss = _split_dot(kkv * kkv, hsum_ref[...])
    kkn = kkv * lax.rsqrt(jnp.maximum(ss, 1e-24))
    kkn_out[...] = kkn.astype(BF16)
    zl = zl_ref[...]
    for d in range(2):
        wd = zl[:, d * LORA_PAD:(d + 1) * LORA_PAD].astype(F32)
        w_raw = w0_ref[d] + _dot(jnp.tanh(wd).astype(BF16), wl_ref[d])
        lw_out[d] = -_sigmoid(w_raw) * float(np.exp(-0.5))
        ad = zl[:, (2 + d) * LORA_PAD:(3 + d) * LORA_PAD]
        a = _sigmoid(a0_ref[d] + _dot(ad, al_ref[d]))
        kd_out[d] = (k * (1.0 + (a - 1.0) * ka_ref[...])).astype(BF16)
        bb_out[d] = (kkn * a).astype(BF16)


def _rwkv_prepare(z, p, geo):
    rows = z.shape[0]
    t = geo.tp
    base = geo.col0
    hb = t // BF16_SUBLANES
    n_halo = rows // BF16_SUBLANES

    def zcol(off):
        cb = (base + off) // D_RWKV
        return pl.BlockSpec((t, D_RWKV), lambda i: (i, cb))

    def prev(off):
        cb = (base + off) // D_RWKV
        return pl.BlockSpec((BF16_SUBLANES, D_RWKV), lambda i: (jnp.maximum(i * hb - 1, 0), cb))

    def nxt(off):
        cb = (base + off) // D_RWKV
        return pl.BlockSpec((BF16_SUBLANES, D_RWKV),
                            lambda i: (jnp.minimum((i + 1) * hb, n_halo - 1), cb))

    def full(a):
        nd = a.ndim
        return pl.BlockSpec(a.shape, lambda i: (0,) * nd)

    params = (p['conv'], p['w0'], p['w_lora'], p['a0'], p['a_lora'], p['k_k'], p['k_a'], geo.hsum)
    row_spec = pl.BlockSpec((t, D_RWKV), lambda i: (i, 0))
    dir_spec = pl.BlockSpec((2, t, D_RWKV), lambda i: (0, i, 0))
    kern = functools.partial(_prep_kernel, t=t, lat_tiles=geo.r_lat // t, lat_per=geo.s // t,
                             ctx_per=geo.lc // t)
    return pl.pallas_call(
        kern,
        out_shape=(jax.ShapeDtypeStruct((rows, D_RWKV), BF16),
                   jax.ShapeDtypeStruct((rows, D_RWKV), BF16),
                   jax.ShapeDtypeStruct((rows, D_RWKV), BF16),
                   jax.ShapeDtypeStruct((2, rows, D_RWKV), F32),
                   jax.ShapeDtypeStruct((2, rows, D_RWKV), BF16),
                   jax.ShapeDtypeStruct((2, rows, D_RWKV), BF16)),
        grid=(rows // t,),
        in_specs=[zcol(OFF_K), zcol(OFF_V), zcol(OFF_LORA), zcol(OFF_R),
                  prev(OFF_K), prev(OFF_V), prev(OFF_R), nxt(OFF_K), nxt(OFF_V), nxt(OFF_R)]
                 + [full(a) for a in params],
        out_specs=(row_spec, row_spec, row_spec, dir_spec, dir_spec, dir_spec),
        compiler_params=_cparams("parallel"),
        name="rwkv_prepare",
    )(z, z, z, z, z, z, z, z, z, z, *params)


def _wkv_kernel(v_ref, r_ref, kk_ref, lw_ref, kd_ref, bb_ref, y_ref, s_ref, *, c):
    d = pl.program_id(1)

    @pl.when(pl.program_id(2) == 0)
    def _():
        s_ref[...] = jnp.zeros_like(s_ref)

    sign = 1 - 2 * d
    c2 = 2 * c
    ti = lax.broadcasted_iota(jnp.int32, (c, c), 0)
    ii = lax.broadcasted_iota(jnp.int32, (c, c), 1)
    tri = jnp.where((ti - ii) * sign >= 0, 1.0, 0.0).astype(BF16)
    rt = lax.broadcasted_iota(jnp.int32, (c2, c2), 0)
    ci = lax.broadcasted_iota(jnp.int32, (c2, c2), 1)
    same = (rt // c) == (ci // c)
    before = (rt % c - ci % c) * sign
    strict = same & (before > 0)
    incl = same & (before >= 0)
    head0 = lax.broadcasted_iota(jnp.int32, (c, PAIR), 1) < RWKV_HEAD_DIM
    eye = (lax.broadcasted_iota(jnp.int32, (PAIR, PAIR), 0)
           == lax.broadcasted_iota(jnp.int32, (PAIR, PAIR), 1)).astype(F32)

    def stack(x):
        return jnp.concatenate([jnp.where(head0, x, 0.0), jnp.where(head0, 0.0, x)], axis=0).astype(BF16)

    for p in range(RWKV_HEADS // 2):
        sl = slice(p * PAIR, (p + 1) * PAIR)
        lw = lw_ref[:, sl]
        lw_hi = lw.astype(BF16)
        lw_lo = (lw - lw_hi.astype(F32)).astype(BF16)
        cum = _dot(tri, lw_hi) + _dot(tri, lw_lo)
        g_in = jnp.exp(cum)
        g_ex = jnp.exp(cum - lw)
        g_inv = jnp.exp(-cum)
        g_end = jnp.exp(jnp.sum(lw, axis=0, keepdims=True))
        kkn = kk_ref[:, sl].astype(F32)
        a_s = stack(-kkn * g_ex)
        b_s = stack(bb_ref[:, sl].astype(F32) * g_inv)
        k_s = stack(kd_ref[:, sl].astype(F32) * g_inv)
        r_s = stack(r_ref[:, sl].astype(F32) * g_in)
        v_s = stack(v_ref[:, sl].astype(F32))

        g = _dot_nt(jnp.concatenate([a_s, r_s], axis=0), jnp.concatenate([b_s, k_s], axis=0))
        l_ab = jnp.where(strict, g[:c2, :c2], 0.0).astype(BF16)
        l_ak = jnp.where(strict, g[:c2, c2:], 0.0).astype(BF16)
        m_rb = jnp.where(incl, g[c2:, :c2], 0.0).astype(BF16)
        m_rk = jnp.where(incl, g[c2:, c2:], 0.0).astype(BF16)

        x = jnp.concatenate([a_s.astype(F32), _dot(l_ak, v_s)], axis=1)
        lp = l_ab
        n_sq = int(np.log2(c))
        for sq in range(n_sq):
            x = x + _dot(lp, x.astype(BF16))
            if sq + 1 < n_sq:
                lp = _dot(lp, lp).astype(BF16)
        xb = x.astype(BF16)
        mx = _dot(m_rb, xb)
        gq = r_s.astype(F32) + mx[:, :PAIR]
        yq = mx[:, PAIR:] + _dot(m_rk, v_s)
        wb = _dot_tn(xb, b_s)
        vk = _dot_tn(v_s, k_s)
        s0 = s_ref[p]
        s0b = s0.astype(BF16)
        ys = _dot_nt(gq.astype(BF16), s0b) + yq
        y_ref[:, sl] = ys[:c] + ys[c:]
        p_mat = ((eye + wb[:PAIR]) * g_end).astype(BF16)
        s_ref[p] = _dot(s0b, p_mat) + (wb[PAIR:] + vk) * g_end


def _wkv_scan(v, r, kkn, lw, kd, bb, geo):
    rows = v.shape[0]
    c = geo.chunk
    n_ctx, n_lat = geo.lc // c, geo.s // c
    lat_blocks = geo.r_lat // c

    def rowblk(b, d, s):
        ctx_j = jnp.where(d == 0, s, n_ctx - 1 - s)
        lat_j = jnp.where(d == 0, s - n_ctx, n_lat - 1 - (s - n_ctx))
        return jnp.where(s < n_ctx, lat_blocks + b * n_ctx + ctx_j, b * n_lat + lat_j)

    shared = pl.BlockSpec((c, D_RWKV), lambda b, d, s: (rowblk(b, d, s), 0))
    perdir = pl.BlockSpec((None, c, D_RWKV), lambda b, d, s: (d, rowblk(b, d, s), 0))
    return pl.pallas_call(
        functools.partial(_wkv_kernel, c=c),
        out_shape=jax.ShapeDtypeStruct((2, rows, D_RWKV), F32),
        grid=(geo.b, 2, n_ctx + n_lat),
        in_specs=[shared, shared, shared, perdir, perdir, perdir],
        out_specs=perdir,
        scratch_shapes=[pltpu.VMEM((RWKV_HEADS // 2, PAIR, PAIR), F32)],
        compiler_params=_cparams("parallel", "parallel", "arbitrary"),
        name="wkv_scan",
    )(v, r, kkn, lw, kd, bb)


def _rwkv_out_kernel(y_ref, r_ref, v_ref, kd_ref, gd_ref, lg_ref, lb_ref, rk_ref, gl_ref, hsum_ref,
                     o_ref):
    hsum = hsum_ref[...]
    inv_n = 1.0 / RWKV_HEAD_DIM
    y = y_ref[0] + y_ref[1]
    mu = _split_dot(y, hsum) * inv_n
    dlt = y - mu
    var = _split_dot(dlt * dlt, hsum) * inv_n
    yn = dlt * lax.rsqrt(var + GN_EPS) * lg_ref[...] + lb_ref[...]
    k_bonus = 0.5 * (kd_ref[0].astype(F32) + kd_ref[1].astype(F32))
    rsum = _split_dot(r_ref[...].astype(F32) * k_bonus * rk_ref[...], hsum)
    bonus = rsum * v_ref[...].astype(F32)
    gate = _dot(_sigmoid(gd_ref[...].astype(F32)).astype(BF16), gl_ref[...])
    o_ref[...] = ((yn + bonus) * gate).astype(BF16)


def _rwkv_output(y, r, v, kd, z, p, geo):
    rows = r.shape[0]
    t = geo.tp
    gcb = (geo.col0 + OFF_GD) // GATE_LORA
    row_spec = pl.BlockSpec((t, D_RWKV), lambda i: (i, 0))
    dir_spec = pl.BlockSpec((2, t, D_RWKV), lambda i: (0, i, 0))
    vec = pl.BlockSpec((1, D_RWKV), lambda i: (0, 0))
    return pl.pallas_call(
        _rwkv_out_kernel,
        out_shape=jax.ShapeDtypeStruct((rows, D_RWKV), BF16),
        grid=(rows // t,),
        in_specs=[dir_spec, row_spec, row_spec, dir_spec,
                  pl.BlockSpec((t, GATE_LORA), lambda i: (i, gcb)),
                  vec, vec, vec,
                  pl.BlockSpec((GATE_LORA, D_RWKV), lambda i: (0, 0)),
                  pl.BlockSpec((D_RWKV, D_RWKV), lambda i: (0, 0))],
        out_specs=row_spec,
        compiler_params=_cparams("parallel"),
        name="rwkv_output",
    )(y, r, v, kd, z, p['lnx_g'], p['lnx_b'], p['r_k'], p['g_lora'], geo.hsum)


def _attn_prep_kernel(q_ref, k_ref, cos_ref, sin_ref, qg_ref, kg_ref, qo_ref, ko_ref):
    cos = cos_ref[...]
    sin = sin_ref[...]
    lane = lax.broadcasted_iota(jnp.int32, cos.shape, 1)
    low_half = (lane % (2 * AXIS_FREQS)) < AXIS_FREQS

    def norm_rope(x, g, scale):
        x = x.astype(F32)
        xn = x * lax.rsqrt(jnp.mean(x * x, axis=-1, keepdims=True) + NORM_EPS) * g
        partner = jnp.where(low_half, pltpu.roll(xn, ATTN_HEAD_DIM - AXIS_FREQS, 1),
                            pltpu.roll(xn, AXIS_FREQS, 1))
        return ((xn * cos + partner * sin) * scale).astype(BF16)

    for h in range(ATTN_HEADS):
        sl = slice(h * ATTN_HEAD_DIM, (h + 1) * ATTN_HEAD_DIM)
        qo_ref[:, sl] = norm_rope(q_ref[:, sl], qg_ref[...], ATTN_SCALE)
    for h in range(ATTN_KV_HEADS):
        sl = slice(h * ATTN_HEAD_DIM, (h + 1) * ATTN_HEAD_DIM)
        ko_ref[:, sl] = norm_rope(k_ref[:, sl], kg_ref[...], 1.0)


def _attn_prep(z, cos_t, sin_t, p, geo):
    rows = z.shape[0]
    t = geo.tp
    qcb = (geo.col0 + OFF_Q) // D_ATTN
    kcb = (geo.col0 + OFF_KA) // D_ATTN_KV
    lat_tiles, lat_per = geo.r_lat // t, geo.s // t

    def tab(i):
        return (jnp.where(i < lat_tiles, i % lat_per, lat_per), 0)

    vec = pl.BlockSpec((1, ATTN_HEAD_DIM), lambda i: (0, 0))
    return pl.pallas_call(
        _attn_prep_kernel,
        out_shape=(jax.ShapeDtypeStruct((rows, D_ATTN), BF16),
                   jax.ShapeDtypeStruct((rows, D_ATTN_KV), BF16)),
        grid=(rows // t,),
        in_specs=[pl.BlockSpec((t, D_ATTN), lambda i: (i, qcb)),
                  pl.BlockSpec((t, D_ATTN_KV), lambda i: (i, kcb)),
                  pl.BlockSpec((t, ATTN_HEAD_DIM), tab),
                  pl.BlockSpec((t, ATTN_HEAD_DIM), tab),
                  vec, vec],
        out_specs=(pl.BlockSpec((t, D_ATTN), lambda i: (i, 0)),
                   pl.BlockSpec((t, D_ATTN_KV), lambda i: (i, 0))),
        compiler_params=_cparams("parallel"),
        name="attn_prep",
    )(z, z, cos_t, sin_t, p['q_norm_g'], p['k_norm_g'])


def _attn_kernel(*refs, n_seg):
    q_ref = refs[0]
    k_refs = refs[1:1 + n_seg]
    v_refs = refs[1 + n_seg:1 + 2 * n_seg]
    o_ref = refs[1 + 2 * n_seg]
    for g in range(GQA_GROUP):
        sl = slice(g * ATTN_HEAD_DIM, (g + 1) * ATTN_HEAD_DIM)
        q = q_ref[:, sl]
        scores = [_dot_nt(q, k_ref[...]) for k_ref in k_refs]
        m = scores[0].max(axis=-1, keepdims=True)
        for s in scores[1:]:
            m = jnp.maximum(m, s.max(axis=-1, keepdims=True))
        denom = None
        acc = None
        for s, v_ref in zip(scores, v_refs):
            e = jnp.exp(s - m)
            es = e.sum(axis=-1, keepdims=True)
            pv = _dot(e.astype(BF16), v_ref[...])
            denom = es if denom is None else denom + es
            acc = pv if acc is None else acc + pv
        o_ref[:, sl] = (acc / denom).astype(BF16)


def _attention(qn, kn, z, geo, latent):
    gw = GQA_GROUP * ATTN_HEAD_DIM
    vcb = (geo.col0 + OFF_VA) // ATTN_HEAD_DIM
    ctx_blk0 = geo.r_lat // geo.lc
    k_ctx = pl.BlockSpec((geo.lc, ATTN_HEAD_DIM), lambda b, h, i: (ctx_blk0 + b, h))
    v_ctx = pl.BlockSpec((geo.lc, ATTN_HEAD_DIM), lambda b, h, i: (ctx_blk0 + b, vcb + h))
    if latent:
        tq = geo.tq
        per = geo.s // tq
        q_spec = pl.BlockSpec((tq, gw), lambda b, h, i: (b * per + i, h))
        k_lat = pl.BlockSpec((geo.s, ATTN_HEAD_DIM), lambda b, h, i: (b, h))
        v_lat = pl.BlockSpec((geo.s, ATTN_HEAD_DIM), lambda b, h, i: (b, vcb + h))
        in_specs = [q_spec, k_ctx, k_lat, v_ctx, v_lat]
        args = (qn, kn, kn, z, z)
        n_seg, out_rows = 2, geo.r_lat
        o_spec = q_spec
    else:
        tq, per = geo.lc, 1
        q_spec = pl.BlockSpec((tq, gw), lambda b, h, i: (ctx_blk0 + b, h))
        in_specs = [q_spec, k_ctx, v_ctx]
        args = (qn, kn, z)
        n_seg, out_rows = 1, geo.r_ctx
        o_spec = pl.BlockSpec((tq, gw), lambda b, h, i: (b, h))
    return pl.pallas_call(
        functools.partial(_attn_kernel, n_seg=n_seg),
        out_shape=jax.ShapeDtypeStruct((out_rows, D_ATTN), BF16),
        grid=(geo.b, ATTN_KV_HEADS, per),
        in_specs=in_specs,
        out_specs=o_spec,
        compiler_params=_cparams("parallel", "parallel", "parallel"),
        name="attention_lat" if latent else "attention_ctx",
    )(*args)


def _mm_kernel(x_ref, w_ref, o_ref):
    o_ref[...] = _dot(x_ref[...], w_ref[...]).astype(o_ref.dtype)


def _fourier_channels(z, wcs, geo):
    rows = z.shape[0]
    t = geo.tp
    fcb = (geo.col0 + OFF_FOUR) // D_FOURIER
    return pl.pallas_call(
        _mm_kernel,
        out_shape=jax.ShapeDtypeStruct((rows, 2 * D_FOURIER), BF16),
        grid=(rows // t,),
        in_specs=[pl.BlockSpec((t, D_FOURIER), lambda i: (i, fcb)),
                  pl.BlockSpec((D_FOURIER, 2 * D_FOURIER), lambda i: (0, 0))],
        out_specs=pl.BlockSpec((t, 2 * D_FOURIER), lambda i: (i, 0)),
        compiler_params=_cparams("parallel"),
        name="fourier_channels",
    )(z, wcs)


def _dft_kernel(cl_ref, sl_ref, xc_ref, xs_ref, o_ref):
    o_ref[...] = (_dot(cl_ref[...], xc_ref[...]) - _dot(sl_ref[...], xs_ref[...])).astype(BF16)


def _fourier_positions(xcs, cl, sl, n, blk0, nb, tmf):
    per = n // tmf
    return pl.pallas_call(
        _dft_kernel,
        out_shape=jax.ShapeDtypeStruct((nb * n, D_FOURIER), BF16),
        grid=(per, nb),
        in_specs=[pl.BlockSpec((tmf, n), lambda i, b: (i, 0)),
                  pl.BlockSpec((tmf, n), lambda i, b: (i, 0)),
                  pl.BlockSpec((n, D_FOURIER), lambda i, b: (blk0 + b, 0)),
                  pl.BlockSpec((n, D_FOURIER), lambda i, b: (blk0 + b, 1))],
        out_specs=pl.BlockSpec((tmf, D_FOURIER), lambda i, b: (b * per + i, 0)),
        compiler_params=_cparams("parallel", "parallel"),
        name="fourier_positions",
    )(cl, sl, xcs, xcs)


def _merge_kernel(rw_ref, fo_ref, at_ref, ga_ref, gf_ref, gc_ref, wr_ref, wf_ref, wa_ref, o_ref):
    y = _sigmoid(ga_ref[...].astype(F32)) * _dot(rw_ref[...], wr_ref[...])
    y += _sigmoid(gf_ref[...].astype(F32)) * _dot(fo_ref[...], wf_ref[...])
    y += _sigmoid(gc_ref[...].astype(F32)) * _dot(at_ref[...], wa_ref[...])
    o_ref[...] = y.astype(BF16)


def _merge(rw, fo, at, z, p, geo, n_rows):
    d = geo.d
    tm = geo.tm
    tn = min(1024, d)
    nj = d // tn
    return pl.pallas_call(
        _merge_kernel,
        out_shape=jax.ShapeDtypeStruct((n_rows, d), BF16),
        grid=(nj, n_rows // tm),
        in_specs=[pl.BlockSpec((tm, D_RWKV), lambda j, i: (i, 0)),
                  pl.BlockSpec((tm, D_FOURIER), lambda j, i: (i, 0)),
                  pl.BlockSpec((tm, D_ATTN), lambda j, i: (i, 0)),
                  pl.BlockSpec((tm, tn), lambda j, i: (i, j)),
                  pl.BlockSpec((tm, tn), lambda j, i: (i, nj + j)),
                  pl.BlockSpec((tm, tn), lambda j, i: (i, 2 * nj + j)),
                  pl.BlockSpec((D_RWKV, tn), lambda j, i: (0, j)),
                  pl.BlockSpec((D_FOURIER, tn), lambda j, i: (0, j)),
                  pl.BlockSpec((D_ATTN, tn), lambda j, i: (0, j))],
        out_specs=pl.BlockSpec((tm, tn), lambda j, i: (i, j)),
        compiler_params=_cparams("parallel", "parallel"),
        name="merge",
    )(rw, fo, at, z, z, z, p['w_br_rwkv'], p['w_br_fourier'], p['w_br_attn'])


def _outres_kernel(y_ref, w_ref, x_ref, gt_ref, o_ref):
    o_ref[...] = x_ref[...] + gt_ref[...] * _dot(y_ref[...], w_ref[...])


def _out_proj_residual(y, w, xs, mod4, geo, n_rows):
    d = geo.d
    tm = geo.tm
    tn = min(1024, d)
    bidx = geo.batch_of_tile(tm)
    return pl.pallas_call(
        _outres_kernel,
        out_shape=jax.ShapeDtypeStruct((n_rows, d), F32),
        grid=(d // tn, n_rows // tm),
        in_specs=[pl.BlockSpec((tm, d), lambda j, i: (i, 0)),
                  pl.BlockSpec((d, tn), lambda j, i: (0, j)),
                  pl.BlockSpec((tm, tn), lambda j, i: (i, j)),
                  pl.BlockSpec((None, None, 1, tn), lambda j, i: (bidx(i), 2, 0, j))],
        out_specs=pl.BlockSpec((tm, tn), lambda j, i: (i, j)),
        compiler_params=_cparams("parallel", "parallel"),
        name="out_proj_residual",
    )(y, w, xs, mod4)


def _expert_ffn(h, wgu, bgu, wd, bd, f):
    gu = _dot(h, wgu) + bgu
    gate = jnp.minimum(gu[:, :f], SWIGLU_LIMIT)
    up = jnp.clip(gu[:, f:], -SWIGLU_LIMIT, SWIGLU_LIMIT)
    act = (up + 1.0) * gate * _sigmoid(SWIGLU_ALPHA * gate)
    return _dot(act.astype(BF16), wd) + bd


def _moe_dense_kernel(h_ref, g_ref, wgu_ref, bgu_ref, wd_ref, bd_ref, x_ref, gt_ref, o_ref, acc_ref,
                      *, f):
    e = pl.program_id(1)

    @pl.when(e == 0)
    def _():
        acc_ref[...] = jnp.zeros_like(acc_ref)

    y = _expert_ffn(h_ref[...], wgu_ref[...], bgu_ref[...], wd_ref[...], bd_ref[...], f)
    g = g_ref[...]
    lane = lax.broadcasted_iota(jnp.int32, g.shape, 1)
    ge = jnp.sum(jnp.where(lane == e, g, 0.0), axis=-1, keepdims=True)
    acc_ref[...] += ge * y

    @pl.when(e == pl.num_programs(1) - 1)
    def _():
        o_ref[...] = x_ref[...] + gt_ref[...] * acc_ref[...]


def _moe_dense(h2, gates, wgu, bgu, wd, bd, xs, mod4, geo, n_rows):
    d = geo.d
    n_exp, _, f2 = wgu.shape
    f = f2 // 2
    tm = min(512, geo.tm)
    bidx = geo.batch_of_tile(tm)
    return pl.pallas_call(
        functools.partial(_moe_dense_kernel, f=f),
        out_shape=jax.ShapeDtypeStruct((n_rows, d), F32),
        grid=(n_rows // tm, n_exp),
        in_specs=[pl.BlockSpec((tm, d), lambda i, e: (i, 0)),
                  pl.BlockSpec((tm, ROUTER_PAD), lambda i, e: (i, 0)),
                  pl.BlockSpec((None, d, f2), lambda i, e: (e, 0, 0)),
                  pl.BlockSpec((None, 1, f2), lambda i, e: (e, 0, 0)),
                  pl.BlockSpec((None, f, d), lambda i, e: (e, 0, 0)),
                  pl.BlockSpec((None, 1, d), lambda i, e: (e, 0, 0)),
                  pl.BlockSpec((tm, d), lambda i, e: (i, 0)),
                  pl.BlockSpec((None, None, 1, d), lambda i, e: (bidx(i), 5, 0, 0))],
        out_specs=pl.BlockSpec((tm, d), lambda i, e: (i, 0)),
        scratch_shapes=[pltpu.VMEM((tm, d), F32)],
        compiler_params=_cparams("parallel", "arbitrary"),
        name="moe_dense",
    )(h2, gates, wgu, bgu, wd, bd, xs, mod4)


class _Geometry:
    def __init__(self, b, s, lc, d):
        self.b, self.s, self.lc, self.d = b, s, lc, d
        self.r_lat, self.r_ctx = b * s, b * lc
        self.rows = self.r_lat + self.r_ctx
        self.col0 = 3 * d
        assert self.col0 % 1024 == 0, "gate columns must end on a 1024-column boundary"
        self.tm = min(1024, s, self.r_ctx)
        self.tp = min(256, s, lc)
        self.tq = min(512, s)
        self.chunk = min(WKV_CHUNK, lc, s)
        for t in (self.tm,):
            assert s % t == 0 and self.r_ctx % t == 0
        assert s % self.tp == 0 and lc % self.tp == 0 and self.tp % BF16_SUBLANES == 0
        assert s % self.chunk == 0 and lc % self.chunk == 0 and s % GRID_W == 0
        self.hsum = jnp.asarray(np.kron(np.eye(RWKV_HEADS), np.ones((RWKV_HEAD_DIM,) * 2)), BF16)

    def batch_of_tile(self, tm):
        per, nb = self.s // tm, self.b
        return lambda i: jnp.minimum(i // per, nb)


def _dft_tables(n):
    j = np.arange(n, dtype=np.int64)
    ang = 2.0 * np.pi * ((j[:, None] * j[None, :]) % n).astype(np.float64) / n
    return np.cos(ang) / np.sqrt(n), np.sin(ang) / np.sqrt(n)


def _rope_tables(s, t):
    pos = np.arange(s)
    inv_freq = ROPE_THETA ** (-np.arange(AXIS_FREQS, dtype=np.float32) / AXIS_FREQS)
    ang_r = (pos // GRID_W).astype(np.float32)[:, None] * inv_freq.astype(np.float32)
    ang_c = (pos % GRID_W).astype(np.float32)[:, None] * inv_freq.astype(np.float32)
    cr, sr, cc, sc = np.cos(ang_r), np.sin(ang_r), np.cos(ang_c), np.sin(ang_c)
    cos = np.concatenate([cr, cr, cc, cc], axis=1)
    sin = np.concatenate([-sr, sr, -sc, sc], axis=1)
    cos = np.concatenate([cos, np.ones((t, ATTN_HEAD_DIM))], axis=0)
    sin = np.concatenate([sin, np.zeros((t, ATTN_HEAD_DIM))], axis=0)
    return jnp.asarray(cos, F32), jnp.asarray(sin, F32)


def _permute_w_in(w_in, d):
    o = np.cumsum([0, D_RWKV, D_RWKV, LORA, LORA, LORA, LORA, D_ATTN_KV, D_ATTN_KV, D_RWKV, GATE_LORA,
                   D_ATTN, D_FOURIER, 3 * d])
    seg = lambda i: w_in[..., o[i]:o[i + 1]]
    pad = lambda a: jnp.pad(a, ((0, 0), (0, 0), (0, LORA_PAD - LORA)))
    parts = [seg(12), seg(0), seg(1), pad(seg(2)), pad(seg(3)), pad(seg(4)), pad(seg(5)), seg(8), seg(10),
             seg(11), seg(6), seg(7), seg(9)]
    w = jnp.concatenate(parts, axis=-1)
    n = w.shape[-1]
    n_pad = -(-n // W_IN_TN) * W_IN_TN
    return jnp.pad(w, ((0, 0), (0, 0), (0, n_pad - n))).astype(BF16)


def _split_hi_lo(w):
    hi = w.astype(BF16)
    return hi, (w - hi.astype(F32)).astype(BF16)


def kernel(x, c, ctx, c_ctx, ada_w, ada_b, norm1_g, norm2_g, w_in, rwkv_conv, w0, w_lora, a0, a_lora, g_lora, k_k, k_a, r_k, lnx_g, lnx_b, q_norm_g, k_norm_g, w_br_rwkv, w_br_fourier, w_br_attn, w_out, router_w, router_b, exp_w_gu, exp_b_gu, exp_w_down, exp_b_down):
    b, s, d = x.shape
    lc = ctx.shape[1]
    depth = w_in.shape[0]
    n_exp = router_w.shape[-1]
    geo = _Geometry(b, s, lc, d)

    w_in_p = _permute_w_in(w_in, d)
    lora_pad = ((0, 0), (0, 0), (0, LORA_PAD - LORA), (0, 0))
    w_lora_p = jnp.pad(w_lora, lora_pad).astype(BF16)
    a_lora_p = jnp.pad(a_lora, lora_pad).astype(BF16)
    g_lora_b = g_lora.astype(BF16)
    wbr_r, wbr_f, wbr_a, w_out_b = (w.astype(BF16) for w in (w_br_rwkv, w_br_fourier, w_br_attn, w_out))
    rw_p = jnp.pad(router_w, ((0, 0), (0, 0), (0, ROUTER_PAD - n_exp)))
    rw_hi, rw_lo = _split_hi_lo(rw_p)
    rb_p = jnp.pad(router_b, ((0, 0), (0, ROUTER_PAD - n_exp)), constant_values=NEG_BIG)
    wgu_b, wd_b = exp_w_gu.astype(BF16), exp_w_down.astype(BF16)
    cos_t, sin_t = _rope_tables(s, geo.tp)
    cc, sc = _dft_tables(FOURIER_GROUP_DIM)
    groups = D_FOURIER // FOURIER_GROUP_DIM
    wcs = jnp.asarray(np.concatenate([np.kron(np.eye(groups), cc), np.kron(np.eye(groups), sc)], axis=1), BF16)
    cl_lat, sl_lat = (jnp.asarray(m, BF16) for m in _dft_tables(s))
    cl_ctx, sl_ctx = (jnp.asarray(m, BF16) for m in _dft_tables(lc))

    mod_rows = -(-(b + 1) // BF16_SUBLANES) * BF16_SUBLANES
    c_all = jnp.concatenate([c, c_ctx[None], jnp.zeros((mod_rows - b - 1, d), F32)], axis=0)
    mod = _adaln(c_all, ada_w, ada_b).reshape(depth, mod_rows, 6, 1, d)

    xs = jnp.concatenate([x.reshape(b * s, d), ctx.reshape(b * lc, d)], axis=0)
    for l in range(depth):
        last = l == depth - 1
        mod4 = mod[l]
        p = dict(conv=rwkv_conv[l], w0=w0[l][:, None, :], w_lora=w_lora_p[l], a0=a0[l][:, None, :],
                 a_lora=a_lora_p[l], k_k=k_k[l][None], k_a=k_a[l][None], r_k=r_k[l][None],
                 lnx_g=lnx_g[l][None], lnx_b=lnx_b[l][None], g_lora=g_lora_b[l],
                 q_norm_g=q_norm_g[l][None], k_norm_g=k_norm_g[l][None],
                 w_br_rwkv=wbr_r[l], w_br_fourier=wbr_f[l], w_br_attn=wbr_a[l])
        n_rows = geo.r_lat if last else geo.rows

        z = _norm_in_proj(xs, norm1_g[l], mod4, w_in_p[l], geo)

        v_c, r_c, kkn, lw, kd, bb = _rwkv_prepare(z, p, geo)
        y = _wkv_scan(v_c, r_c, kkn, lw, kd, bb, geo)
        rw = _rwkv_output(y, r_c, v_c, kd, z, p, geo)

        qn, kn = _attn_prep(z, cos_t, sin_t, p, geo)
        att = _attention(qn, kn, z, geo, latent=True)

        xcs = _fourier_channels(z, wcs, geo)
        fo = _fourier_positions(xcs, cl_lat, sl_lat, s, 0, b, min(1024, s))
        if not last:
            att = jnp.concatenate([att, _attention(qn, kn, z, geo, latent=False)], axis=0)
            fo_c = _fourier_positions(xcs, cl_ctx, sl_ctx, lc, geo.r_lat // lc, b, lc)
            fo = jnp.concatenate([fo, fo_c], axis=0)

        ym = _merge(rw, fo, att, z, p, geo, n_rows)
        xs = _out_proj_residual(ym, w_out_b[l], xs, mod4, geo, n_rows)

        h2, gates = _norm_router(xs, norm2_g[l], mod4, rw_hi[l], rw_lo[l], rb_p[l][None], geo, n_rows)
        xs = _moe_dense(h2, gates, wgu_b[l], exp_b_gu[l][:, None, :], wd_b[l], exp_b_down[l][:, None, :],
                        xs, mod4, geo, n_rows)
    return xs[:geo.r_lat].reshape(b, s, d)
```

```python
import functools

import numpy as np
import jax
import jax.numpy as jnp
from jax import lax
from jax.experimental import pallas as pl
from jax.experimental.pallas import tpu as pltpu

F32 = jnp.float32
BF16 = jnp.bfloat16

NORM_EPS = 1e-6
GN_EPS = 64e-5
RWKV_HEADS = 8
RWKV_HEAD_DIM = 64
D_RWKV = RWKV_HEADS * RWKV_HEAD_DIM
LORA = 96
GATE_LORA = 256
D_FOURIER = 512
FOURIER_GROUP_DIM = 128
ATTN_HEADS = 8
ATTN_KV_HEADS = 2
ATTN_HEAD_DIM = 128
GQA_GROUP = ATTN_HEADS // ATTN_KV_HEADS
D_ATTN = ATTN_HEADS * ATTN_HEAD_DIM
D_ATTN_KV = ATTN_KV_HEADS * ATTN_HEAD_DIM
ATTN_SCALE = ATTN_HEAD_DIM ** -0.5
GRID_W = 64
ROPE_THETA = 10000.0
AXIS_FREQS = ATTN_HEAD_DIM // 4
TOP_K = 4
SWIGLU_LIMIT = 7.0
SWIGLU_ALPHA = 1.702

LANES = 128
BF16_SUBLANES = 16
VMEM_LIMIT = 56 * 1024 * 1024

LORA_PAD = LANES
W_IN_TN = 1536
WKV_CHUNK = 64
PAIR = 2 * RWKV_HEAD_DIM
ROUTER_PAD = LANES
NEG_BIG = -1e30

OFF_K, OFF_V, OFF_LORA, OFF_R, OFF_Q, OFF_FOUR, OFF_KA, OFF_VA, OFF_GD, OFF_END = (
    0, 512, 1024, 1536, 2048, 3072, 3584, 3840, 4096, 4352)


def _cparams(*sem):
    return pltpu.CompilerParams(dimension_semantics=sem, vmem_limit_bytes=VMEM_LIMIT)


def _dot(a, b):
    return jnp.dot(a, b, preferred_element_type=F32)


def _dot_nt(a, b):
    return lax.dot_general(a, b, (((1,), (1,)), ((), ())), preferred_element_type=F32)


def _dot_tn(a, b):
    return lax.dot_general(a, b, (((0,), (0,)), ((), ())), preferred_element_type=F32)


def _split_dot(x, g):
    hi = x.astype(BF16)
    lo = (x - hi.astype(F32)).astype(BF16)
    return _dot(hi, g) + _dot(lo, g)


def _sigmoid(x):
    return 1.0 / (1.0 + jnp.exp(-x))


def _adaln_kernel(c_ref, w_ref, b_ref, o_ref):
    c = c_ref[...]
    s = (c * _sigmoid(c)).astype(BF16)
    o_ref[...] = _dot(s, w_ref[...].astype(BF16)) + b_ref[...]


def _adaln(c_all, ada_w, ada_b):
    depth, d, n = ada_w.shape
    rows = c_all.shape[0]
    tn = 1024 if n % 1024 == 0 else n
    return pl.pallas_call(
        _adaln_kernel,
        out_shape=jax.ShapeDtypeStruct((depth, rows, n), F32),
        grid=(depth, n // tn),
        in_specs=[
            pl.BlockSpec((rows, d), lambda l, j: (0, 0)),
            pl.BlockSpec((None, d, tn), lambda l, j: (l, 0, j)),
            pl.BlockSpec((None, 1, tn), lambda l, j: (l, 0, j)),
        ],
        out_specs=pl.BlockSpec((None, rows, tn), lambda l, j: (l, 0, j)),
        compiler_params=_cparams("parallel", "parallel"),
        name="adaln",
    )(c_all, ada_w, ada_b.reshape(depth, 1, n))


def _modulated_norm(x, g, sc, sh):
    ms = jnp.mean(x * x, axis=-1, keepdims=True)
    return x * lax.rsqrt(ms + NORM_EPS) * g * (1.0 + sc) + sh


def _normmm_kernel(x_ref, g_ref, sc_ref, sh_ref, w_ref, o_ref, h_ref):
    @pl.when(pl.program_id(1) == 0)
    def _():
        h_ref[...] = _modulated_norm(x_ref[...], g_ref[...], sc_ref[...], sh_ref[...]).astype(BF16)

    o_ref[...] = _dot(h_ref[...], w_ref[...]).astype(o_ref.dtype)


def _norm_in_proj(xs, gain, mod4, w, geo):
    rows, d = xs.shape
    n = w.shape[1]
    tm, tn = geo.tm, W_IN_TN
    bidx = geo.batch_of_tile(tm)
    return pl.pallas_call(
        _normmm_kernel,
        out_shape=jax.ShapeDtypeStruct((rows, n), BF16),
        grid=(rows // tm, n // tn),
        in_specs=[
            pl.BlockSpec((tm, d), lambda i, j: (i, 0)),
            pl.BlockSpec((1, d), lambda i, j: (0, 0)),
            pl.BlockSpec((None, None, 1, d), lambda i, j: (bidx(i), 1, 0, 0)),
            pl.BlockSpec((None, None, 1, d), lambda i, j: (bidx(i), 0, 0, 0)),
            pl.BlockSpec((d, tn), lambda i, j: (0, j)),
        ],
        out_specs=pl.BlockSpec((tm, tn), lambda i, j: (i, j)),
        scratch_shapes=[pltpu.VMEM((tm, d), BF16)],
        compiler_params=_cparams("parallel", "arbitrary"),
        name="norm_in_proj",
    )(xs, gain.reshape(1, d), mod4, mod4, w)


def _norm_router_kernel(x_ref, g_ref, sc_ref, sh_ref, wh_ref, wl_ref, rb_ref, h_ref, sel_ref):
    h = _modulated_norm(x_ref[...], g_ref[...], sc_ref[...], sh_ref[...])
    h_ref[...] = h
    hi = h.astype(BF16)
    lo = (h - hi.astype(F32)).astype(BF16)
    logits = _dot(hi, wh_ref[...]) + _dot(lo, wh_ref[...]) + _dot(hi, wl_ref[...]) + rb_ref[...]
    lane = lax.broadcasted_iota(jnp.int32, logits.shape, 1).astype(F32)
    work = logits
    vals, firsts = [], []
    for _ in range(TOP_K):
        m = jnp.max(work, axis=-1, keepdims=True)
        first = jnp.min(jnp.where(work == m, lane, float(ROUTER_PAD)), axis=-1, keepdims=True)
        vals.append(m)
        firsts.append(first)
        work = jnp.where(lane == first, 2.0 * NEG_BIG, work)
    exps = [jnp.exp(v - vals[0]) for v in vals]
    denom = exps[0] + exps[1] + exps[2] + exps[3]
    table = jnp.zeros_like(logits)
    for k in range(TOP_K):
        table = jnp.where(lane == float(k), firsts[k], table)
        table = jnp.where(lane == float(TOP_K + k), exps[k] / denom, table)
    sel_ref[...] = table


def _norm_router(xs, gain, mod4, rw_hi, rw_lo, rb, geo, n_rows):
    d = xs.shape[1]
    tm = geo.tm
    bidx = geo.batch_of_tile(tm)
    return pl.pallas_call(
        _norm_router_kernel,
        out_shape=(jax.ShapeDtypeStruct((n_rows, d), F32),
                   jax.ShapeDtypeStruct((n_rows, ROUTER_PAD), F32)),
        grid=(n_rows // tm,),
        in_specs=[
            pl.BlockSpec((tm, d), lambda i: (i, 0)),
            pl.BlockSpec((1, d), lambda i: (0, 0)),
            pl.BlockSpec((None, None, 1, d), lambda i: (bidx(i), 4, 0, 0)),
            pl.BlockSpec((None, None, 1, d), lambda i: (bidx(i), 3, 0, 0)),
            pl.BlockSpec((d, ROUTER_PAD), lambda i: (0, 0)),
            pl.BlockSpec((d, ROUTER_PAD), lambda i: (0, 0)),
            pl.BlockSpec((1, ROUTER_PAD), lambda i: (0, 0)),
        ],
        out_specs=(pl.BlockSpec((tm, d), lambda i: (i, 0)),
                   pl.BlockSpec((tm, ROUTER_PAD), lambda i: (i, 0))),
        compiler_params=_cparams("parallel"),
        name="norm_router",
    )(xs, gain.reshape(1, d), mod4, mod4, rw_hi, rw_lo, rb)


def _prep_kernel(zk_ref, zv_ref, zl_ref, zr_ref, pk_ref, pv_ref, pr_ref, nk_ref, nv_ref, nr_ref,
                 conv_ref, w0_ref, wl_ref, a0_ref, al_ref, kk_ref, ka_ref, hsum_ref,
                 v_out, r_out, kkn_out, lw_out, kd_out, bb_out, *, t, lat_tiles, lat_per, ctx_per):
    i = pl.program_id(0)
    is_lat = i < lat_tiles
    per = jnp.where(is_lat, lat_per, ctx_per)
    j = jnp.where(is_lat, i, i - lat_tiles) % per
    first = j == 0
    last = j == per - 1
    row = lax.broadcasted_iota(jnp.int32, (t, 1), 0)

    def conv(z_ref, p_ref, n_ref, which):
        z = z_ref[...].astype(F32)
        prev_row = p_ref[...].astype(F32)[BF16_SUBLANES - 1:BF16_SUBLANES, :]
        next_row = n_ref[...].astype(F32)[0:1, :]
        prev_row = jnp.where(first, 0.0, prev_row)
        next_row = jnp.where(last, 0.0, next_row)
        zm = jnp.where(row == 0, prev_row, pltpu.roll(z, 1, 0))
        zp = jnp.where(row == t - 1, next_row, pltpu.roll(z, t - 1, 0))
        w = conv_ref[which]
        return zm * w[0:1] + z * w[1:2] + zp * w[2:3]

    k = conv(zk_ref, pk_ref, nk_ref, 0)
    v = conv(zv_ref, pv_ref, nv_ref, 1)
    r = conv(zr_ref, pr_ref, nr_ref, 2)
    v_out[...] = v.astype(BF16)
    r_out[...] = r.astype(BF16)
    kkv = k * kk_ref[...]
    ss = _split_dot(kkv * kkv, hsum_ref[...])
    kkn = kkv * lax.rsqrt(jnp.maximum(ss, 1e-24))
    kkn_out[...] = kkn.astype(BF16)
    zl = zl_ref[...]
    for d in range(2):
        wd = zl[:, d * LORA_PAD:(d + 1) * LORA_PAD].astype(F32)
        w_raw = w0_ref[d] + _dot(jnp.tanh(wd).astype(BF16), wl_ref[d])
        lw_out[d] = -_sigmoid(w_raw) * float(np.exp(-0.5))
        ad = zl[:, (2 + d) * LORA_PAD:(3 + d) * LORA_PAD]
        a = _sigmoid(a0_ref[d] + _dot(ad, al_ref[d]))
        kd_out[d] = (k * (1.0 + (a - 1.0) * ka_ref[...])).astype(BF16)
        bb_out[d] = (kkn * a).astype(BF16)


def _rwkv_prepare(z, p, geo):
    rows = z.shape[0]
    t = geo.tp
    base = geo.col0
    hb = t // BF16_SUBLANES
    n_halo = rows // BF16_SUBLANES

    def zcol(off):
        cb = (base + off) // D_RWKV
        return pl.BlockSpec((t, D_RWKV), lambda i: (i, cb))

    def prev(off):
        cb = (base + off) // D_RWKV
        return pl.BlockSpec((BF16_SUBLANES, D_RWKV), lambda i: (jnp.maximum(i * hb - 1, 0), cb))

    def nxt(off):
        cb = (base + off) // D_RWKV
        return pl.BlockSpec((BF16_SUBLANES, D_RWKV),
                            lambda i: (jnp.minimum((i + 1) * hb, n_halo - 1), cb))

    def full(a):
        nd = a.ndim
        return pl.BlockSpec(a.shape, lambda i: (0,) * nd)

    params = (p['conv'], p['w0'], p['w_lora'], p['a0'], p['a_lora'], p['k_k'], p['k_a'], geo.hsum)
    row_spec = pl.BlockSpec((t, D_RWKV), lambda i: (i, 0))
    dir_spec = pl.BlockSpec((2, t, D_RWKV), lambda i: (0, i, 0))
    kern = functools.partial(_prep_kernel, t=t, lat_tiles=geo.r_lat // t, lat_per=geo.s // t,
                             ctx_per=geo.lc // t)
    return pl.pallas_call(
        kern,
        out_shape=(jax.ShapeDtypeStruct((rows, D_RWKV), BF16),
                   jax.ShapeDtypeStruct((rows, D_RWKV), BF16),
                   jax.ShapeDtypeStruct((rows, D_RWKV), BF16),
                   jax.ShapeDtypeStruct((2, rows, D_RWKV), F32),
                   jax.ShapeDtypeStruct((2, rows, D_RWKV), BF16),
                   jax.ShapeDtypeStruct((2, rows, D_RWKV), BF16)),
        grid=(rows // t,),
        in_specs=[zcol(OFF_K), zcol(OFF_V), zcol(OFF_LORA), zcol(OFF_R),
                  prev(OFF_K), prev(OFF_V), prev(OFF_R), nxt(OFF_K), nxt(OFF_V), nxt(OFF_R)]
                 + [full(a) for a in params],
        out_specs=(row_spec, row_spec, row_spec, dir_spec, dir_spec, dir_spec),
        compiler_params=_cparams("parallel"),
        name="rwkv_prepare",
    )(z, z, z, z, z, z, z, z, z, z, *params)


def _wkv_kernel(vf_ref, rf_ref, kf_ref, lwf_ref, kdf_ref, bbf_ref,
                vb_ref, rb_ref, kb_ref, lwb_ref, kdb_ref, bbb_ref, yf_ref, yb_ref, s_ref, *, c):
    @pl.when(pl.program_id(1) == 0)
    def _():
        s_ref[...] = jnp.zeros_like(s_ref)

    c2 = 2 * c
    n_pairs = RWKV_HEADS // 2
    ti = lax.broadcasted_iota(jnp.int32, (c, c), 0)
    ii = lax.broadcasted_iota(jnp.int32, (c, c), 1)
    rt = lax.broadcasted_iota(jnp.int32, (c2, c2), 0)
    ci = lax.broadcasted_iota(jnp.int32, (c2, c2), 1)
    same = (rt // c) == (ci // c)
    tri, strict, incl = [], [], []
    for sign in (1, -1):
        tri.append(jnp.where((ti - ii) * sign >= 0, 1.0, 0.0).astype(BF16))
        before = (rt % c - ci % c) * sign
        strict.append(same & (before > 0))
        incl.append(same & (before >= 0))
    head0 = lax.broadcasted_iota(jnp.int32, (c, PAIR), 1) < RWKV_HEAD_DIM
    eye = (lax.broadcasted_iota(jnp.int32, (PAIR, PAIR), 0)
           == lax.broadcasted_iota(jnp.int32, (PAIR, PAIR), 1)).astype(F32)

    def stack(x):
        return jnp.concatenate([jnp.where(head0, x, 0.0), jnp.where(head0, 0.0, x)], axis=0).astype(BF16)

    refs = ((vf_ref, rf_ref, kf_ref, lwf_ref, kdf_ref, bbf_ref, yf_ref),
            (vb_ref, rb_ref, kb_ref, lwb_ref, kdb_ref, bbb_ref, yb_ref))
    chains = [(d, p) for d in range(2) for p in range(n_pairs)]
    sls = [slice(p * PAIR, (p + 1) * PAIR) for _, p in chains]
    n = len(chains)

    lws = [refs[d][3][:, sl] for (d, _), sl in zip(chains, sls)]
    cums = []
    for (d, _), lw in zip(chains, lws):
        hi = lw.astype(BF16)
        lo = (lw - hi.astype(F32)).astype(BF16)
        cums.append(_dot(tri[d], hi) + _dot(tri[d], lo))
    g_end = [jnp.exp(jnp.sum(lw, axis=0, keepdims=True)) for lw in lws]
    a_s, b_s, k_s, r_s, v_s = [], [], [], [], []
    for (d, _), sl, lw, cum in zip(chains, sls, lws, cums):
        v_ref, r_ref, kk_ref, _, kd_ref, bb_ref, _ = refs[d]
        g_inv = jnp.exp(-cum)
        a_s.append(stack(-kk_ref[:, sl].astype(F32) * jnp.exp(cum - lw)))
        b_s.append(stack(bb_ref[:, sl].astype(F32) * g_inv))
        k_s.append(stack(kd_ref[:, sl].astype(F32) * g_inv))
        r_s.append(stack(r_ref[:, sl].astype(F32) * jnp.exp(cum)))
        v_s.append(stack(v_ref[:, sl].astype(F32)))
    gs = [_dot_nt(jnp.concatenate([a_s[i], r_s[i]], axis=0), jnp.concatenate([b_s[i], k_s[i]], axis=0))
          for i in range(n)]
    l_ab = [jnp.where(strict[d], g[:c2, :c2], 0.0).astype(BF16) for (d, _), g in zip(chains, gs)]
    l_ak = [jnp.where(strict[d], g[:c2, c2:], 0.0).astype(BF16) for (d, _), g in zip(chains, gs)]
    m_rb = [jnp.where(incl[d], g[c2:, :c2], 0.0).astype(BF16) for (d, _), g in zip(chains, gs)]
    m_rk = [jnp.where(incl[d], g[c2:, c2:], 0.0).astype(BF16) for (d, _), g in zip(chains, gs)]

    xs = [jnp.concatenate([a_s[i].astype(F32), _dot(l_ak[i], v_s[i])], axis=1) for i in range(n)]
    lps = l_ab
    n_sq = int(np.log2(c))
    for sq in range(n_sq):
        xs = [x + _dot(lp, x.astype(BF16)) for x, lp in zip(xs, lps)]
        if sq + 1 < n_sq:
            lps = [_dot(lp, lp).astype(BF16) for lp in lps]
    xbs = [x.astype(BF16) for x in xs]
    mxs = [_dot(m_rb[i], xbs[i]) for i in range(n)]
    gqs = [(r_s[i].astype(F32) + mxs[i][:, :PAIR]).astype(BF16) for i in range(n)]
    yqs = [mxs[i][:, PAIR:] + _dot(m_rk[i], v_s[i]) for i in range(n)]
    wbs = [_dot_tn(xbs[i], b_s[i]) for i in range(n)]
    vks = [_dot_tn(v_s[i], k_s[i]) for i in range(n)]
    p_mats = [((eye + wbs[i][:PAIR]) * g_end[i]).astype(BF16) for i in range(n)]
    q_mats = [(wbs[i][PAIR:] + vks[i]) * g_end[i] for i in range(n)]
    s0b = [s_ref[i].astype(BF16) for i in range(n)]
    for i, ((d, _), sl) in enumerate(zip(chains, sls)):
        ys = _dot_nt(gqs[i], s0b[i]) + yqs[i]
        refs[d][6][:, sl] = ys[:c] + ys[c:]
        s_ref[i] = _dot(s0b[i], p_mats[i]) + q_mats[i]


def _wkv_scan(v, r, kkn, lw, kd, bb, geo):
    rows = v.shape[0]
    c = geo.chunk
    n_ctx, n_lat = geo.lc // c, geo.s // c
    lat_blocks = geo.r_lat // c

    def rowblk(d):
        def f(b, s):
            ctx_j = s if d == 0 else n_ctx - 1 - s
            lat_j = s - n_ctx if d == 0 else n_lat - 1 - (s - n_ctx)
            return jnp.where(s < n_ctx, lat_blocks + b * n_ctx + ctx_j, b * n_lat + lat_j)
        return f

    def shared(d):
        f = rowblk(d)
        return pl.BlockSpec((c, D_RWKV), lambda b, s: (f(b, s), 0))

    def perdir(d):
        f = rowblk(d)
        return pl.BlockSpec((None, c, D_RWKV), lambda b, s: (d, f(b, s), 0))

    in_specs = []
    for d in range(2):
        in_specs += [shared(d), shared(d), shared(d), perdir(d), perdir(d), perdir(d)]
    return pl.pallas_call(
        functools.partial(_wkv_kernel, c=c),
        out_shape=(jax.ShapeDtypeStruct((rows, D_RWKV), F32),) * 2,
        grid=(geo.b, n_ctx + n_lat),
        in_specs=in_specs,
        out_specs=(shared(0), shared(1)),
        scratch_shapes=[pltpu.VMEM((RWKV_HEADS, PAIR, PAIR), F32)],
        compiler_params=_cparams("parallel", "arbitrary"),
        name="wkv_scan",
    )(v, r, kkn, lw, kd, bb, v, r, kkn, lw, kd, bb)


def _rwkv_out_kernel(yf_ref, yb_ref, r_ref, v_ref, kd_ref, gd_ref, lg_ref, lb_ref, rk_ref, gl_ref,
                     hsum_ref, o_ref):
    hsum = hsum_ref[...]
    inv_n = 1.0 / RWKV_HEAD_DIM
    y = yf_ref[...] + yb_ref[...]
    mu = _split_dot(y, hsum) * inv_n
    dlt = y - mu
    var = _split_dot(dlt * dlt, hsum) * inv_n
    yn = dlt * lax.rsqrt(var + GN_EPS) * lg_ref[...] + lb_ref[...]
    k_bonus = 0.5 * (kd_ref[0].astype(F32) + kd_ref[1].astype(F32))
    rsum = _split_dot(r_ref[...].astype(F32) * k_bonus * rk_ref[...], hsum)
    bonus = rsum * v_ref[...].astype(F32)
    gate = _dot(_sigmoid(gd_ref[...].astype(F32)).astype(BF16), gl_ref[...])
    o_ref[...] = ((yn + bonus) * gate).astype(BF16)


def _rwkv_output(yf, yb, r, v, kd, z, p, geo):
    rows = r.shape[0]
    t = geo.tp
    gcb = (geo.col0 + OFF_GD) // GATE_LORA
    row_spec = pl.BlockSpec((t, D_RWKV), lambda i: (i, 0))
    dir_spec = pl.BlockSpec((2, t, D_RWKV), lambda i: (0, i, 0))
    vec = pl.BlockSpec((1, D_RWKV), lambda i: (0, 0))
    return pl.pallas_call(
        _rwkv_out_kernel,
        out_shape=jax.ShapeDtypeStruct((rows, D_RWKV), BF16),
        grid=(rows // t,),
        in_specs=[row_spec, row_spec, row_spec, row_spec, dir_spec,
                  pl.BlockSpec((t, GATE_LORA), lambda i: (i, gcb)),
                  vec, vec, vec,
                  pl.BlockSpec((GATE_LORA, D_RWKV), lambda i: (0, 0)),
                  pl.BlockSpec((D_RWKV, D_RWKV), lambda i: (0, 0))],
        out_specs=row_spec,
        compiler_params=_cparams("parallel"),
        name="rwkv_output",
    )(yf, yb, r, v, kd, z, p['lnx_g'], p['lnx_b'], p['r_k'], p['g_lora'], geo.hsum)


def _attn_prep_kernel(q_ref, k_ref, cos_ref, sin_ref, qg_ref, kg_ref, qo_ref, ko_ref):
    cos = cos_ref[...]
    sin = sin_ref[...]
    lane = lax.broadcasted_iota(jnp.int32, cos.shape, 1)
    low_half = (lane % (2 * AXIS_FREQS)) < AXIS_FREQS

    def norm_rope(x, g, scale):
        x = x.astype(F32)
        xn = x * lax.rsqrt(jnp.mean(x * x, axis=-1, keepdims=True) + NORM_EPS) * g
        partner = jnp.where(low_half, pltpu.roll(xn, ATTN_HEAD_DIM - AXIS_FREQS, 1),
                            pltpu.roll(xn, AXIS_FREQS, 1))
        return ((xn * cos + partner * sin) * scale).astype(BF16)

    for h in range(ATTN_HEADS):
        sl = slice(h * ATTN_HEAD_DIM, (h + 1) * ATTN_HEAD_DIM)
        qo_ref[:, sl] = norm_rope(q_ref[:, sl], qg_ref[...], ATTN_SCALE)
    for h in range(ATTN_KV_HEADS):
        sl = slice(h * ATTN_HEAD_DIM, (h + 1) * ATTN_HEAD_DIM)
        ko_ref[:, sl] = norm_rope(k_ref[:, sl], kg_ref[...], 1.0)


def _attn_prep(z, cos_t, sin_t, p, geo):
    rows = z.shape[0]
    t = geo.tp
    qcb = (geo.col0 + OFF_Q) // D_ATTN
    kcb = (geo.col0 + OFF_KA) // D_ATTN_KV
    lat_tiles, lat_per = geo.r_lat // t, geo.s // t

    def tab(i):
        return (jnp.where(i < lat_tiles, i % lat_per, lat_per), 0)

    vec = pl.BlockSpec((1, ATTN_HEAD_DIM), lambda i: (0, 0))
    return pl.pallas_call(
        _attn_prep_kernel,
        out_shape=(jax.ShapeDtypeStruct((rows, D_ATTN), BF16),
                   jax.ShapeDtypeStruct((rows, D_ATTN_KV), BF16)),
        grid=(rows // t,),
        in_specs=[pl.BlockSpec((t, D_ATTN), lambda i: (i, qcb)),
                  pl.BlockSpec((t, D_ATTN_KV), lambda i: (i, kcb)),
                  pl.BlockSpec((t, ATTN_HEAD_DIM), tab),
                  pl.BlockSpec((t, ATTN_HEAD_DIM), tab),
                  vec, vec],
        out_specs=(pl.BlockSpec((t, D_ATTN), lambda i: (i, 0)),
                   pl.BlockSpec((t, D_ATTN_KV), lambda i: (i, 0))),
        compiler_params=_cparams("parallel"),
        name="attn_prep",
    )(z, z, cos_t, sin_t, p['q_norm_g'], p['k_norm_g'])


def _attn_kernel(*refs, n_seg):
    q_ref = refs[0]
    k_refs = refs[1:1 + n_seg]
    v_refs = refs[1 + n_seg:1 + 2 * n_seg]
    o_ref = refs[1 + 2 * n_seg]
    for g in range(GQA_GROUP):
        sl = slice(g * ATTN_HEAD_DIM, (g + 1) * ATTN_HEAD_DIM)
        q = q_ref[:, sl]
        scores = [_dot_nt(q, k_ref[...]) for k_ref in k_refs]
        m = scores[0].max(axis=-1, keepdims=True)
        for s in scores[1:]:
            m = jnp.maximum(m, s.max(axis=-1, keepdims=True))
        denom = None
        acc = None
        for s, v_ref in zip(scores, v_refs):
            e = jnp.exp(s - m)
            es = e.sum(axis=-1, keepdims=True)
            pv = _dot(e.astype(BF16), v_ref[...])
            denom = es if denom is None else denom + es
            acc = pv if acc is None else acc + pv
        o_ref[:, sl] = (acc / denom).astype(BF16)


def _attention(qn, kn, z, geo, latent):
    gw = GQA_GROUP * ATTN_HEAD_DIM
    vcb = (geo.col0 + OFF_VA) // ATTN_HEAD_DIM
    ctx_blk0 = geo.r_lat // geo.lc
    k_ctx = pl.BlockSpec((geo.lc, ATTN_HEAD_DIM), lambda b, h, i: (ctx_blk0 + b, h))
    v_ctx = pl.BlockSpec((geo.lc, ATTN_HEAD_DIM), lambda b, h, i: (ctx_blk0 + b, vcb + h))
    if latent:
        tq = geo.tq
        per = geo.s // tq
        q_spec = pl.BlockSpec((tq, gw), lambda b, h, i: (b * per + i, h))
        k_lat = pl.BlockSpec((geo.s, ATTN_HEAD_DIM), lambda b, h, i: (b, h))
        v_lat = pl.BlockSpec((geo.s, ATTN_HEAD_DIM), lambda b, h, i: (b, vcb + h))
        in_specs = [q_spec, k_ctx, k_lat, v_ctx, v_lat]
        args = (qn, kn, kn, z, z)
        n_seg, out_rows = 2, geo.r_lat
        o_spec = q_spec
    else:
        tq, per = geo.lc, 1
        q_spec = pl.BlockSpec((tq, gw), lambda b, h, i: (ctx_blk0 + b, h))
        in_specs = [q_spec, k_ctx, v_ctx]
        args = (qn, kn, z)
        n_seg, out_rows = 1, geo.r_ctx
        o_spec = pl.BlockSpec((tq, gw), lambda b, h, i: (b, h))
    return pl.pallas_call(
        functools.partial(_attn_kernel, n_seg=n_seg),
        out_shape=jax.ShapeDtypeStruct((out_rows, D_ATTN), BF16),
        grid=(geo.b, ATTN_KV_HEADS, per),
        in_specs=in_specs,
        out_specs=o_spec,
        compiler_params=_cparams("parallel", "parallel", "parallel"),
        name="attention_lat" if latent else "attention_ctx",
    )(*args)


def _mm_kernel(x_ref, w_ref, o_ref):
    o_ref[...] = _dot(x_ref[...], w_ref[...]).astype(o_ref.dtype)


def _fourier_channels(z, wcs, geo):
    rows = z.shape[0]
    t = geo.tp
    fcb = (geo.col0 + OFF_FOUR) // D_FOURIER
    return pl.pallas_call(
        _mm_kernel,
        out_shape=jax.ShapeDtypeStruct((rows, 2 * D_FOURIER), BF16),
        grid=(rows // t,),
        in_specs=[pl.BlockSpec((t, D_FOURIER), lambda i: (i, fcb)),
                  pl.BlockSpec((D_FOURIER, 2 * D_FOURIER), lambda i: (0, 0))],
        out_specs=pl.BlockSpec((t, 2 * D_FOURIER), lambda i: (i, 0)),
        compiler_params=_cparams("parallel"),
        name="fourier_channels",
    )(z, wcs)


def _dft_kernel(cl_ref, sl_ref, xc_ref, xs_ref, o_ref):
    o_ref[...] = (_dot(cl_ref[...], xc_ref[...]) - _dot(sl_ref[...], xs_ref[...])).astype(BF16)


def _fourier_positions(xcs, cl, sl, n, blk0, nb, tmf):
    per = n // tmf
    return pl.pallas_call(
        _dft_kernel,
        out_shape=jax.ShapeDtypeStruct((nb * n, D_FOURIER), BF16),
        grid=(per, nb),
        in_specs=[pl.BlockSpec((tmf, n), lambda i, b: (i, 0)),
                  pl.BlockSpec((tmf, n), lambda i, b: (i, 0)),
                  pl.BlockSpec((n, D_FOURIER), lambda i, b: (blk0 + b, 0)),
                  pl.BlockSpec((n, D_FOURIER), lambda i, b: (blk0 + b, 1))],
        out_specs=pl.BlockSpec((tmf, D_FOURIER), lambda i, b: (b * per + i, 0)),
        compiler_params=_cparams("parallel", "parallel"),
        name="fourier_positions",
    )(cl, sl, xcs, xcs)


def _merge_kernel(rw_ref, fo_ref, at_ref, ga_ref, gf_ref, gc_ref, wr_ref, wf_ref, wa_ref, o_ref):
    y = _sigmoid(ga_ref[...].astype(F32)) * _dot(rw_ref[...], wr_ref[...])
    y += _sigmoid(gf_ref[...].astype(F32)) * _dot(fo_ref[...], wf_ref[...])
    y += _sigmoid(gc_ref[...].astype(F32)) * _dot(at_ref[...], wa_ref[...])
    o_ref[...] = y.astype(BF16)


def _merge(rw, fo, at, z, p, geo, n_rows):
    d = geo.d
    tm = geo.tm
    tn = min(1024, d)
    nj = d // tn
    return pl.pallas_call(
        _merge_kernel,
        out_shape=jax.ShapeDtypeStruct((n_rows, d), BF16),
        grid=(nj, n_rows // tm),
        in_specs=[pl.BlockSpec((tm, D_RWKV), lambda j, i: (i, 0)),
                  pl.BlockSpec((tm, D_FOURIER), lambda j, i: (i, 0)),
                  pl.BlockSpec((tm, D_ATTN), lambda j, i: (i, 0)),
                  pl.BlockSpec((tm, tn), lambda j, i: (i, j)),
                  pl.BlockSpec((tm, tn), lambda j, i: (i, nj + j)),
                  pl.BlockSpec((tm, tn), lambda j, i: (i, 2 * nj + j)),
                  pl.BlockSpec((D_RWKV, tn), lambda j, i: (0, j)),
                  pl.BlockSpec((D_FOURIER, tn), lambda j, i: (0, j)),
                  pl.BlockSpec((D_ATTN, tn), lambda j, i: (0, j))],
        out_specs=pl.BlockSpec((tm, tn), lambda j, i: (i, j)),
        compiler_params=_cparams("parallel", "parallel"),
        name="merge",
    )(rw, fo, at, z, z, z, p['w_br_rwkv'], p['w_br_fourier'], p['w_br_attn'])


def _outres_kernel(y_ref, w_ref, x_ref, gt_ref, o_ref):
    o_ref[...] = x_ref[...] + gt_ref[...] * _dot(y_ref[...], w_ref[...])


def _out_proj_residual(y, w, xs, mod4, geo, n_rows):
    d = geo.d
    tm = geo.tm
    tn = min(1024, d)
    bidx = geo.batch_of_tile(tm)
    return pl.pallas_call(
        _outres_kernel,
        out_shape=jax.ShapeDtypeStruct((n_rows, d), F32),
        grid=(d // tn, n_rows // tm),
        in_specs=[pl.BlockSpec((tm, d), lambda j, i: (i, 0)),
                  pl.BlockSpec((d, tn), lambda j, i: (0, j)),
                  pl.BlockSpec((tm, tn), lambda j, i: (i, j)),
                  pl.BlockSpec((None, None, 1, tn), lambda j, i: (bidx(i), 2, 0, j))],
        out_specs=pl.BlockSpec((tm, tn), lambda j, i: (i, j)),
        compiler_params=_cparams("parallel", "parallel"),
        name="out_proj_residual",
    )(y, w, xs, mod4)


def _expert_ffn(h, wgu, bgu, wd, bd, f):
    gu = _dot(h, wgu) + bgu
    gate = jnp.minimum(gu[:, :f], SWIGLU_LIMIT)
    up = jnp.clip(gu[:, f:], -SWIGLU_LIMIT, SWIGLU_LIMIT)
    act = (up + 1.0) * gate * _sigmoid(SWIGLU_ALPHA * gate)
    return _dot(act.astype(BF16), wd) + bd


def _route(sel, n, t, n_exp):
    p = TOP_K * n
    n_tiles = p // t + n_exp
    e_flat = sel[:, :TOP_K].astype(jnp.int32).reshape(p)
    w_flat = sel[:, TOP_K:2 * TOP_K].reshape(p)
    order = jnp.argsort(e_flat, stable=True).astype(jnp.int32)
    experts = jnp.arange(n_exp, dtype=jnp.int32)
    counts = jnp.sum((e_flat[:, None] == experts[None, :]).astype(jnp.int32), axis=0)
    cstart = jnp.cumsum(counts) - counts
    ptiles = (counts + t - 1) // t
    tend = jnp.cumsum(ptiles)
    tstart = tend - ptiles
    n_used = tend[-1]
    tile_ids = jnp.arange(n_tiles, dtype=jnp.int32)
    last_e = jnp.max(jnp.where(counts > 0, experts, 0))
    te = jnp.searchsorted(tend, tile_ids, side='right').astype(jnp.int32)
    te = jnp.where(tile_ids < n_used, jnp.minimum(te, n_exp - 1), last_e)
    slot = jnp.arange(n_tiles * t, dtype=jnp.int32)
    e_s = te[slot // t]
    r = slot - tstart[e_s] * t
    valid = (slot // t < n_used) & (r < counts[e_s])
    pair = order[jnp.clip(cstart[e_s] + r, 0, p - 1)]
    token, k = pair // TOP_K, pair % TOP_K
    tok = jnp.where(valid, token, 0)
    dest = jnp.where(valid, k * n + token, p + e_s * t + jnp.clip(r - counts[e_s], 0, t - 1))
    w = jnp.where(valid, w_flat[pair], 0.0)
    return (te, n_used.reshape(1).astype(jnp.int32), tok.reshape(n_tiles, 1, t),
            dest.reshape(n_tiles, 1, t), w.reshape(n_tiles * t, 1))


def _moe_routed_kernel(te_ref, nu_ref, tok_ref, nxt_ref, dst_ref, w_ref, h_hbm, wgu_ref, bgu_ref,
                       wd_ref, bd_ref, o_hbm, gbuf, obuf, gsem, ssem, *, f, t, dump0, n_dump):
    i = pl.program_id(0)
    n_used = nu_ref[0]
    slot = i % 2

    def gather(idx_ref, sl):
        def body(r, carry):
            pltpu.make_async_copy(h_hbm.at[pl.ds(idx_ref[0, 0, r], 1)], gbuf.at[sl, pl.ds(r, 1)],
                                  gsem.at[sl]).start()
            return carry
        lax.fori_loop(0, t, body, 0, unroll=8)

    def wait_gather(sl):
        pltpu.make_async_copy(h_hbm.at[pl.ds(0, t)], gbuf.at[sl], gsem.at[sl]).wait()

    def wait_scatter(sl):
        pltpu.make_async_copy(obuf.at[sl], o_hbm.at[pl.ds(0, t)], ssem.at[sl]).wait()

    @pl.when(i == 0)
    def _():
        gather(tok_ref, 0)
        obuf[1] = jnp.zeros((t, obuf.shape[2]), F32)
        fills = [pltpu.make_async_copy(obuf.at[1], o_hbm.at[pl.ds(dump0 + e * t, t)], ssem.at[1])
                 for e in range(n_dump)]
        for c in fills:
            c.start()
        for c in fills:
            c.wait()

    @pl.when(i < n_used)
    def _():
        gather(nxt_ref, 1 - slot)
        wait_gather(slot)
        y = _expert_ffn(gbuf[slot].astype(BF16), wgu_ref[...], bgu_ref[...], wd_ref[...], bd_ref[...], f)
        y = y * w_ref[...]

        @pl.when(i >= 2)
        def _():
            wait_scatter(slot)

        obuf[slot] = y

        def body(r, carry):
            pltpu.make_async_copy(obuf.at[slot, pl.ds(r, 1)], o_hbm.at[pl.ds(dst_ref[0, 0, r], 1)],
                                  ssem.at[slot]).start()
            return carry
        lax.fori_loop(0, t, body, 0, unroll=8)

    @pl.when(i == pl.num_programs(0) - 1)
    def _():
        wait_gather(n_used % 2)
        wait_scatter((n_used - 1) % 2)

        @pl.when(n_used >= 2)
        def _():
            wait_scatter(n_used % 2)


def _moe_routed(h2, sel, wgu, bgu, wd, bd, geo, n_rows):
    d = geo.d
    n_exp, _, f2 = wgu.shape
    f = f2 // 2
    t = geo.t_moe
    te, n_used, tok, dest, w = _route(sel, n_rows, t, n_exp)
    n_tiles = tok.shape[0]
    smem = functools.partial(pl.BlockSpec, (1, 1, t), memory_space=pltpu.SMEM)
    grid_spec = pltpu.PrefetchScalarGridSpec(
        num_scalar_prefetch=2,
        grid=(n_tiles,),
        in_specs=[smem(lambda i, te, nu: (i, 0, 0)),
                  smem(lambda i, te, nu: (jnp.minimum(i + 1, n_tiles - 1), 0, 0)),
                  smem(lambda i, te, nu: (i, 0, 0)),
                  pl.BlockSpec((t, 1), lambda i, te, nu: (i, 0)),
                  pl.BlockSpec(memory_space=pl.ANY),
                  pl.BlockSpec((None, d, f2), lambda i, te, nu: (te[i], 0, 0)),
                  pl.BlockSpec((None, 1, f2), lambda i, te, nu: (te[i], 0, 0)),
                  pl.BlockSpec((None, f, d), lambda i, te, nu: (te[i], 0, 0)),
                  pl.BlockSpec((None, 1, d), lambda i, te, nu: (te[i], 0, 0))],
        out_specs=pl.BlockSpec(memory_space=pl.ANY),
        scratch_shapes=[pltpu.VMEM((2, t, d), F32), pltpu.VMEM((2, t, d), F32),
                        pltpu.SemaphoreType.DMA((2,)), pltpu.SemaphoreType.DMA((2,))])
    return pl.pallas_call(
        functools.partial(_moe_routed_kernel, f=f, t=t, dump0=TOP_K * n_rows, n_dump=n_exp),
        out_shape=jax.ShapeDtypeStruct((TOP_K * n_rows + n_exp * t, d), F32),
        grid_spec=grid_spec,
        compiler_params=pltpu.CompilerParams(dimension_semantics=("arbitrary",),
                                             vmem_limit_bytes=VMEM_LIMIT, disable_bounds_checks=True),
        name="moe_routed",
    )(te, n_used, tok, tok, dest, w, h2, wgu, bgu, wd, bd)


def _combine_kernel(x_ref, gt_ref, y0_ref, y1_ref, y2_ref, y3_ref, o_ref):
    acc = (y0_ref[...] + y1_ref[...]) + (y2_ref[...] + y3_ref[...])
    o_ref[...] = x_ref[...] + gt_ref[...] * acc


def _moe_combine(y4, xs, mod4, geo, n_rows):
    d = geo.d
    tm = geo.t_moe
    per = n_rows // tm
    bidx = geo.batch_of_tile(tm)

    def part(k):
        return pl.BlockSpec((tm, d), lambda i: (k * per + i, 0))

    return pl.pallas_call(
        _combine_kernel,
        out_shape=jax.ShapeDtypeStruct((n_rows, d), F32),
        grid=(per,),
        in_specs=[pl.BlockSpec((tm, d), lambda i: (i, 0)),
                  pl.BlockSpec((None, None, 1, d), lambda i: (bidx(i), 5, 0, 0)),
                  part(0), part(1), part(2), part(3)],
        out_specs=pl.BlockSpec((tm, d), lambda i: (i, 0)),
        compiler_params=_cparams("parallel"),
        name="moe_combine",
    )(xs, mod4, y4, y4, y4, y4)


class _Geometry:
    def __init__(self, b, s, lc, d):
        self.b, self.s, self.lc, self.d = b, s, lc, d
        self.r_lat, self.r_ctx = b * s, b * lc
        self.rows = self.r_lat + self.r_ctx
        self.col0 = 3 * d
        assert self.col0 % 1024 == 0, "gate columns must end on a 1024-column boundary"
        self.tm = min(1024, s, self.r_ctx)
        self.tp = min(256, s, lc)
        self.tq = min(512, s)
        self.chunk = min(WKV_CHUNK, lc, s)
        self.t_moe = min(256, self.tm)
        for t in (self.tm,):
            assert s % t == 0 and self.r_ctx % t == 0
        assert s % self.tp == 0 and lc % self.tp == 0 and self.tp % BF16_SUBLANES == 0
        assert s % self.chunk == 0 and lc % self.chunk == 0 and s % GRID_W == 0
        self.hsum = jnp.asarray(np.kron(np.eye(RWKV_HEADS), np.ones((RWKV_HEAD_DIM,) * 2)), BF16)

    def batch_of_tile(self, tm):
        per, nb = self.s // tm, self.b
        return lambda i: jnp.minimum(i // per, nb)


def _dft_tables(n):
    j = np.arange(n, dtype=np.int64)
    ang = 2.0 * np.pi * ((j[:, None] * j[None, :]) % n).astype(np.float64) / n
    return np.cos(ang) / np.sqrt(n), np.sin(ang) / np.sqrt(n)


def _rope_tables(s, t):
    pos = np.arange(s)
    inv_freq = ROPE_THETA ** (-np.arange(AXIS_FREQS, dtype=np.float32) / AXIS_FREQS)
    ang_r = (pos // GRID_W).astype(np.float32)[:, None] * inv_freq.astype(np.float32)
    ang_c = (pos % GRID_W).astype(np.float32)[:, None] * inv_freq.astype(np.float32)
    cr, sr, cc, sc = np.cos(ang_r), np.sin(ang_r), np.cos(ang_c), np.sin(ang_c)
    cos = np.concatenate([cr, cr, cc, cc], axis=1)
    sin = np.concatenate([-sr, sr, -sc, sc], axis=1)
    cos = np.concatenate([cos, np.ones((t, ATTN_HEAD_DIM))], axis=0)
    sin = np.concatenate([sin, np.zeros((t, ATTN_HEAD_DIM))], axis=0)
    return jnp.asarray(cos, F32), jnp.asarray(sin, F32)


def _permute_w_in(w_in, d):
    o = np.cumsum([0, D_RWKV, D_RWKV, LORA, LORA, LORA, LORA, D_ATTN_KV, D_ATTN_KV, D_RWKV, GATE_LORA,
                   D_ATTN, D_FOURIER, 3 * d])
    seg = lambda i: w_in[..., o[i]:o[i + 1]]
    pad = lambda a: jnp.pad(a, ((0, 0), (0, 0), (0, LORA_PAD - LORA)))
    parts = [seg(12), seg(0), seg(1), pad(seg(2)), pad(seg(3)), pad(seg(4)), pad(seg(5)), seg(8), seg(10),
             seg(11), seg(6), seg(7), seg(9)]
    w = jnp.concatenate(parts, axis=-1)
    n = w.shape[-1]
    n_pad = -(-n // W_IN_TN) * W_IN_TN
    return jnp.pad(w, ((0, 0), (0, 0), (0, n_pad - n))).astype(BF16)


def _split_hi_lo(w):
    hi = w.astype(BF16)
    return hi, (w - hi.astype(F32)).astype(BF16)


def kernel(x, c, ctx, c_ctx, ada_w, ada_b, norm1_g, norm2_g, w_in, rwkv_conv, w0, w_lora, a0, a_lora, g_lora, k_k, k_a, r_k, lnx_g, lnx_b, q_norm_g, k_norm_g, w_br_rwkv, w_br_fourier, w_br_attn, w_out, router_w, router_b, exp_w_gu, exp_b_gu, exp_w_down, exp_b_down):
    b, s, d = x.shape
    lc = ctx.shape[1]
    depth = w_in.shape[0]
    n_exp = router_w.shape[-1]
    geo = _Geometry(b, s, lc, d)

    w_in_p = _permute_w_in(w_in, d)
    lora_pad = ((0, 0), (0, 0), (0, LORA_PAD - LORA), (0, 0))
    w_lora_p = jnp.pad(w_lora, lora_pad).astype(BF16)
    a_lora_p = jnp.pad(a_lora, lora_pad).astype(BF16)
    g_lora_b = g_lora.astype(BF16)
    wbr_r, wbr_f, wbr_a, w_out_b = (w.astype(BF16) for w in (w_br_rwkv, w_br_fourier, w_br_attn, w_out))
    rw_p = jnp.pad(router_w, ((0, 0), (0, 0), (0, ROUTER_PAD - n_exp)))
    rw_hi, rw_lo = _split_hi_lo(rw_p)
    rb_p = jnp.pad(router_b, ((0, 0), (0, ROUTER_PAD - n_exp)), constant_values=NEG_BIG)
    wgu_b, wd_b = exp_w_gu.astype(BF16), exp_w_down.astype(BF16)
    cos_t, sin_t = _rope_tables(s, geo.tp)
    cc, sc = _dft_tables(FOURIER_GROUP_DIM)
    groups = D_FOURIER // FOURIER_GROUP_DIM
    wcs = jnp.asarray(np.concatenate([np.kron(np.eye(groups), cc), np.kron(np.eye(groups), sc)], axis=1), BF16)
    cl_lat, sl_lat = (jnp.asarray(m, BF16) for m in _dft_tables(s))
    cl_ctx, sl_ctx = (jnp.asarray(m, BF16) for m in _dft_tables(lc))

    mod_rows = -(-(b + 1) // BF16_SUBLANES) * BF16_SUBLANES
    c_all = jnp.concatenate([c, c_ctx[None], jnp.zeros((mod_rows - b - 1, d), F32)], axis=0)
    mod = _adaln(c_all, ada_w, ada_b).reshape(depth, mod_rows, 6, 1, d)

    xs = jnp.concatenate([x.reshape(b * s, d), ctx.reshape(b * lc, d)], axis=0)
    for l in range(depth):
        last = l == depth - 1
        mod4 = mod[l]
        p = dict(conv=rwkv_conv[l], w0=w0[l][:, None, :], w_lora=w_lora_p[l], a0=a0[l][:, None, :],
                 a_lora=a_lora_p[l], k_k=k_k[l][None], k_a=k_a[l][None], r_k=r_k[l][None],
                 lnx_g=lnx_g[l][None], lnx_b=lnx_b[l][None], g_lora=g_lora_b[l],
                 q_norm_g=q_norm_g[l][None], k_norm_g=k_norm_g[l][None],
                 w_br_rwkv=wbr_r[l], w_br_fourier=wbr_f[l], w_br_attn=wbr_a[l])
        n_rows = geo.r_lat if last else geo.rows

        z = _norm_in_proj(xs, norm1_g[l], mod4, w_in_p[l], geo)

        v_c, r_c, kkn, lw, kd, bb = _rwkv_prepare(z, p, geo)
        yf, yb = _wkv_scan(v_c, r_c, kkn, lw, kd, bb, geo)
        rw = _rwkv_output(yf, yb, r_c, v_c, kd, z, p, geo)

        qn, kn = _attn_prep(z, cos_t, sin_t, p, geo)
        att = _attention(qn, kn, z, geo, latent=True)

        xcs = _fourier_channels(z, wcs, geo)
        fo = _fourier_positions(xcs, cl_lat, sl_lat, s, 0, b, min(1024, s))
        if not last:
            att = jnp.concatenate([att, _attention(qn, kn, z, geo, latent=False)], axis=0)
            fo_c = _fourier_positions(xcs, cl_ctx, sl_ctx, lc, geo.r_lat // lc, b, lc)
            fo = jnp.concatenate([fo, fo_c], axis=0)

        ym = _merge(rw, fo, att, z, p, geo, n_rows)
        xs = _out_proj_residual(ym, w_out_b[l], xs, mod4, geo, n_rows)

        h2, sel = _norm_router(xs, norm2_g[l], mod4, rw_hi[l], rw_lo[l], rb_p[l][None], geo, n_rows)
        y4 = _moe_routed(h2, sel, wgu_b[l], exp_b_gu[l][:, None, :], wd_b[l], exp_b_down[l][:, None, :],
                         geo, n_rows)
        xs = _moe_combine(y4, xs, mod4, geo, n_rows)
    return xs[:geo.r_lat].reshape(b, s, d)
```

```python
import functools

import numpy as np
import jax
import jax.numpy as jnp
from jax import lax
from jax.experimental import pallas as pl
from jax.experimental.pallas import tpu as pltpu

F32 = jnp.float32
BF16 = jnp.bfloat16

NORM_EPS = 1e-6
GN_EPS = 64e-5
RWKV_HEADS = 8
RWKV_HEAD_DIM = 64
D_RWKV = RWKV_HEADS * RWKV_HEAD_DIM
LORA = 96
GATE_LORA = 256
D_FOURIER = 512
FOURIER_GROUP_DIM = 128
ATTN_HEADS = 8
ATTN_KV_HEADS = 2
ATTN_HEAD_DIM = 128
GQA_GROUP = ATTN_HEADS // ATTN_KV_HEADS
D_ATTN = ATTN_HEADS * ATTN_HEAD_DIM
D_ATTN_KV = ATTN_KV_HEADS * ATTN_HEAD_DIM
ATTN_SCALE = ATTN_HEAD_DIM ** -0.5
GRID_W = 64
ROPE_THETA = 10000.0
AXIS_FREQS = ATTN_HEAD_DIM // 4
TOP_K = 4
SWIGLU_LIMIT = 7.0
SWIGLU_ALPHA = 1.702

LANES = 128
BF16_SUBLANES = 16
VMEM_LIMIT = 56 * 1024 * 1024

LORA_PAD = LANES
W_IN_TN = 1536
WKV_CHUNK = 64
PAIR = 2 * RWKV_HEAD_DIM
ROUTER_PAD = LANES
NEG_BIG = -1e30

OFF_K, OFF_V, OFF_LORA, OFF_R, OFF_Q, OFF_FOUR, OFF_KA, OFF_VA, OFF_GD, OFF_END = (
    0, 512, 1024, 1536, 2048, 3072, 3584, 3840, 4096, 4352)


def _cparams(*sem):
    return pltpu.CompilerParams(dimension_semantics=sem, vmem_limit_bytes=VMEM_LIMIT)


def _dot(a, b):
    return jnp.dot(a, b, preferred_element_type=F32)


def _dot_nt(a, b):
    return lax.dot_general(a, b, (((1,), (1,)), ((), ())), preferred_element_type=F32)


def _dot_tn(a, b):
    return lax.dot_general(a, b, (((0,), (0,)), ((), ())), preferred_element_type=F32)


def _split_dot(x, g):
    hi = x.astype(BF16)
    lo = (x - hi.astype(F32)).astype(BF16)
    return _dot(hi, g) + _dot(lo, g)


def _sigmoid(x):
    return 1.0 / (1.0 + jnp.exp(-x))


def _adaln_kernel(c_ref, w_ref, b_ref, o_ref):
    c = c_ref[...]
    s = (c * _sigmoid(c)).astype(BF16)
    o_ref[...] = _dot(s, w_ref[...].astype(BF16)) + b_ref[...]


def _adaln(c_all, ada_w, ada_b):
    depth, d, n = ada_w.shape
    rows = c_all.shape[0]
    tn = 1024 if n % 1024 == 0 else n
    return pl.pallas_call(
        _adaln_kernel,
        out_shape=jax.ShapeDtypeStruct((depth, rows, n), F32),
        grid=(depth, n // tn),
        in_specs=[
            pl.BlockSpec((rows, d), lambda l, j: (0, 0)),
            pl.BlockSpec((None, d, tn), lambda l, j: (l, 0, j)),
            pl.BlockSpec((None, 1, tn), lambda l, j: (l, 0, j)),
        ],
        out_specs=pl.BlockSpec((None, rows, tn), lambda l, j: (l, 0, j)),
        compiler_params=_cparams("parallel", "parallel"),
        name="adaln",
    )(c_all, ada_w, ada_b.reshape(depth, 1, n))


def _modulated_norm(x, g, sc, sh):
    ms = jnp.mean(x * x, axis=-1, keepdims=True)
    return x * lax.rsqrt(ms + NORM_EPS) * g * (1.0 + sc) + sh


def _normmm_kernel(x_ref, g_ref, sc_ref, sh_ref, w_ref, o_ref, h_ref):
    @pl.when(pl.program_id(1) == 0)
    def _():
        h_ref[...] = _modulated_norm(x_ref[...], g_ref[...], sc_ref[...], sh_ref[...]).astype(BF16)

    o_ref[...] = _dot(h_ref[...], w_ref[...]).astype(o_ref.dtype)


def _norm_in_proj(xs, gain, mod4, w, geo):
    rows, d = xs.shape
    n = w.shape[1]
    tm, tn = geo.tm, W_IN_TN
    bidx = geo.batch_of_tile(tm)
    return pl.pallas_call(
        _normmm_kernel,
        out_shape=jax.ShapeDtypeStruct((rows, n), BF16),
        grid=(rows // tm, n // tn),
        in_specs=[
            pl.BlockSpec((tm, d), lambda i, j: (i, 0)),
            pl.BlockSpec((1, d), lambda i, j: (0, 0)),
            pl.BlockSpec((None, None, 1, d), lambda i, j: (bidx(i), 1, 0, 0)),
            pl.BlockSpec((None, None, 1, d), lambda i, j: (bidx(i), 0, 0, 0)),
            pl.BlockSpec((d, tn), lambda i, j: (0, j)),
        ],
        out_specs=pl.BlockSpec((tm, tn), lambda i, j: (i, j)),
        scratch_shapes=[pltpu.VMEM((tm, d), BF16)],
        compiler_params=_cparams("parallel", "arbitrary"),
        name="norm_in_proj",
    )(xs, gain.reshape(1, d), mod4, mod4, w)


def _norm_router_kernel(x_ref, g_ref, sc_ref, sh_ref, wh_ref, wl_ref, rb_ref, h_ref, sel_ref):
    h = _modulated_norm(x_ref[...], g_ref[...], sc_ref[...], sh_ref[...])
    h_ref[...] = h
    hi = h.astype(BF16)
    lo = (h - hi.astype(F32)).astype(BF16)
    logits = _dot(hi, wh_ref[...]) + _dot(lo, wh_ref[...]) + _dot(hi, wl_ref[...]) + rb_ref[...]
    lane = lax.broadcasted_iota(jnp.int32, logits.shape, 1).astype(F32)
    work = logits
    vals, firsts = [], []
    for _ in range(TOP_K):
        m = jnp.max(work, axis=-1, keepdims=True)
        first = jnp.min(jnp.where(work == m, lane, float(ROUTER_PAD)), axis=-1, keepdims=True)
        vals.append(m)
        firsts.append(first)
        work = jnp.where(lane == first, 2.0 * NEG_BIG, work)
    exps = [jnp.exp(v - vals[0]) for v in vals]
    denom = exps[0] + exps[1] + exps[2] + exps[3]
    table = jnp.zeros_like(logits)
    for k in range(TOP_K):
        table = jnp.where(lane == float(k), firsts[k], table)
        table = jnp.where(lane == float(TOP_K + k), exps[k] / denom, table)
    sel_ref[...] = table


def _norm_router(xs, gain, mod4, rw_hi, rw_lo, rb, geo, n_rows):
    d = xs.shape[1]
    tm = geo.tm
    bidx = geo.batch_of_tile(tm)
    return pl.pallas_call(
        _norm_router_kernel,
        out_shape=(jax.ShapeDtypeStruct((n_rows, d), F32),
                   jax.ShapeDtypeStruct((n_rows, ROUTER_PAD), F32)),
        grid=(n_rows // tm,),
        in_specs=[
            pl.BlockSpec((tm, d), lambda i: (i, 0)),
            pl.BlockSpec((1, d), lambda i: (0, 0)),
            pl.BlockSpec((None, None, 1, d), lambda i: (bidx(i), 4, 0, 0)),
            pl.BlockSpec((None, None, 1, d), lambda i: (bidx(i), 3, 0, 0)),
            pl.BlockSpec((d, ROUTER_PAD), lambda i: (0, 0)),
            pl.BlockSpec((d, ROUTER_PAD), lambda i: (0, 0)),
            pl.BlockSpec((1, ROUTER_PAD), lambda i: (0, 0)),
        ],
        out_specs=(pl.BlockSpec((tm, d), lambda i: (i, 0)),
                   pl.BlockSpec((tm, ROUTER_PAD), lambda i: (i, 0))),
        compiler_params=_cparams("parallel"),
        name="norm_router",
    )(xs, gain.reshape(1, d), mod4, mod4, rw_hi, rw_lo, rb)


def _prep_kernel(zk_ref, zv_ref, zl_ref, zr_ref, pk_ref, pv_ref, pr_ref, nk_ref, nv_ref, nr_ref,
                 conv_ref, w0_ref, wl_ref, a0_ref, al_ref, kk_ref, ka_ref, hsum_ref,
                 v_out, r_out, kkn_out, lw_out, kd_out, bb_out, *, t, lat_tiles, lat_per, ctx_per):
    i = pl.program_id(0)
    is_lat = i < lat_tiles
    per = jnp.where(is_lat, lat_per, ctx_per)
    j = jnp.where(is_lat, i, i - lat_tiles) % per
    first = j == 0
    last = j == per - 1
    row = lax.broadcasted_iota(jnp.int32, (t, 1), 0)

    def conv(z_ref, p_ref, n_ref, which):
        z = z_ref[...].astype(F32)
        prev_row = p_ref[...].astype(F32)[BF16_SUBLANES - 1:BF16_SUBLANES, :]
        next_row = n_ref[...].astype(F32)[0:1, :]
        prev_row = jnp.where(first, 0.0, prev_row)
        next_row = jnp.where(last, 0.0, next_row)
        zm = jnp.where(row == 0, prev_row, pltpu.roll(z, 1, 0))
        zp = jnp.where(row == t - 1, next_row, pltpu.roll(z, t - 1, 0))
        w = conv_ref[which]
        return zm * w[0:1] + z * w[1:2] + zp * w[2:3]

    k = conv(zk_ref, pk_ref, nk_ref, 0)
    v = conv(zv_ref, pv_ref, nv_ref, 1)
    r = conv(zr_ref, pr_ref, nr_ref, 2)
    v_out[...] = v.astype(BF16)
    r_out[...] = r.astype(BF16)
    kkv = k * kk_ref[...]
    ss = _split_dot(kkv * kkv, hsum_ref[...])
    kkn = kkv * lax.rsqrt(jnp.maximum(ss, 1e-24))
    kkn_out[...] = kkn.astype(BF16)
    zl = zl_ref[...]
    for d in range(2):
        wd = zl[:, d * LORA_PAD:(d + 1) * LORA_PAD].astype(F32)
        w_raw = w0_ref[d] + _dot(jnp.tanh(wd).astype(BF16), wl_ref[d])
        lw_out[d] = -_sigmoid(w_raw) * float(np.exp(-0.5))
        ad = zl[:, (2 + d) * LORA_PAD:(3 + d) * LORA_PAD]
        a = _sigmoid(a0_ref[d] + _dot(ad, al_ref[d]))
        kd_out[d] = (k * (1.0 + (a - 1.0) * ka_ref[...])).astype(BF16)
        bb_out[d] = (kkn * a).astype(BF16)


def _rwkv_prepare(z, p, geo):
    rows = z.shape[0]
    t = geo.tp
    base = geo.col0
    hb = t // BF16_SUBLANES
    n_halo = rows // BF16_SUBLANES

    def zcol(off):
        cb = (base + off) // D_RWKV
        return pl.BlockSpec((t, D_RWKV), lambda i: (i, cb))

    def prev(off):
        cb = (base + off) // D_RWKV
        return pl.BlockSpec((BF16_SUBLANES, D_RWKV), lambda i: (jnp.maximum(i * hb - 1, 0), cb))

    def nxt(off):
        cb = (base + off) // D_RWKV
        return pl.BlockSpec((BF16_SUBLANES, D_RWKV),
                            lambda i: (jnp.minimum((i + 1) * hb, n_halo - 1), cb))

    def full(a):
        nd = a.ndim
        return pl.BlockSpec(a.shape, lambda i: (0,) * nd)

    params = (p['conv'], p['w0'], p['w_lora'], p['a0'], p['a_lora'], p['k_k'], p['k_a'], geo.hsum)
    row_spec = pl.BlockSpec((t, D_RWKV), lambda i: (i, 0))
    dir_spec = pl.BlockSpec((2, t, D_RWKV), lambda i: (0, i, 0))
    kern = functools.partial(_prep_kernel, t=t, lat_tiles=geo.r_lat // t, lat_per=geo.s // t,
                             ctx_per=geo.lc // t)
    return pl.pallas_call(
        kern,
        out_shape=(jax.ShapeDtypeStruct((rows, D_RWKV), BF16),
                   jax.ShapeDtypeStruct((rows, D_RWKV), BF16),
                   jax.ShapeDtypeStruct((rows, D_RWKV), BF16),
                   jax.ShapeDtypeStruct((2, rows, D_RWKV), F32),
                   jax.ShapeDtypeStruct((2, rows, D_RWKV), BF16),
                   jax.ShapeDtypeStruct((2, rows, D_RWKV), BF16)),
        grid=(rows // t,),
        in_specs=[zcol(OFF_K), zcol(OFF_V), zcol(OFF_LORA), zcol(OFF_R),
                  prev(OFF_K), prev(OFF_V), prev(OFF_R), nxt(OFF_K), nxt(OFF_V), nxt(OFF_R)]
                 + [full(a) for a in params],
        out_specs=(row_spec, row_spec, row_spec, dir_spec, dir_spec, dir_spec),
        compiler_params=_cparams("parallel"),
        name="rwkv_prepare",
    )(z, z, z, z, z, z, z, z, z, z, *params)


def _wkv_kernel(vf_ref, rf_ref, kf_ref, lwf_ref, kdf_ref, bbf_ref,
                vb_ref, rb_ref, kb_ref, lwb_ref, kdb_ref, bbb_ref, yf_ref, yb_ref, s_ref, *, c):
    @pl.when(pl.program_id(1) == 0)
    def _():
        s_ref[...] = jnp.zeros_like(s_ref)

    c2 = 2 * c
    n_pairs = RWKV_HEADS // 2
    ti = lax.broadcasted_iota(jnp.int32, (c, c), 0)
    ii = lax.broadcasted_iota(jnp.int32, (c, c), 1)
    rt = lax.broadcasted_iota(jnp.int32, (c2, c2), 0)
    ci = lax.broadcasted_iota(jnp.int32, (c2, c2), 1)
    same = (rt // c) == (ci // c)
    tri, strict, incl = [], [], []
    for sign in (1, -1):
        tri.append(jnp.where((ti - ii) * sign >= 0, 1.0, 0.0).astype(BF16))
        before = (rt % c - ci % c) * sign
        strict.append(same & (before > 0))
        incl.append(same & (before >= 0))
    head0 = lax.broadcasted_iota(jnp.int32, (c, PAIR), 1) < RWKV_HEAD_DIM
    eye = (lax.broadcasted_iota(jnp.int32, (PAIR, PAIR), 0)
           == lax.broadcasted_iota(jnp.int32, (PAIR, PAIR), 1)).astype(F32)

    def stack(x):
        return jnp.concatenate([jnp.where(head0, x, 0.0), jnp.where(head0, 0.0, x)], axis=0).astype(BF16)

    refs = ((vf_ref, rf_ref, kf_ref, lwf_ref, kdf_ref, bbf_ref, yf_ref),
            (vb_ref, rb_ref, kb_ref, lwb_ref, kdb_ref, bbb_ref, yb_ref))
    chains = [(d, p) for d in range(2) for p in range(n_pairs)]
    sls = [slice(p * PAIR, (p + 1) * PAIR) for _, p in chains]
    n = len(chains)

    lws = [refs[d][3][:, sl] for (d, _), sl in zip(chains, sls)]
    cums = []
    for (d, _), lw in zip(chains, lws):
        hi = lw.astype(BF16)
        lo = (lw - hi.astype(F32)).astype(BF16)
        cums.append(_dot(tri[d], hi) + _dot(tri[d], lo))
    g_end = [jnp.exp(jnp.sum(lw, axis=0, keepdims=True)) for lw in lws]
    a_s, b_s, k_s, r_s, v_s = [], [], [], [], []
    for (d, _), sl, lw, cum in zip(chains, sls, lws, cums):
        v_ref, r_ref, kk_ref, _, kd_ref, bb_ref, _ = refs[d]
        g_inv = jnp.exp(-cum)
        a_s.append(stack(-kk_ref[:, sl].astype(F32) * jnp.exp(cum - lw)))
        b_s.append(stack(bb_ref[:, sl].astype(F32) * g_inv))
        k_s.append(stack(kd_ref[:, sl].astype(F32) * g_inv))
        r_s.append(stack(r_ref[:, sl].astype(F32) * jnp.exp(cum)))
        v_s.append(stack(v_ref[:, sl].astype(F32)))
    gs = [_dot_nt(jnp.concatenate([a_s[i], r_s[i]], axis=0), jnp.concatenate([b_s[i], k_s[i]], axis=0))
          for i in range(n)]
    l_ab = [jnp.where(strict[d], g[:c2, :c2], 0.0).astype(BF16) for (d, _), g in zip(chains, gs)]
    l_ak = [jnp.where(strict[d], g[:c2, c2:], 0.0).astype(BF16) for (d, _), g in zip(chains, gs)]
    m_rb = [jnp.where(incl[d], g[c2:, :c2], 0.0).astype(BF16) for (d, _), g in zip(chains, gs)]
    m_rk = [jnp.where(incl[d], g[c2:, c2:], 0.0).astype(BF16) for (d, _), g in zip(chains, gs)]

    xs = [jnp.concatenate([a_s[i].astype(F32), _dot(l_ak[i], v_s[i])], axis=1) for i in range(n)]
    lps = l_ab
    n_sq = int(np.log2(c))
    for sq in range(n_sq):
        xs = [x + _dot(lp, x.astype(BF16)) for x, lp in zip(xs, lps)]
        if sq + 1 < n_sq:
            lps = [_dot(lp, lp).astype(BF16) for lp in lps]
    xbs = [x.astype(BF16) for x in xs]
    mxs = [_dot(m_rb[i], xbs[i]) for i in range(n)]
    gqs = [(r_s[i].astype(F32) + mxs[i][:, :PAIR]).astype(BF16) for i in range(n)]
    yqs = [mxs[i][:, PAIR:] + _dot(m_rk[i], v_s[i]) for i in range(n)]
    wbs = [_dot_tn(xbs[i], b_s[i]) for i in range(n)]
    vks = [_dot_tn(v_s[i], k_s[i]) for i in range(n)]
    p_mats = [((eye + wbs[i][:PAIR]) * g_end[i]).astype(BF16) for i in range(n)]
    q_mats = [(wbs[i][PAIR:] + vks[i]) * g_end[i] for i in range(n)]
    s0b = [s_ref[i].astype(BF16) for i in range(n)]
    for i, ((d, _), sl) in enumerate(zip(chains, sls)):
        ys = _dot_nt(gqs[i], s0b[i]) + yqs[i]
        refs[d][6][:, sl] = ys[:c] + ys[c:]
        s_ref[i] = _dot(s0b[i], p_mats[i]) + q_mats[i]


def _wkv_scan(v, r, kkn, lw, kd, bb, geo):
    rows = v.shape[0]
    c = geo.chunk
    n_ctx, n_lat = geo.lc // c, geo.s // c
    lat_blocks = geo.r_lat // c

    def rowblk(d):
        def f(b, s):
            ctx_j = s if d == 0 else n_ctx - 1 - s
            lat_j = s - n_ctx if d == 0 else n_lat - 1 - (s - n_ctx)
            return jnp.where(s < n_ctx, lat_blocks + b * n_ctx + ctx_j, b * n_lat + lat_j)
        return f

    def shared(d):
        f = rowblk(d)
        return pl.BlockSpec((c, D_RWKV), lambda b, s: (f(b, s), 0))

    def perdir(d):
        f = rowblk(d)
        return pl.BlockSpec((None, c, D_RWKV), lambda b, s: (d, f(b, s), 0))

    in_specs = []
    for d in range(2):
        in_specs += [shared(d), shared(d), shared(d), perdir(d), perdir(d), perdir(d)]
    return pl.pallas_call(
        functools.partial(_wkv_kernel, c=c),
        out_shape=(jax.ShapeDtypeStruct((rows, D_RWKV), F32),) * 2,
        grid=(geo.b, n_ctx + n_lat),
        in_specs=in_specs,
        out_specs=(shared(0), shared(1)),
        scratch_shapes=[pltpu.VMEM((RWKV_HEADS, PAIR, PAIR), F32)],
        compiler_params=_cparams("parallel", "arbitrary"),
        name="wkv_scan",
    )(v, r, kkn, lw, kd, bb, v, r, kkn, lw, kd, bb)


def _rwkv_out_kernel(yf_ref, yb_ref, r_ref, v_ref, kd_ref, gd_ref, lg_ref, lb_ref, rk_ref, gl_ref,
                     hsum_ref, o_ref):
    hsum = hsum_ref[...]
    inv_n = 1.0 / RWKV_HEAD_DIM
    y = yf_ref[...] + yb_ref[...]
    mu = _split_dot(y, hsum) * inv_n
    dlt = y - mu
    var = _split_dot(dlt * dlt, hsum) * inv_n
    yn = dlt * lax.rsqrt(var + GN_EPS) * lg_ref[...] + lb_ref[...]
    k_bonus = 0.5 * (kd_ref[0].astype(F32) + kd_ref[1].astype(F32))
    rsum = _split_dot(r_ref[...].astype(F32) * k_bonus * rk_ref[...], hsum)
    bonus = rsum * v_ref[...].astype(F32)
    gate = _dot(_sigmoid(gd_ref[...].astype(F32)).astype(BF16), gl_ref[...])
    o_ref[...] = ((yn + bonus) * gate).astype(BF16)


def _rwkv_output(yf, yb, r, v, kd, z, p, geo):
    rows = r.shape[0]
    t = geo.tp
    gcb = (geo.col0 + OFF_GD) // GATE_LORA
    row_spec = pl.BlockSpec((t, D_RWKV), lambda i: (i, 0))
    dir_spec = pl.BlockSpec((2, t, D_RWKV), lambda i: (0, i, 0))
    vec = pl.BlockSpec((1, D_RWKV), lambda i: (0, 0))
    return pl.pallas_call(
        _rwkv_out_kernel,
        out_shape=jax.ShapeDtypeStruct((rows, D_RWKV), BF16),
        grid=(rows // t,),
        in_specs=[row_spec, row_spec, row_spec, row_spec, dir_spec,
                  pl.BlockSpec((t, GATE_LORA), lambda i: (i, gcb)),
                  vec, vec, vec,
                  pl.BlockSpec((GATE_LORA, D_RWKV), lambda i: (0, 0)),
                  pl.BlockSpec((D_RWKV, D_RWKV), lambda i: (0, 0))],
        out_specs=row_spec,
        compiler_params=_cparams("parallel"),
        name="rwkv_output",
    )(yf, yb, r, v, kd, z, p['lnx_g'], p['lnx_b'], p['r_k'], p['g_lora'], geo.hsum)


def _attn_prep_kernel(q_ref, k_ref, cos_ref, sin_ref, qg_ref, kg_ref, qo_ref, ko_ref):
    cos = cos_ref[...]
    sin = sin_ref[...]
    lane = lax.broadcasted_iota(jnp.int32, cos.shape, 1)
    low_half = (lane % (2 * AXIS_FREQS)) < AXIS_FREQS

    def norm_rope(x, g, scale):
        x = x.astype(F32)
        xn = x * lax.rsqrt(jnp.mean(x * x, axis=-1, keepdims=True) + NORM_EPS) * g
        partner = jnp.where(low_half, pltpu.roll(xn, ATTN_HEAD_DIM - AXIS_FREQS, 1),
                            pltpu.roll(xn, AXIS_FREQS, 1))
        return ((xn * cos + partner * sin) * scale).astype(BF16)

    for h in range(ATTN_HEADS):
        sl = slice(h * ATTN_HEAD_DIM, (h + 1) * ATTN_HEAD_DIM)
        qo_ref[:, sl] = norm_rope(q_ref[:, sl], qg_ref[...], ATTN_SCALE)
    for h in range(ATTN_KV_HEADS):
        sl = slice(h * ATTN_HEAD_DIM, (h + 1) * ATTN_HEAD_DIM)
        ko_ref[:, sl] = norm_rope(k_ref[:, sl], kg_ref[...], 1.0)


def _attn_prep(z, cos_t, sin_t, p, geo):
    rows = z.shape[0]
    t = geo.tp
    qcb = (geo.col0 + OFF_Q) // D_ATTN
    kcb = (geo.col0 + OFF_KA) // D_ATTN_KV
    lat_tiles, lat_per = geo.r_lat // t, geo.s // t

    def tab(i):
        return (jnp.where(i < lat_tiles, i % lat_per, lat_per), 0)

    vec = pl.BlockSpec((1, ATTN_HEAD_DIM), lambda i: (0, 0))
    return pl.pallas_call(
        _attn_prep_kernel,
        out_shape=(jax.ShapeDtypeStruct((rows, D_ATTN), BF16),
                   jax.ShapeDtypeStruct((rows, D_ATTN_KV), BF16)),
        grid=(rows // t,),
        in_specs=[pl.BlockSpec((t, D_ATTN), lambda i: (i, qcb)),
                  pl.BlockSpec((t, D_ATTN_KV), lambda i: (i, kcb)),
                  pl.BlockSpec((t, ATTN_HEAD_DIM), tab),
                  pl.BlockSpec((t, ATTN_HEAD_DIM), tab),
                  vec, vec],
        out_specs=(pl.BlockSpec((t, D_ATTN), lambda i: (i, 0)),
                   pl.BlockSpec((t, D_ATTN_KV), lambda i: (i, 0))),
        compiler_params=_cparams("parallel"),
        name="attn_prep",
    )(z, z, cos_t, sin_t, p['q_norm_g'], p['k_norm_g'])


def _attn_kernel(*refs, n_seg):
    q_ref = refs[0]
    k_refs = refs[1:1 + n_seg]
    v_refs = refs[1 + n_seg:1 + 2 * n_seg]
    o_ref = refs[1 + 2 * n_seg]
    for g in range(GQA_GROUP):
        sl = slice(g * ATTN_HEAD_DIM, (g + 1) * ATTN_HEAD_DIM)
        q = q_ref[:, sl]
        scores = [_dot_nt(q, k_ref[...]) for k_ref in k_refs]
        m = scores[0].max(axis=-1, keepdims=True)
        for s in scores[1:]:
            m = jnp.maximum(m, s.max(axis=-1, keepdims=True))
        denom = None
        acc = None
        for s, v_ref in zip(scores, v_refs):
            e = jnp.exp(s - m)
            es = e.sum(axis=-1, keepdims=True)
            pv = _dot(e.astype(BF16), v_ref[...])
            denom = es if denom is None else denom + es
            acc = pv if acc is None else acc + pv
        o_ref[:, sl] = (acc / denom).astype(BF16)


def _attention(qn, kn, z, geo, latent):
    gw = GQA_GROUP * ATTN_HEAD_DIM
    vcb = (geo.col0 + OFF_VA) // ATTN_HEAD_DIM
    ctx_blk0 = geo.r_lat // geo.lc
    k_ctx = pl.BlockSpec((geo.lc, ATTN_HEAD_DIM), lambda b, h, i: (ctx_blk0 + b, h))
    v_ctx = pl.BlockSpec((geo.lc, ATTN_HEAD_DIM), lambda b, h, i: (ctx_blk0 + b, vcb + h))
    if latent:
        tq = geo.tq
        per = geo.s // tq
        q_spec = pl.BlockSpec((tq, gw), lambda b, h, i: (b * per + i, h))
        k_lat = pl.BlockSpec((geo.s, ATTN_HEAD_DIM), lambda b, h, i: (b, h))
        v_lat = pl.BlockSpec((geo.s, ATTN_HEAD_DIM), lambda b, h, i: (b, vcb + h))
        in_specs = [q_spec, k_ctx, k_lat, v_ctx, v_lat]
        args = (qn, kn, kn, z, z)
        n_seg, out_rows = 2, geo.r_lat
        o_spec = q_spec
    else:
        tq, per = geo.lc, 1
        q_spec = pl.BlockSpec((tq, gw), lambda b, h, i: (ctx_blk0 + b, h))
        in_specs = [q_spec, k_ctx, v_ctx]
        args = (qn, kn, z)
        n_seg, out_rows = 1, geo.r_ctx
        o_spec = pl.BlockSpec((tq, gw), lambda b, h, i: (b, h))
    return pl.pallas_call(
        functools.partial(_attn_kernel, n_seg=n_seg),
        out_shape=jax.ShapeDtypeStruct((out_rows, D_ATTN), BF16),
        grid=(geo.b, ATTN_KV_HEADS, per),
        in_specs=in_specs,
        out_specs=o_spec,
        compiler_params=_cparams("parallel", "parallel", "parallel"),
        name="attention_lat" if latent else "attention_ctx",
    )(*args)


def _mm_kernel(x_ref, w_ref, o_ref):
    o_ref[...] = _dot(x_ref[...], w_ref[...]).astype(o_ref.dtype)


def _fourier_channels(z, wcs, geo):
    rows = z.shape[0]
    t = geo.tp
    fcb = (geo.col0 + OFF_FOUR) // D_FOURIER
    return pl.pallas_call(
        _mm_kernel,
        out_shape=jax.ShapeDtypeStruct((rows, 2 * D_FOURIER), BF16),
        grid=(rows // t,),
        in_specs=[pl.BlockSpec((t, D_FOURIER), lambda i: (i, fcb)),
                  pl.BlockSpec((D_FOURIER, 2 * D_FOURIER), lambda i: (0, 0))],
        out_specs=pl.BlockSpec((t, 2 * D_FOURIER), lambda i: (i, 0)),
        compiler_params=_cparams("parallel"),
        name="fourier_channels",
    )(z, wcs)


def _dft_kernel(cl_ref, sl_ref, xc_ref, xs_ref, o_ref):
    o_ref[...] = (_dot(cl_ref[...], xc_ref[...]) - _dot(sl_ref[...], xs_ref[...])).astype(BF16)


def _fourier_positions(xcs, cl, sl, n, blk0, nb, tmf):
    per = n // tmf
    return pl.pallas_call(
        _dft_kernel,
        out_shape=jax.ShapeDtypeStruct((nb * n, D_FOURIER), BF16),
        grid=(per, nb),
        in_specs=[pl.BlockSpec((tmf, n), lambda i, b: (i, 0)),
                  pl.BlockSpec((tmf, n), lambda i, b: (i, 0)),
                  pl.BlockSpec((n, D_FOURIER), lambda i, b: (blk0 + b, 0)),
                  pl.BlockSpec((n, D_FOURIER), lambda i, b: (blk0 + b, 1))],
        out_specs=pl.BlockSpec((tmf, D_FOURIER), lambda i, b: (b * per + i, 0)),
        compiler_params=_cparams("parallel", "parallel"),
        name="fourier_positions",
    )(cl, sl, xcs, xcs)


def _merge_kernel(rw_ref, fo_ref, at_ref, ga_ref, gf_ref, gc_ref, wr_ref, wf_ref, wa_ref, o_ref):
    y = _sigmoid(ga_ref[...].astype(F32)) * _dot(rw_ref[...], wr_ref[...])
    y += _sigmoid(gf_ref[...].astype(F32)) * _dot(fo_ref[...], wf_ref[...])
    y += _sigmoid(gc_ref[...].astype(F32)) * _dot(at_ref[...], wa_ref[...])
    o_ref[...] = y.astype(BF16)


def _merge(rw, fo, at, z, p, geo, n_rows):
    d = geo.d
    tm = geo.tm
    tn = min(1024, d)
    nj = d // tn
    return pl.pallas_call(
        _merge_kernel,
        out_shape=jax.ShapeDtypeStruct((n_rows, d), BF16),
        grid=(nj, n_rows // tm),
        in_specs=[pl.BlockSpec((tm, D_RWKV), lambda j, i: (i, 0)),
                  pl.BlockSpec((tm, D_FOURIER), lambda j, i: (i, 0)),
                  pl.BlockSpec((tm, D_ATTN), lambda j, i: (i, 0)),
                  pl.BlockSpec((tm, tn), lambda j, i: (i, j)),
                  pl.BlockSpec((tm, tn), lambda j, i: (i, nj + j)),
                  pl.BlockSpec((tm, tn), lambda j, i: (i, 2 * nj + j)),
                  pl.BlockSpec((D_RWKV, tn), lambda j, i: (0, j)),
                  pl.BlockSpec((D_FOURIER, tn), lambda j, i: (0, j)),
                  pl.BlockSpec((D_ATTN, tn), lambda j, i: (0, j))],
        out_specs=pl.BlockSpec((tm, tn), lambda j, i: (i, j)),
        compiler_params=_cparams("parallel", "parallel"),
        name="merge",
    )(rw, fo, at, z, z, z, p['w_br_rwkv'], p['w_br_fourier'], p['w_br_attn'])


def _outres_kernel(y_ref, w_ref, x_ref, gt_ref, o_ref):
    o_ref[...] = x_ref[...] + gt_ref[...] * _dot(y_ref[...], w_ref[...])


def _out_proj_residual(y, w, xs, mod4, geo, n_rows):
    d = geo.d
    tm = geo.tm
    tn = min(1024, d)
    bidx = geo.batch_of_tile(tm)
    return pl.pallas_call(
        _outres_kernel,
        out_shape=jax.ShapeDtypeStruct((n_rows, d), F32),
        grid=(d // tn, n_rows // tm),
        in_specs=[pl.BlockSpec((tm, d), lambda j, i: (i, 0)),
                  pl.BlockSpec((d, tn), lambda j, i: (0, j)),
                  pl.BlockSpec((tm, tn), lambda j, i: (i, j)),
                  pl.BlockSpec((None, None, 1, tn), lambda j, i: (bidx(i), 2, 0, j))],
        out_specs=pl.BlockSpec((tm, tn), lambda j, i: (i, j)),
        compiler_params=_cparams("parallel", "parallel"),
        name="out_proj_residual",
    )(y, w, xs, mod4)


def _expert_ffn(h, wgu, bgu, wd, bd, f):
    gu = _dot(h, wgu) + bgu
    gate = jnp.minimum(gu[:, :f], SWIGLU_LIMIT)
    up = jnp.clip(gu[:, f:], -SWIGLU_LIMIT, SWIGLU_LIMIT)
    act = (up + 1.0) * gate * _sigmoid(SWIGLU_ALPHA * gate)
    return _dot(act.astype(BF16), wd) + bd


def _route(sel, n, t, n_exp):
    p = TOP_K * n
    max_used = p // t + n_exp
    assert max_used % 2 == 0
    n_steps = max_used // 2 + 1
    n_tiles = 2 * n_steps
    e_flat = sel[:, :TOP_K].astype(jnp.int32).reshape(p)
    w_flat = sel[:, TOP_K:2 * TOP_K].reshape(p)
    e_sorted, pair_sorted, w_sorted = lax.sort(
        (e_flat, jnp.arange(p, dtype=jnp.int32), w_flat), num_keys=1, is_stable=True)
    experts = jnp.arange(n_exp, dtype=jnp.int32)
    bounds = jnp.searchsorted(e_sorted, jnp.arange(n_exp + 1, dtype=jnp.int32)).astype(jnp.int32)
    cstart, counts = bounds[:-1], bounds[1:] - bounds[:-1]
    ptiles = (counts + t - 1) // t
    tend = jnp.cumsum(ptiles)
    tstart = tend - ptiles
    n_used = tend[-1]
    tile_ids = jnp.arange(n_tiles, dtype=jnp.int32)
    used = tile_ids < n_used
    last_e = jnp.max(jnp.where(counts > 0, experts, 0))
    te = jnp.sum((tend[None, :] <= tile_ids[:, None]).astype(jnp.int32), axis=1)
    te = jnp.where(used, jnp.minimum(te, n_exp - 1), last_e)
    rank0 = (tile_ids - tstart[te]) * t
    n_valid = jnp.where(used, jnp.clip(counts[te] - rank0, 0, t), 0)
    src0 = jnp.clip(cstart[te] + rank0, 0, p)
    take = jax.vmap(lambda a, s0: lax.dynamic_slice(a, (s0,), (t,)), in_axes=(None, 0))
    pair = take(jnp.pad(pair_sorted, (0, t)), src0)
    w_tile = take(jnp.pad(w_sorted, (0, t)), src0)
    lane = jnp.arange(t, dtype=jnp.int32)[None, :]
    valid = lane < n_valid[:, None]
    token, k = pair // TOP_K, pair % TOP_K
    tok = jnp.where(valid, token, 0)
    spare = p + n_exp * t + lane
    dest_pad = jnp.where(used[:, None], p + te[:, None] * t + (lane - n_valid[:, None]), spare)
    dest = jnp.where(valid, k * n + token, dest_pad)
    w = jnp.where(valid, w_tile, 0.0)
    dest_prev = jnp.concatenate([spare, dest[1:-1:2]], axis=0)
    return (te, n_used.reshape(1).astype(jnp.int32), tok.reshape(n_tiles, 1, t),
            dest.reshape(n_tiles, 1, t), dest_prev.reshape(n_steps, 1, t), w.reshape(n_tiles * t, 1))


def _moe_routed_kernel(te_ref, nu_ref, toka_ref, tokb_ref, tokn_ref, dsta_ref, dstp_ref, wa_ref, wb_ref,
                       h_hbm, wgu_a, bgu_a, wd_a, bd_a, wgu_b, bgu_b, wd_b, bd_b, o_hbm,
                       ga, gb, oa, ob, gsem, ssem, *, f, t, dump0, n_dump):
    j = pl.program_id(0)
    n_used = nu_ref[0]

    def gather(idx_ref, buf, sem):
        for r in range(t):
            pltpu.make_async_copy(h_hbm.at[pl.ds(idx_ref[0, 0, r], 1)], buf.at[pl.ds(r, 1)], sem).start()

    def scatter(buf, idx_ref, sem):
        for r in range(t):
            pltpu.make_async_copy(buf.at[pl.ds(r, 1)], o_hbm.at[pl.ds(idx_ref[0, 0, r], 1)], sem).start()

    def wait_rows_in(buf, sem):
        pltpu.make_async_copy(h_hbm.at[pl.ds(0, t)], buf, sem).wait()

    def wait_rows_out(buf, sem):
        pltpu.make_async_copy(buf, o_hbm.at[pl.ds(0, t)], sem).wait()

    @pl.when(j == 0)
    def _():
        def body(r, carry):
            pltpu.make_async_copy(h_hbm.at[pl.ds(toka_ref[0, 0, r], 1)], ga.at[pl.ds(r, 1)],
                                  gsem.at[0]).start()
            return carry
        lax.fori_loop(0, t, body, 0)
        ob[...] = jnp.zeros_like(ob)
        fills = [pltpu.make_async_copy(ob, o_hbm.at[pl.ds(dump0 + e * t, t)], ssem.at[1])
                 for e in range(n_dump - 1)]
        for c in fills:
            c.start()
        for c in fills:
            c.wait()
        pltpu.make_async_copy(ob, o_hbm.at[pl.ds(dump0 + (n_dump - 1) * t, t)], ssem.at[0]).start()

    @pl.when(2 * j <= n_used)
    def _():
        wait_rows_in(ga, gsem.at[0])
        scatter(ob, dstp_ref, ssem.at[1])
        gather(tokb_ref, gb, gsem.at[1])
        ya = _expert_ffn(ga[...].astype(BF16), wgu_a[...], bgu_a[...], wd_a[...], bd_a[...], f) * wa_ref[...]
        wait_rows_out(oa, ssem.at[0])
        oa[...] = ya
        wait_rows_in(gb, gsem.at[1])
        scatter(oa, dsta_ref, ssem.at[0])
        gather(tokn_ref, ga, gsem.at[0])
        yb = _expert_ffn(gb[...].astype(BF16), wgu_b[...], bgu_b[...], wd_b[...], bd_b[...], f) * wb_ref[...]
        wait_rows_out(ob, ssem.at[1])
        ob[...] = yb

    @pl.when(j == pl.num_programs(0) - 1)
    def _():
        wait_rows_in(ga, gsem.at[0])
        wait_rows_out(oa, ssem.at[0])


def _moe_routed(h2, sel, wgu, bgu, wd, bd, geo, n_rows):
    d = geo.d
    n_exp, _, f2 = wgu.shape
    f = f2 // 2
    t = geo.t_moe
    te, n_used, tok, dest, dest_prev, w = _route(sel, n_rows, t, n_exp)
    n_tiles, n_steps = tok.shape[0], dest_prev.shape[0]
    n_dump = n_exp + 2
    smem = functools.partial(pl.BlockSpec, (1, 1, t), memory_space=pltpu.SMEM)

    def weights(which):
        return [pl.BlockSpec((None, d, f2), lambda j, te, nu: (te[2 * j + which], 0, 0)),
                pl.BlockSpec((None, 1, f2), lambda j, te, nu: (te[2 * j + which], 0, 0)),
                pl.BlockSpec((None, f, d), lambda j, te, nu: (te[2 * j + which], 0, 0)),
                pl.BlockSpec((None, 1, d), lambda j, te, nu: (te[2 * j + which], 0, 0))]

    grid_spec = pltpu.PrefetchScalarGridSpec(
        num_scalar_prefetch=2,
        grid=(n_steps,),
        in_specs=[smem(lambda j, te, nu: (2 * j, 0, 0)),
                  smem(lambda j, te, nu: (2 * j + 1, 0, 0)),
                  smem(lambda j, te, nu: (jnp.minimum(2 * j + 2, n_tiles - 1), 0, 0)),
                  smem(lambda j, te, nu: (2 * j, 0, 0)),
                  smem(lambda j, te, nu: (j, 0, 0)),
                  pl.BlockSpec((t, 1), lambda j, te, nu: (2 * j, 0)),
                  pl.BlockSpec((t, 1), lambda j, te, nu: (2 * j + 1, 0)),
                  pl.BlockSpec(memory_space=pl.ANY)] + weights(0) + weights(1),
        out_specs=pl.BlockSpec(memory_space=pl.ANY),
        scratch_shapes=[pltpu.VMEM((t, d), F32)] * 4
                       + [pltpu.SemaphoreType.DMA((2,)), pltpu.SemaphoreType.DMA((2,))])
    return pl.pallas_call(
        functools.partial(_moe_routed_kernel, f=f, t=t, dump0=TOP_K * n_rows, n_dump=n_dump),
        out_shape=jax.ShapeDtypeStruct((TOP_K * n_rows + n_dump * t, d), F32),
        grid_spec=grid_spec,
        compiler_params=pltpu.CompilerParams(dimension_semantics=("arbitrary",),
                                             vmem_limit_bytes=VMEM_LIMIT, disable_bounds_checks=True),
        name="moe_routed",
    )(te, n_used, tok, tok, tok, dest, dest_prev, w, w, h2, wgu, bgu, wd, bd, wgu, bgu, wd, bd)


def _combine_kernel(x_ref, gt_ref, y0_ref, y1_ref, y2_ref, y3_ref, o_ref):
    acc = (y0_ref[...] + y1_ref[...]) + (y2_ref[...] + y3_ref[...])
    o_ref[...] = x_ref[...] + gt_ref[...] * acc


def _moe_combine(y4, xs, mod4, geo, n_rows):
    d = geo.d
    tm = geo.t_moe
    per = n_rows // tm
    bidx = geo.batch_of_tile(tm)

    def part(k):
        return pl.BlockSpec((tm, d), lambda i: (k * per + i, 0))

    return pl.pallas_call(
        _combine_kernel,
        out_shape=jax.ShapeDtypeStruct((n_rows, d), F32),
        grid=(per,),
        in_specs=[pl.BlockSpec((tm, d), lambda i: (i, 0)),
                  pl.BlockSpec((None, None, 1, d), lambda i: (bidx(i), 5, 0, 0)),
                  part(0), part(1), part(2), part(3)],
        out_specs=pl.BlockSpec((tm, d), lambda i: (i, 0)),
        compiler_params=_cparams("parallel"),
        name="moe_combine",
    )(xs, mod4, y4, y4, y4, y4)


class _Geometry:
    def __init__(self, b, s, lc, d):
        self.b, self.s, self.lc, self.d = b, s, lc, d
        self.r_lat, self.r_ctx = b * s, b * lc
        self.rows = self.r_lat + self.r_ctx
        self.col0 = 3 * d
        assert self.col0 % 1024 == 0, "gate columns must end on a 1024-column boundary"
        self.tm = min(1024, s, self.r_ctx)
        self.tp = min(256, s, lc)
        self.tq = min(512, s)
        self.chunk = min(WKV_CHUNK, lc, s)
        self.t_moe = min(256, self.tm)
        for t in (self.tm,):
            assert s % t == 0 and self.r_ctx % t == 0
        assert s % self.tp == 0 and lc % self.tp == 0 and self.tp % BF16_SUBLANES == 0
        assert s % self.chunk == 0 and lc % self.chunk == 0 and s % GRID_W == 0
        self.hsum = jnp.asarray(np.kron(np.eye(RWKV_HEADS), np.ones((RWKV_HEAD_DIM,) * 2)), BF16)

    def batch_of_tile(self, tm):
        per, nb = self.s // tm, self.b
        return lambda i: jnp.minimum(i // per, nb)


def _dft_tables(n):
    j = np.arange(n, dtype=np.int64)
    ang = 2.0 * np.pi * ((j[:, None] * j[None, :]) % n).astype(np.float64) / n
    return np.cos(ang) / np.sqrt(n), np.sin(ang) / np.sqrt(n)


def _rope_tables(s, t):
    pos = np.arange(s)
    inv_freq = ROPE_THETA ** (-np.arange(AXIS_FREQS, dtype=np.float32) / AXIS_FREQS)
    ang_r = (pos // GRID_W).astype(np.float32)[:, None] * inv_freq.astype(np.float32)
    ang_c = (pos % GRID_W).astype(np.float32)[:, None] * inv_freq.astype(np.float32)
    cr, sr, cc, sc = np.cos(ang_r), np.sin(ang_r), np.cos(ang_c), np.sin(ang_c)
    cos = np.concatenate([cr, cr, cc, cc], axis=1)
    sin = np.concatenate([-sr, sr, -sc, sc], axis=1)
    cos = np.concatenate([cos, np.ones((t, ATTN_HEAD_DIM))], axis=0)
    sin = np.concatenate([sin, np.zeros((t, ATTN_HEAD_DIM))], axis=0)
    return jnp.asarray(cos, F32), jnp.asarray(sin, F32)


def _permute_w_in(w_in, d):
    o = np.cumsum([0, D_RWKV, D_RWKV, LORA, LORA, LORA, LORA, D_ATTN_KV, D_ATTN_KV, D_RWKV, GATE_LORA,
                   D_ATTN, D_FOURIER, 3 * d])
    seg = lambda i: w_in[..., o[i]:o[i + 1]]
    pad = lambda a: jnp.pad(a, ((0, 0), (0, 0), (0, LORA_PAD - LORA)))
    parts = [seg(12), seg(0), seg(1), pad(seg(2)), pad(seg(3)), pad(seg(4)), pad(seg(5)), seg(8), seg(10),
             seg(11), seg(6), seg(7), seg(9)]
    w = jnp.concatenate(parts, axis=-1)
    n = w.shape[-1]
    n_pad = -(-n // W_IN_TN) * W_IN_TN
    return jnp.pad(w, ((0, 0), (0, 0), (0, n_pad - n))).astype(BF16)


def _split_hi_lo(w):
    hi = w.astype(BF16)
    return hi, (w - hi.astype(F32)).astype(BF16)


def kernel(x, c, ctx, c_ctx, ada_w, ada_b, norm1_g, norm2_g, w_in, rwkv_conv, w0, w_lora, a0, a_lora, g_lora, k_k, k_a, r_k, lnx_g, lnx_b, q_norm_g, k_norm_g, w_br_rwkv, w_br_fourier, w_br_attn, w_out, router_w, router_b, exp_w_gu, exp_b_gu, exp_w_down, exp_b_down):
    b, s, d = x.shape
    lc = ctx.shape[1]
    depth = w_in.shape[0]
    n_exp = router_w.shape[-1]
    geo = _Geometry(b, s, lc, d)

    w_in_p = _permute_w_in(w_in, d)
    lora_pad = ((0, 0), (0, 0), (0, LORA_PAD - LORA), (0, 0))
    w_lora_p = jnp.pad(w_lora, lora_pad).astype(BF16)
    a_lora_p = jnp.pad(a_lora, lora_pad).astype(BF16)
    g_lora_b = g_lora.astype(BF16)
    wbr_r, wbr_f, wbr_a, w_out_b = (w.astype(BF16) for w in (w_br_rwkv, w_br_fourier, w_br_attn, w_out))
    rw_p = jnp.pad(router_w, ((0, 0), (0, 0), (0, ROUTER_PAD - n_exp)))
    rw_hi, rw_lo = _split_hi_lo(rw_p)
    rb_p = jnp.pad(router_b, ((0, 0), (0, ROUTER_PAD - n_exp)), constant_values=NEG_BIG)
    wgu_b, wd_b = exp_w_gu.astype(BF16), exp_w_down.astype(BF16)
    cos_t, sin_t = _rope_tables(s, geo.tp)
    cc, sc = _dft_tables(FOURIER_GROUP_DIM)
    groups = D_FOURIER // FOURIER_GROUP_DIM
    wcs = jnp.asarray(np.concatenate([np.kron(np.eye(groups), cc), np.kron(np.eye(groups), sc)], axis=1), BF16)
    cl_lat, sl_lat = (jnp.asarray(m, BF16) for m in _dft_tables(s))
    cl_ctx, sl_ctx = (jnp.asarray(m, BF16) for m in _dft_tables(lc))

    mod_rows = -(-(b + 1) // BF16_SUBLANES) * BF16_SUBLANES
    c_all = jnp.concatenate([c, c_ctx[None], jnp.zeros((mod_rows - b - 1, d), F32)], axis=0)
    mod = _adaln(c_all, ada_w, ada_b).reshape(depth, mod_rows, 6, 1, d)

    xs = jnp.concatenate([x.reshape(b * s, d), ctx.reshape(b * lc, d)], axis=0)
    for l in range(depth):
        last = l == depth - 1
        mod4 = mod[l]
        p = dict(conv=rwkv_conv[l], w0=w0[l][:, None, :], w_lora=w_lora_p[l], a0=a0[l][:, None, :],
                 a_lora=a_lora_p[l], k_k=k_k[l][None], k_a=k_a[l][None], r_k=r_k[l][None],
                 lnx_g=lnx_g[l][None], lnx_b=lnx_b[l][None], g_lora=g_lora_b[l],
                 q_norm_g=q_norm_g[l][None], k_norm_g=k_norm_g[l][None],
                 w_br_rwkv=wbr_r[l], w_br_fourier=wbr_f[l], w_br_attn=wbr_a[l])
        n_rows = geo.r_lat if last else geo.rows

        z = _norm_in_proj(xs, norm1_g[l], mod4, w_in_p[l], geo)

        v_c, r_c, kkn, lw, kd, bb = _rwkv_prepare(z, p, geo)
        yf, yb = _wkv_scan(v_c, r_c, kkn, lw, kd, bb, geo)
        rw = _rwkv_output(yf, yb, r_c, v_c, kd, z, p, geo)

        qn, kn = _attn_prep(z, cos_t, sin_t, p, geo)
        att = _attention(qn, kn, z, geo, latent=True)

        xcs = _fourier_channels(z, wcs, geo)
        fo = _fourier_positions(xcs, cl_lat, sl_lat, s, 0, b, min(1024, s))
        if not last:
            att = jnp.concatenate([att, _attention(qn, kn, z, geo, latent=False)], axis=0)
            fo_c = _fourier_positions(xcs, cl_ctx, sl_ctx, lc, geo.r_lat // lc, b, lc)
            fo = jnp.concatenate([fo, fo_c], axis=0)

        ym = _merge(rw, fo, att, z, p, geo, n_rows)
        xs = _out_proj_residual(ym, w_out_b[l], xs, mod4, geo, n_rows)

        h2, sel = _norm_router(xs, norm2_g[l], mod4, rw_hi[l], rw_lo[l], rb_p[l][None], geo, n_rows)
        y4 = _moe_routed(h2, sel, wgu_b[l], exp_b_gu[l][:, None, :], wd_b[l], exp_b_down[l][:, None, :],
                         geo, n_rows)
        xs = _moe_combine(y4, xs, mod4, geo, n_rows)
    return xs[:geo.r_lat].reshape(b, s, d)
```

```python
import functools

import numpy as np
import jax
import jax.numpy as jnp
from jax import lax
from jax.experimental import pallas as pl
from jax.experimental.pallas import tpu as pltpu

F32 = jnp.float32
BF16 = jnp.bfloat16

NORM_EPS = 1e-6
GN_EPS = 64e-5
RWKV_HEADS = 8
RWKV_HEAD_DIM = 64
D_RWKV = RWKV_HEADS * RWKV_HEAD_DIM
LORA = 96
GATE_LORA = 256
D_FOURIER = 512
FOURIER_GROUP_DIM = 128
ATTN_HEADS = 8
ATTN_KV_HEADS = 2
ATTN_HEAD_DIM = 128
GQA_GROUP = ATTN_HEADS // ATTN_KV_HEADS
D_ATTN = ATTN_HEADS * ATTN_HEAD_DIM
D_ATTN_KV = ATTN_KV_HEADS * ATTN_HEAD_DIM
ATTN_SCALE = ATTN_HEAD_DIM ** -0.5
GRID_W = 64
ROPE_THETA = 10000.0
AXIS_FREQS = ATTN_HEAD_DIM // 4
TOP_K = 4
SWIGLU_LIMIT = 7.0
SWIGLU_ALPHA = 1.702

LANES = 128
BF16_SUBLANES = 16
VMEM_LIMIT = 56 * 1024 * 1024

LORA_PAD = LANES
W_IN_TN = 1536
WKV_CHUNK = 64
PAIR = 2 * RWKV_HEAD_DIM
ROUTER_PAD = LANES
NEG_BIG = -1e30
DMA_UNROLL = 32

OFF_K, OFF_V, OFF_LORA, OFF_R, OFF_Q, OFF_FOUR, OFF_KA, OFF_VA, OFF_GD, OFF_END = (
    0, 512, 1024, 1536, 2048, 3072, 3584, 3840, 4096, 4352)


def _cparams(*sem):
    return pltpu.CompilerParams(dimension_semantics=sem, vmem_limit_bytes=VMEM_LIMIT)


def _dot(a, b):
    return jnp.dot(a, b, preferred_element_type=F32)


def _dot_nt(a, b):
    return lax.dot_general(a, b, (((1,), (1,)), ((), ())), preferred_element_type=F32)


def _dot_tn(a, b):
    return lax.dot_general(a, b, (((0,), (0,)), ((), ())), preferred_element_type=F32)


def _split_dot(x, g):
    hi = x.astype(BF16)
    lo = (x - hi.astype(F32)).astype(BF16)
    return _dot(hi, g) + _dot(lo, g)


def _sigmoid(x):
    return 1.0 / (1.0 + jnp.exp(-x))


def _adaln_kernel(c_ref, w_ref, b_ref, o_ref):
    c = c_ref[...]
    s = (c * _sigmoid(c)).astype(BF16)
    o_ref[...] = _dot(s, w_ref[...].astype(BF16)) + b_ref[...]


def _adaln(c_all, ada_w, ada_b):
    depth, d, n = ada_w.shape
    rows = c_all.shape[0]
    tn = 1024 if n % 1024 == 0 else n
    return pl.pallas_call(
        _adaln_kernel,
        out_shape=jax.ShapeDtypeStruct((depth, rows, n), F32),
        grid=(depth, n // tn),
        in_specs=[
            pl.BlockSpec((rows, d), lambda l, j: (0, 0)),
            pl.BlockSpec((None, d, tn), lambda l, j: (l, 0, j)),
            pl.BlockSpec((None, 1, tn), lambda l, j: (l, 0, j)),
        ],
        out_specs=pl.BlockSpec((None, rows, tn), lambda l, j: (l, 0, j)),
        compiler_params=_cparams("parallel", "parallel"),
        name="adaln",
    )(c_all, ada_w, ada_b.reshape(depth, 1, n))


def _modulated_norm(x, g, sc, sh):
    ms = jnp.mean(x * x, axis=-1, keepdims=True)
    return x * lax.rsqrt(ms + NORM_EPS) * g * (1.0 + sc) + sh


def _normmm_kernel(x_ref, g_ref, sc_ref, sh_ref, w_ref, o_ref, h_ref):
    @pl.when(pl.program_id(1) == 0)
    def _():
        h_ref[...] = _modulated_norm(x_ref[...], g_ref[...], sc_ref[...], sh_ref[...]).astype(BF16)

    o_ref[...] = _dot(h_ref[...], w_ref[...]).astype(o_ref.dtype)


def _norm_in_proj(xs, gain, mod4, w, geo):
    rows, d = xs.shape
    n = w.shape[1]
    tm, tn = geo.tm, W_IN_TN
    bidx = geo.batch_of_tile(tm)
    return pl.pallas_call(
        _normmm_kernel,
        out_shape=jax.ShapeDtypeStruct((rows, n), BF16),
        grid=(rows // tm, n // tn),
        in_specs=[
            pl.BlockSpec((tm, d), lambda i, j: (i, 0)),
            pl.BlockSpec((1, d), lambda i, j: (0, 0)),
            pl.BlockSpec((None, None, 1, d), lambda i, j: (bidx(i), 1, 0, 0)),
            pl.BlockSpec((None, None, 1, d), lambda i, j: (bidx(i), 0, 0, 0)),
            pl.BlockSpec((d, tn), lambda i, j: (0, j)),
        ],
        out_specs=pl.BlockSpec((tm, tn), lambda i, j: (i, j)),
        scratch_shapes=[pltpu.VMEM((tm, d), BF16)],
        compiler_params=_cparams("parallel", "arbitrary"),
        name="norm_in_proj",
    )(xs, gain.reshape(1, d), mod4, mod4, w)


def _pack_halves(x):
    half = x.shape[1] // 2
    xb = x.astype(BF16).astype(F32)
    lo = lax.shift_right_logical(lax.bitcast_convert_type(xb[:, :half], jnp.uint32), jnp.uint32(16))
    hi = lax.bitcast_convert_type(xb[:, half:], jnp.uint32) & jnp.uint32(0xFFFF0000)
    return lo | hi


def _unpack_halves(words):
    lo = lax.bitcast_convert_type(lax.shift_left(words, jnp.uint32(16)), F32)
    hi = lax.bitcast_convert_type(words & jnp.uint32(0xFFFF0000), F32)
    return lo, hi


def _norm_router_kernel(x_ref, g_ref, sc_ref, sh_ref, wh_ref, wl_ref, rb_ref, h_ref, sel_ref, cnt_ref):
    @pl.when(pl.program_id(0) == 0)
    def _():
        cnt_ref[...] = jnp.zeros_like(cnt_ref)

    h = _modulated_norm(x_ref[...], g_ref[...], sc_ref[...], sh_ref[...])
    h_ref[...] = _pack_halves(h)
    hi = h.astype(BF16)
    lo = (h - hi.astype(F32)).astype(BF16)
    logits = _dot(hi, wh_ref[...]) + _dot(lo, wh_ref[...]) + _dot(hi, wl_ref[...]) + rb_ref[...]
    lane = lax.broadcasted_iota(jnp.int32, logits.shape, 1).astype(F32)
    work = logits
    vals, firsts = [], []
    for _ in range(TOP_K):
        m = jnp.max(work, axis=-1, keepdims=True)
        first = jnp.min(jnp.where(work == m, lane, float(ROUTER_PAD)), axis=-1, keepdims=True)
        vals.append(m)
        firsts.append(first)
        work = jnp.where(lane == first, 2.0 * NEG_BIG, work)
    exps = [jnp.exp(v - vals[0]) for v in vals]
    denom = exps[0] + exps[1] + exps[2] + exps[3]
    table = jnp.zeros_like(logits)
    picked = jnp.zeros_like(logits)
    for k in range(TOP_K):
        table = jnp.where(lane == float(k), firsts[k], table)
        table = jnp.where(lane == float(TOP_K + k), exps[k] / denom, table)
        picked = picked + jnp.where(lane == firsts[k], 1.0, 0.0)
    sel_ref[...] = table
    cnt_ref[...] += jnp.sum(picked, axis=0, keepdims=True)


def _norm_router(xs, gain, mod4, rw_hi, rw_lo, rb, geo, n_rows):
    d = xs.shape[1]
    tm = geo.tm
    bidx = geo.batch_of_tile(tm)
    return pl.pallas_call(
        _norm_router_kernel,
        out_shape=(jax.ShapeDtypeStruct((n_rows, d // 2), jnp.uint32),
                   jax.ShapeDtypeStruct((n_rows, ROUTER_PAD), F32),
                   jax.ShapeDtypeStruct((1, ROUTER_PAD), F32)),
        grid=(n_rows // tm,),
        in_specs=[
            pl.BlockSpec((tm, d), lambda i: (i, 0)),
            pl.BlockSpec((1, d), lambda i: (0, 0)),
            pl.BlockSpec((None, None, 1, d), lambda i: (bidx(i), 4, 0, 0)),
            pl.BlockSpec((None, None, 1, d), lambda i: (bidx(i), 3, 0, 0)),
            pl.BlockSpec((d, ROUTER_PAD), lambda i: (0, 0)),
            pl.BlockSpec((d, ROUTER_PAD), lambda i: (0, 0)),
            pl.BlockSpec((1, ROUTER_PAD), lambda i: (0, 0)),
        ],
        out_specs=(pl.BlockSpec((tm, d // 2), lambda i: (i, 0)),
                   pl.BlockSpec((tm, ROUTER_PAD), lambda i: (i, 0)),
                   pl.BlockSpec((1, ROUTER_PAD), lambda i: (0, 0))),
        compiler_params=_cparams("arbitrary"),
        name="norm_router",
    )(xs, gain.reshape(1, d), mod4, mod4, rw_hi, rw_lo, rb)


def _prep_kernel(zk_ref, zv_ref, zl_ref, zr_ref, pk_ref, pv_ref, pr_ref, nk_ref, nv_ref, nr_ref,
                 conv_ref, w0_ref, wl_ref, a0_ref, al_ref, kk_ref, ka_ref, hsum_ref,
                 v_out, r_out, kkn_out, lw_out, kd_out, bb_out, *, t, lat_tiles, lat_per, ctx_per):
    i = pl.program_id(0)
    is_lat = i < lat_tiles
    per = jnp.where(is_lat, lat_per, ctx_per)
    j = jnp.where(is_lat, i, i - lat_tiles) % per
    first = j == 0
    last = j == per - 1
    row = lax.broadcasted_iota(jnp.int32, (t, 1), 0)

    def conv(z_ref, p_ref, n_ref, which):
        z = z_ref[...].astype(F32)
        prev_row = p_ref[...].astype(F32)[BF16_SUBLANES - 1:BF16_SUBLANES, :]
        next_row = n_ref[...].astype(F32)[0:1, :]
        prev_row = jnp.where(first, 0.0, prev_row)
        next_row = jnp.where(last, 0.0, next_row)
        zm = jnp.where(row == 0, prev_row, pltpu.roll(z, 1, 0))
        zp = jnp.where(row == t - 1, next_row, pltpu.roll(z, t - 1, 0))
        w = conv_ref[which]
        return zm * w[0:1] + z * w[1:2] + zp * w[2:3]

    k = conv(zk_ref, pk_ref, nk_ref, 0)
    v = conv(zv_ref, pv_ref, nv_ref, 1)
    r = conv(zr_ref, pr_ref, nr_ref, 2)
    v_out[...] = v.astype(BF16)
    r_out[...] = r.astype(BF16)
    kkv = k * kk_ref[...]
    ss = _split_dot(kkv * kkv, hsum_ref[...])
    kkn = kkv * lax.rsqrt(jnp.maximum(ss, 1e-24))
    kkn_out[...] = kkn.astype(BF16)
    zl = zl_ref[...]
    for d in range(2):
        wd = zl[:, d * LORA_PAD:(d + 1) * LORA_PAD].astype(F32)
        w_raw = w0_ref[d] + _dot(jnp.tanh(wd).astype(BF16), wl_ref[d])
        lw_out[d] = -_sigmoid(w_raw) * float(np.exp(-0.5))
        ad = zl[:, (2 + d) * LORA_PAD:(3 + d) * LORA_PAD]
        a = _sigmoid(a0_ref[d] + _dot(ad, al_ref[d]))
        kd_out[d] = (k * (1.0 + (a - 1.0) * ka_ref[...])).astype(BF16)
        bb_out[d] = (kkn * a).astype(BF16)


def _rwkv_prepare(z, p, geo):
    rows = z.shape[0]
    t = geo.tp
    base = geo.col0
    hb = t // BF16_SUBLANES
    n_halo = rows // BF16_SUBLANES

    def zcol(off):
        cb = (base + off) // D_RWKV
        return pl.BlockSpec((t, D_RWKV), lambda i: (i, cb))

    def prev(off):
        cb = (base + off) // D_RWKV
        return pl.BlockSpec((BF16_SUBLANES, D_RWKV), lambda i: (jnp.maximum(i * hb - 1, 0), cb))

    def nxt(off):
        cb = (base + off) // D_RWKV
        return pl.BlockSpec((BF16_SUBLANES, D_RWKV),
                            lambda i: (jnp.minimum((i + 1) * hb, n_halo - 1), cb))

    def full(a):
        nd = a.ndim
        return pl.BlockSpec(a.shape, lambda i: (0,) * nd)

    params = (p['conv'], p['w0'], p['w_lora'], p['a0'], p['a_lora'], p['k_k'], p['k_a'], geo.hsum)
    row_spec = pl.BlockSpec((t, D_RWKV), lambda i: (i, 0))
    dir_spec = pl.BlockSpec((2, t, D_RWKV), lambda i: (0, i, 0))
    kern = functools.partial(_prep_kernel, t=t, lat_tiles=geo.r_lat // t, lat_per=geo.s // t,
                             ctx_per=geo.lc // t)
    return pl.pallas_call(
        kern,
        out_shape=(jax.ShapeDtypeStruct((rows, D_RWKV), BF16),
                   jax.ShapeDtypeStruct((rows, D_RWKV), BF16),
                   jax.ShapeDtypeStruct((rows, D_RWKV), BF16),
                   jax.ShapeDtypeStruct((2, rows, D_RWKV), F32),
                   jax.ShapeDtypeStruct((2, rows, D_RWKV), BF16),
                   jax.ShapeDtypeStruct((2, rows, D_RWKV), BF16)),
        grid=(rows // t,),
        in_specs=[zcol(OFF_K), zcol(OFF_V), zcol(OFF_LORA), zcol(OFF_R),
                  prev(OFF_K), prev(OFF_V), prev(OFF_R), nxt(OFF_K), nxt(OFF_V), nxt(OFF_R)]
                 + [full(a) for a in params],
        out_specs=(row_spec, row_spec, row_spec, dir_spec, dir_spec, dir_spec),
        compiler_params=_cparams("parallel"),
        name="rwkv_prepare",
    )(z, z, z, z, z, z, z, z, z, z, *params)


def _wkv_kernel(vf_ref, rf_ref, kf_ref, lwf_ref, kdf_ref, bbf_ref,
                vb_ref, rb_ref, kb_ref, lwb_ref, kdb_ref, bbb_ref, yf_ref, yb_ref, s_ref, *, c):
    @pl.when(pl.program_id(1) == 0)
    def _():
        s_ref[...] = jnp.zeros_like(s_ref)

    c2 = 2 * c
    n_pairs = RWKV_HEADS // 2
    ti = lax.broadcasted_iota(jnp.int32, (c, c), 0)
    ii = lax.broadcasted_iota(jnp.int32, (c, c), 1)
    rt = lax.broadcasted_iota(jnp.int32, (c2, c2), 0)
    ci = lax.broadcasted_iota(jnp.int32, (c2, c2), 1)
    same = (rt // c) == (ci // c)
    tri, strict, incl = [], [], []
    for sign in (1, -1):
        tri.append(jnp.where((ti - ii) * sign >= 0, 1.0, 0.0).astype(BF16))
        before = (rt % c - ci % c) * sign
        strict.append(same & (before > 0))
        incl.append(same & (before >= 0))
    head0 = lax.broadcasted_iota(jnp.int32, (c, PAIR), 1) < RWKV_HEAD_DIM
    eye = (lax.broadcasted_iota(jnp.int32, (PAIR, PAIR), 0)
           == lax.broadcasted_iota(jnp.int32, (PAIR, PAIR), 1)).astype(F32)

    def stack(x):
        return jnp.concatenate([jnp.where(head0, x, 0.0), jnp.where(head0, 0.0, x)], axis=0).astype(BF16)

    refs = ((vf_ref, rf_ref, kf_ref, lwf_ref, kdf_ref, bbf_ref, yf_ref),
            (vb_ref, rb_ref, kb_ref, lwb_ref, kdb_ref, bbb_ref, yb_ref))
    chains = [(d, p) for d in range(2) for p in range(n_pairs)]
    sls = [slice(p * PAIR, (p + 1) * PAIR) for _, p in chains]
    n = len(chains)

    lws = [refs[d][3][:, sl] for (d, _), sl in zip(chains, sls)]
    cums = []
    for (d, _), lw in zip(chains, lws):
        hi = lw.astype(BF16)
        lo = (lw - hi.astype(F32)).astype(BF16)
        cums.append(_dot(tri[d], hi) + _dot(tri[d], lo))
    g_end = [jnp.exp(jnp.sum(lw, axis=0, keepdims=True)) for lw in lws]
    a_s, b_s, k_s, r_s, v_s = [], [], [], [], []
    for (d, _), sl, lw, cum in zip(chains, sls, lws, cums):
        v_ref, r_ref, kk_ref, _, kd_ref, bb_ref, _ = refs[d]
        g_inv = jnp.exp(-cum)
        a_s.append(stack(-kk_ref[:, sl].astype(F32) * jnp.exp(cum - lw)))
        b_s.append(stack(bb_ref[:, sl].astype(F32) * g_inv))
        k_s.append(stack(kd_ref[:, sl].astype(F32) * g_inv))
        r_s.append(stack(r_ref[:, sl].astype(F32) * jnp.exp(cum)))
        v_s.append(stack(v_ref[:, sl].astype(F32)))
    gs = [_dot_nt(jnp.concatenate([a_s[i], r_s[i]], axis=0), jnp.concatenate([b_s[i], k_s[i]], axis=0))
          for i in range(n)]
    l_ab = [jnp.where(strict[d], g[:c2, :c2], 0.0).astype(BF16) for (d, _), g in zip(chains, gs)]
    l_ak = [jnp.where(strict[d], g[:c2, c2:], 0.0).astype(BF16) for (d, _), g in zip(chains, gs)]
    m_rb = [jnp.where(incl[d], g[c2:, :c2], 0.0).astype(BF16) for (d, _), g in zip(chains, gs)]
    m_rk = [jnp.where(incl[d], g[c2:, c2:], 0.0).astype(BF16) for (d, _), g in zip(chains, gs)]

    xs = [jnp.concatenate([a_s[i].astype(F32), _dot(l_ak[i], v_s[i])], axis=1) for i in range(n)]
    lps = l_ab
    n_sq = int(np.log2(c))
    for sq in range(n_sq):
        xs = [x + _dot(lp, x.astype(BF16)) for x, lp in zip(xs, lps)]
        if sq + 1 < n_sq:
            lps = [_dot(lp, lp).astype(BF16) for lp in lps]
    xbs = [x.astype(BF16) for x in xs]
    mxs = [_dot(m_rb[i], xbs[i]) for i in range(n)]
    gqs = [(r_s[i].astype(F32) + mxs[i][:, :PAIR]).astype(BF16) for i in range(n)]
    yqs = [mxs[i][:, PAIR:] + _dot(m_rk[i], v_s[i]) for i in range(n)]
    wbs = [_dot_tn(xbs[i], b_s[i]) for i in range(n)]
    vks = [_dot_tn(v_s[i], k_s[i]) for i in range(n)]
    p_mats = [((eye + wbs[i][:PAIR]) * g_end[i]).astype(BF16) for i in range(n)]
    q_mats = [(wbs[i][PAIR:] + vks[i]) * g_end[i] for i in range(n)]
    s0b = [s_ref[i].astype(BF16) for i in range(n)]
    for i, ((d, _), sl) in enumerate(zip(chains, sls)):
        ys = _dot_nt(gqs[i], s0b[i]) + yqs[i]
        refs[d][6][:, sl] = ys[:c] + ys[c:]
        s_ref[i] = _dot(s0b[i], p_mats[i]) + q_mats[i]


def _wkv_scan(v, r, kkn, lw, kd, bb, geo):
    rows = v.shape[0]
    c = geo.chunk
    n_ctx, n_lat = geo.lc // c, geo.s // c
    lat_blocks = geo.r_lat // c

    def rowblk(d):
        def f(b, s):
            ctx_j = s if d == 0 else n_ctx - 1 - s
            lat_j = s - n_ctx if d == 0 else n_lat - 1 - (s - n_ctx)
            return jnp.where(s < n_ctx, lat_blocks + b * n_ctx + ctx_j, b * n_lat + lat_j)
        return f

    def shared(d):
        f = rowblk(d)
        return pl.BlockSpec((c, D_RWKV), lambda b, s: (f(b, s), 0))

    def perdir(d):
        f = rowblk(d)
        return pl.BlockSpec((None, c, D_RWKV), lambda b, s: (d, f(b, s), 0))

    in_specs = []
    for d in range(2):
        in_specs += [shared(d), shared(d), shared(d), perdir(d), perdir(d), perdir(d)]
    return pl.pallas_call(
        functools.partial(_wkv_kernel, c=c),
        out_shape=(jax.ShapeDtypeStruct((rows, D_RWKV), F32),) * 2,
        grid=(geo.b, n_ctx + n_lat),
        in_specs=in_specs,
        out_specs=(shared(0), shared(1)),
        scratch_shapes=[pltpu.VMEM((RWKV_HEADS, PAIR, PAIR), F32)],
        compiler_params=_cparams("parallel", "arbitrary"),
        name="wkv_scan",
    )(v, r, kkn, lw, kd, bb, v, r, kkn, lw, kd, bb)


def _rwkv_out_kernel(yf_ref, yb_ref, r_ref, v_ref, kd_ref, gd_ref, lg_ref, lb_ref, rk_ref, gl_ref,
                     hsum_ref, o_ref):
    hsum = hsum_ref[...]
    inv_n = 1.0 / RWKV_HEAD_DIM
    y = yf_ref[...] + yb_ref[...]
    mu = _split_dot(y, hsum) * inv_n
    dlt = y - mu
    var = _split_dot(dlt * dlt, hsum) * inv_n
    yn = dlt * lax.rsqrt(var + GN_EPS) * lg_ref[...] + lb_ref[...]
    k_bonus = 0.5 * (kd_ref[0].astype(F32) + kd_ref[1].astype(F32))
    rsum = _split_dot(r_ref[...].astype(F32) * k_bonus * rk_ref[...], hsum)
    bonus = rsum * v_ref[...].astype(F32)
    gate = _dot(_sigmoid(gd_ref[...].astype(F32)).astype(BF16), gl_ref[...])
    o_ref[...] = ((yn + bonus) * gate).astype(BF16)


def _rwkv_output(yf, yb, r, v, kd, z, p, geo):
    rows = r.shape[0]
    t = geo.tp
    gcb = (geo.col0 + OFF_GD) // GATE_LORA
    row_spec = pl.BlockSpec((t, D_RWKV), lambda i: (i, 0))
    dir_spec = pl.BlockSpec((2, t, D_RWKV), lambda i: (0, i, 0))
    vec = pl.BlockSpec((1, D_RWKV), lambda i: (0, 0))
    return pl.pallas_call(
        _rwkv_out_kernel,
        out_shape=jax.ShapeDtypeStruct((rows, D_RWKV), BF16),
        grid=(rows // t,),
        in_specs=[row_spec, row_spec, row_spec, row_spec, dir_spec,
                  pl.BlockSpec((t, GATE_LORA), lambda i: (i, gcb)),
                  vec, vec, vec,
                  pl.BlockSpec((GATE_LORA, D_RWKV), lambda i: (0, 0)),
                  pl.BlockSpec((D_RWKV, D_RWKV), lambda i: (0, 0))],
        out_specs=row_spec,
        compiler_params=_cparams("parallel"),
        name="rwkv_output",
    )(yf, yb, r, v, kd, z, p['lnx_g'], p['lnx_b'], p['r_k'], p['g_lora'], geo.hsum)


def _attn_prep_kernel(q_ref, k_ref, cos_ref, sin_ref, qg_ref, kg_ref, qo_ref, ko_ref):
    cos = cos_ref[...]
    sin = sin_ref[...]
    lane = lax.broadcasted_iota(jnp.int32, cos.shape, 1)
    low_half = (lane % (2 * AXIS_FREQS)) < AXIS_FREQS

    def norm_rope(x, g, scale):
        x = x.astype(F32)
        xn = x * lax.rsqrt(jnp.mean(x * x, axis=-1, keepdims=True) + NORM_EPS) * g
        partner = jnp.where(low_half, pltpu.roll(xn, ATTN_HEAD_DIM - AXIS_FREQS, 1),
                            pltpu.roll(xn, AXIS_FREQS, 1))
        return ((xn * cos + partner * sin) * scale).astype(BF16)

    for h in range(ATTN_HEADS):
        sl = slice(h * ATTN_HEAD_DIM, (h + 1) * ATTN_HEAD_DIM)
        qo_ref[:, sl] = norm_rope(q_ref[:, sl], qg_ref[...], ATTN_SCALE)
    for h in range(ATTN_KV_HEADS):
        sl = slice(h * ATTN_HEAD_DIM, (h + 1) * ATTN_HEAD_DIM)
        ko_ref[:, sl] = norm_rope(k_ref[:, sl], kg_ref[...], 1.0)


def _attn_prep(z, cos_t, sin_t, p, geo):
    rows = z.shape[0]
    t = geo.tp
    qcb = (geo.col0 + OFF_Q) // D_ATTN
    kcb = (geo.col0 + OFF_KA) // D_ATTN_KV
    lat_tiles, lat_per = geo.r_lat // t, geo.s // t

    def tab(i):
        return (jnp.where(i < lat_tiles, i % lat_per, lat_per), 0)

    vec = pl.BlockSpec((1, ATTN_HEAD_DIM), lambda i: (0, 0))
    return pl.pallas_call(
        _attn_prep_kernel,
        out_shape=(jax.ShapeDtypeStruct((rows, D_ATTN), BF16),
                   jax.ShapeDtypeStruct((rows, D_ATTN_KV), BF16)),
        grid=(rows // t,),
        in_specs=[pl.BlockSpec((t, D_ATTN), lambda i: (i, qcb)),
                  pl.BlockSpec((t, D_ATTN_KV), lambda i: (i, kcb)),
                  pl.BlockSpec((t, ATTN_HEAD_DIM), tab),
                  pl.BlockSpec((t, ATTN_HEAD_DIM), tab),
                  vec, vec],
        out_specs=(pl.BlockSpec((t, D_ATTN), lambda i: (i, 0)),
                   pl.BlockSpec((t, D_ATTN_KV), lambda i: (i, 0))),
        compiler_params=_cparams("parallel"),
        name="attn_prep",
    )(z, z, cos_t, sin_t, p['q_norm_g'], p['k_norm_g'])


def _attn_kernel(*refs, n_seg):
    q_ref = refs[0]
    k_refs = refs[1:1 + n_seg]
    v_refs = refs[1 + n_seg:1 + 2 * n_seg]
    o_ref = refs[1 + 2 * n_seg]
    for g in range(GQA_GROUP):
        sl = slice(g * ATTN_HEAD_DIM, (g + 1) * ATTN_HEAD_DIM)
        q = q_ref[:, sl]
        scores = [_dot_nt(q, k_ref[...]) for k_ref in k_refs]
        m = scores[0].max(axis=-1, keepdims=True)
        for s in scores[1:]:
            m = jnp.maximum(m, s.max(axis=-1, keepdims=True))
        denom = None
        acc = None
        for s, v_ref in zip(scores, v_refs):
            e = jnp.exp(s - m)
            es = e.sum(axis=-1, keepdims=True)
            pv = _dot(e.astype(BF16), v_ref[...])
            denom = es if denom is None else denom + es
            acc = pv if acc is None else acc + pv
        o_ref[:, sl] = (acc / denom).astype(BF16)


def _attention(qn, kn, z, geo, latent):
    gw = GQA_GROUP * ATTN_HEAD_DIM
    vcb = (geo.col0 + OFF_VA) // ATTN_HEAD_DIM
    ctx_blk0 = geo.r_lat // geo.lc
    k_ctx = pl.BlockSpec((geo.lc, ATTN_HEAD_DIM), lambda b, h, i: (ctx_blk0 + b, h))
    v_ctx = pl.BlockSpec((geo.lc, ATTN_HEAD_DIM), lambda b, h, i: (ctx_blk0 + b, vcb + h))
    if latent:
        tq = geo.tq
        per = geo.s // tq
        q_spec = pl.BlockSpec((tq, gw), lambda b, h, i: (b * per + i, h))
        k_lat = pl.BlockSpec((geo.s, ATTN_HEAD_DIM), lambda b, h, i: (b, h))
        v_lat = pl.BlockSpec((geo.s, ATTN_HEAD_DIM), lambda b, h, i: (b, vcb + h))
        in_specs = [q_spec, k_ctx, k_lat, v_ctx, v_lat]
        args = (qn, kn, kn, z, z)
        n_seg, out_rows = 2, geo.r_lat
        o_spec = q_spec
    else:
        tq, per = geo.lc, 1
        q_spec = pl.BlockSpec((tq, gw), lambda b, h, i: (ctx_blk0 + b, h))
        in_specs = [q_spec, k_ctx, v_ctx]
        args = (qn, kn, z)
        n_seg, out_rows = 1, geo.r_ctx
        o_spec = pl.BlockSpec((tq, gw), lambda b, h, i: (b, h))
    return pl.pallas_call(
        functools.partial(_attn_kernel, n_seg=n_seg),
        out_shape=jax.ShapeDtypeStruct((out_rows, D_ATTN), BF16),
        grid=(geo.b, ATTN_KV_HEADS, per),
        in_specs=in_specs,
        out_specs=o_spec,
        compiler_params=_cparams("parallel", "parallel", "parallel"),
        name="attention_lat" if latent else "attention_ctx",
    )(*args)


def _mm_kernel(x_ref, w_ref, o_ref):
    o_ref[...] = _dot(x_ref[...], w_ref[...]).astype(o_ref.dtype)


def _fourier_channels(z, wcs, geo):
    rows = z.shape[0]
    t = geo.tp
    fcb = (geo.col0 + OFF_FOUR) // D_FOURIER
    return pl.pallas_call(
        _mm_kernel,
        out_shape=jax.ShapeDtypeStruct((rows, 2 * D_FOURIER), BF16),
        grid=(rows // t,),
        in_specs=[pl.BlockSpec((t, D_FOURIER), lambda i: (i, fcb)),
                  pl.BlockSpec((D_FOURIER, 2 * D_FOURIER), lambda i: (0, 0))],
        out_specs=pl.BlockSpec((t, 2 * D_FOURIER), lambda i: (i, 0)),
        compiler_params=_cparams("parallel"),
        name="fourier_channels",
    )(z, wcs)


def _dft_kernel(cl_ref, sl_ref, xc_ref, xs_ref, o_ref):
    o_ref[...] = (_dot(cl_ref[...], xc_ref[...]) - _dot(sl_ref[...], xs_ref[...])).astype(BF16)


def _fourier_positions(xcs, cl, sl, n, blk0, nb, tmf):
    per = n // tmf
    return pl.pallas_call(
        _dft_kernel,
        out_shape=jax.ShapeDtypeStruct((nb * n, D_FOURIER), BF16),
        grid=(per, nb),
        in_specs=[pl.BlockSpec((tmf, n), lambda i, b: (i, 0)),
                  pl.BlockSpec((tmf, n), lambda i, b: (i, 0)),
                  pl.BlockSpec((n, D_FOURIER), lambda i, b: (blk0 + b, 0)),
                  pl.BlockSpec((n, D_FOURIER), lambda i, b: (blk0 + b, 1))],
        out_specs=pl.BlockSpec((tmf, D_FOURIER), lambda i, b: (b * per + i, 0)),
        compiler_params=_cparams("parallel", "parallel"),
        name="fourier_positions",
    )(cl, sl, xcs, xcs)


def _merge_kernel(rw_ref, fo_ref, at_ref, ga_ref, gf_ref, gc_ref, wr_ref, wf_ref, wa_ref, o_ref):
    y = _sigmoid(ga_ref[...].astype(F32)) * _dot(rw_ref[...], wr_ref[...])
    y += _sigmoid(gf_ref[...].astype(F32)) * _dot(fo_ref[...], wf_ref[...])
    y += _sigmoid(gc_ref[...].astype(F32)) * _dot(at_ref[...], wa_ref[...])
    o_ref[...] = y.astype(BF16)


def _merge(rw, fo, at, z, p, geo, n_rows):
    d = geo.d
    tm = geo.tm
    tn = min(1024, d)
    nj = d // tn
    return pl.pallas_call(
        _merge_kernel,
        out_shape=jax.ShapeDtypeStruct((n_rows, d), BF16),
        grid=(nj, n_rows // tm),
        in_specs=[pl.BlockSpec((tm, D_RWKV), lambda j, i: (i, 0)),
                  pl.BlockSpec((tm, D_FOURIER), lambda j, i: (i, 0)),
                  pl.BlockSpec((tm, D_ATTN), lambda j, i: (i, 0)),
                  pl.BlockSpec((tm, tn), lambda j, i: (i, j)),
                  pl.BlockSpec((tm, tn), lambda j, i: (i, nj + j)),
                  pl.BlockSpec((tm, tn), lambda j, i: (i, 2 * nj + j)),
                  pl.BlockSpec((D_RWKV, tn), lambda j, i: (0, j)),
                  pl.BlockSpec((D_FOURIER, tn), lambda j, i: (0, j)),
                  pl.BlockSpec((D_ATTN, tn), lambda j, i: (0, j))],
        out_specs=pl.BlockSpec((tm, tn), lambda j, i: (i, j)),
        compiler_params=_cparams("parallel", "parallel"),
        name="merge",
    )(rw, fo, at, z, z, z, p['w_br_rwkv'], p['w_br_fourier'], p['w_br_attn'])


def _outres_kernel(y_ref, w_ref, x_ref, gt_ref, o_ref):
    o_ref[...] = x_ref[...] + gt_ref[...] * _dot(y_ref[...], w_ref[...])


def _out_proj_residual(y, w, xs, mod4, geo, n_rows):
    d = geo.d
    tm = geo.tm
    tn = min(1024, d)
    bidx = geo.batch_of_tile(tm)
    return pl.pallas_call(
        _outres_kernel,
        out_shape=jax.ShapeDtypeStruct((n_rows, d), F32),
        grid=(d // tn, n_rows // tm),
        in_specs=[pl.BlockSpec((tm, d), lambda j, i: (i, 0)),
                  pl.BlockSpec((d, tn), lambda j, i: (0, j)),
                  pl.BlockSpec((tm, tn), lambda j, i: (i, j)),
                  pl.BlockSpec((None, None, 1, tn), lambda j, i: (bidx(i), 2, 0, j))],
        out_specs=pl.BlockSpec((tm, tn), lambda j, i: (i, j)),
        compiler_params=_cparams("parallel", "parallel"),
        name="out_proj_residual",
    )(y, w, xs, mod4)


def _expert_ffn(h_lo, h_hi, wgu_ref, bgu_ref, wd_ref, bd_ref, f):
    half = wgu_ref.shape[0] // 2
    gu = _dot(h_lo, wgu_ref[:half, :]) + _dot(h_hi, wgu_ref[half:, :]) + bgu_ref[...]
    gate = jnp.minimum(gu[:, :f], SWIGLU_LIMIT)
    up = jnp.clip(gu[:, f:], -SWIGLU_LIMIT, SWIGLU_LIMIT)
    act = (up + 1.0) * gate * _sigmoid(SWIGLU_ALPHA * gate)
    return _dot(act.astype(BF16), wd_ref[...]) + bd_ref[...]


def _route(sel, cnt, n, t, n_exp):
    p = TOP_K * n
    max_used = p // t + n_exp
    assert max_used % 2 == 0
    n_steps = max_used // 2 + 1
    n_tiles = 2 * n_steps
    unused_key = 2 * n_exp
    counts = cnt[0, :n_exp].astype(jnp.int32)
    ptiles = (counts + t - 1) // t
    n_used = jnp.sum(ptiles)
    pad = ptiles * t - counts
    experts = jnp.arange(n_exp, dtype=jnp.int32)
    last_e = jnp.max(jnp.where(counts > 0, experts, 0))

    e_real = sel[:, :TOP_K].astype(jnp.int32).reshape(p)
    w_real = sel[:, TOP_K:2 * TOP_K].reshape(p)
    pair = jnp.arange(p, dtype=jnp.int32)
    tok_real, k_real = pair // TOP_K, pair % TOP_K
    e_cand = jnp.repeat(experts, t)
    c_cand = jnp.tile(jnp.arange(t, dtype=jnp.int32), n_exp)
    key_cand = jnp.where(c_cand < jnp.repeat(pad, t), 2 * e_cand + 1, unused_key)
    n_extra = n_tiles * t - p - n_exp * t
    zeros_i = jnp.zeros((n_exp * t + n_extra,), jnp.int32)
    keys = jnp.concatenate([2 * e_real, key_cand, jnp.full((n_extra,), unused_key, jnp.int32)])
    dests = jnp.concatenate([k_real * n + tok_real, p + e_cand * t + c_cand, jnp.zeros((n_extra,), jnp.int32)])
    toks = jnp.concatenate([tok_real, zeros_i])
    ws = jnp.concatenate([w_real, jnp.zeros((n_exp * t + n_extra,), F32)])
    key_s, dest_s, tok_s, w_s = lax.sort((keys, dests, toks, ws), num_keys=1, is_stable=True)

    lane = jnp.arange(t, dtype=jnp.int32)[None, :]
    spare = p + n_exp * t + lane
    key_s, dest_s = key_s.reshape(n_tiles, t), dest_s.reshape(n_tiles, t)
    dest = jnp.where(key_s == unused_key, spare, dest_s)
    te = jnp.where(jnp.arange(n_tiles) < n_used, jnp.minimum(key_s[:, 0] // 2, n_exp - 1), last_e)
    dest_prev = jnp.concatenate([spare, dest[1:-1:2]], axis=0)
    return (te.astype(jnp.int32), n_used.reshape(1).astype(jnp.int32), tok_s.reshape(n_tiles, 1, t),
            dest.reshape(n_tiles, 1, t), dest_prev.reshape(n_steps, 1, t), w_s.reshape(n_tiles * t, 1))


def _moe_routed_kernel(te_ref, nu_ref, toka_ref, tokb_ref, tokn_ref, dsta_ref, dstp_ref, wa_ref, wb_ref,
                       h_hbm, wgu_a, bgu_a, wd_a, bd_a, wgu_b, bgu_b, wd_b, bd_b, o_hbm,
                       ga, gb, oa, ob, gsem, ssem, *, f, t, dump0, n_dump):
    j = pl.program_id(0)
    n_used = nu_ref[0]

    def gather(idx_ref, buf, sem):
        def body(r, carry):
            pltpu.make_async_copy(h_hbm.at[pl.ds(idx_ref[0, 0, r], 1)], buf.at[pl.ds(r, 1)], sem).start()
            return carry
        lax.fori_loop(0, t, body, 0, unroll=DMA_UNROLL)

    def scatter(buf, idx_ref, sem):
        def body(r, carry):
            pltpu.make_async_copy(buf.at[pl.ds(r, 1)], o_hbm.at[pl.ds(idx_ref[0, 0, r], 1)], sem).start()
            return carry
        lax.fori_loop(0, t, body, 0, unroll=DMA_UNROLL)

    def wait_rows_in(buf, sem):
        pltpu.make_async_copy(h_hbm.at[pl.ds(0, t)], buf, sem).wait()

    def wait_rows_out(buf, sem):
        pltpu.make_async_copy(buf, o_hbm.at[pl.ds(0, t)], sem).wait()

    def experts(buf, wgu, bgu, wd, bd, w_ref):
        lo, hi = _unpack_halves(buf[...])
        y = _expert_ffn(lo.astype(BF16), hi.astype(BF16), wgu, bgu, wd, bd, f)
        return _pack_halves(y * w_ref[...])

    @pl.when(j == 0)
    def _():
        gather(toka_ref, ga, gsem.at[0])
        ob[...] = jnp.zeros_like(ob)
        fills = [pltpu.make_async_copy(ob, o_hbm.at[pl.ds(dump0 + e * t, t)], ssem.at[1])
                 for e in range(n_dump - 1)]
        for c in fills:
            c.start()
        for c in fills:
            c.wait()
        pltpu.make_async_copy(ob, o_hbm.at[pl.ds(dump0 + (n_dump - 1) * t, t)], ssem.at[0]).start()

    @pl.when(2 * j <= n_used)
    def _():
        wait_rows_in(ga, gsem.at[0])
        scatter(ob, dstp_ref, ssem.at[1])
        gather(tokb_ref, gb, gsem.at[1])
        ya = experts(ga, wgu_a, bgu_a, wd_a, bd_a, wa_ref)
        wait_rows_out(oa, ssem.at[0])
        oa[...] = ya
        wait_rows_in(gb, gsem.at[1])
        scatter(oa, dsta_ref, ssem.at[0])
        gather(tokn_ref, ga, gsem.at[0])
        yb = experts(gb, wgu_b, bgu_b, wd_b, bd_b, wb_ref)
        wait_rows_out(ob, ssem.at[1])
        ob[...] = yb

    @pl.when(j == pl.num_programs(0) - 1)
    def _():
        wait_rows_in(ga, gsem.at[0])
        wait_rows_out(oa, ssem.at[0])


def _moe_routed(hp, sel, cnt, wgu, bgu, wd, bd, geo, n_rows):
    d = geo.d
    n_exp, _, f2 = wgu.shape
    f = f2 // 2
    t = geo.t_moe
    te, n_used, tok, dest, dest_prev, w = _route(sel, cnt, n_rows, t, n_exp)
    n_tiles, n_steps = tok.shape[0], dest_prev.shape[0]
    n_dump = n_exp + 2
    smem = functools.partial(pl.BlockSpec, (1, 1, t), memory_space=pltpu.SMEM)

    def weights(which):
        return [pl.BlockSpec((None, d, f2), lambda j, te, nu: (te[2 * j + which], 0, 0)),
                pl.BlockSpec((None, 1, f2), lambda j, te, nu: (te[2 * j + which], 0, 0)),
                pl.BlockSpec((None, f, d), lambda j, te, nu: (te[2 * j + which], 0, 0)),
                pl.BlockSpec((None, 1, d), lambda j, te, nu: (te[2 * j + which], 0, 0))]

    grid_spec = pltpu.PrefetchScalarGridSpec(
        num_scalar_prefetch=2,
        grid=(n_steps,),
        in_specs=[smem(lambda j, te, nu: (2 * j, 0, 0)),
                  smem(lambda j, te, nu: (2 * j + 1, 0, 0)),
                  smem(lambda j, te, nu: (jnp.minimum(2 * j + 2, n_tiles - 1), 0, 0)),
                  smem(lambda j, te, nu: (2 * j, 0, 0)),
                  smem(lambda j, te, nu: (j, 0, 0)),
                  pl.BlockSpec((t, 1), lambda j, te, nu: (2 * j, 0)),
                  pl.BlockSpec((t, 1), lambda j, te, nu: (2 * j + 1, 0)),
                  pl.BlockSpec(memory_space=pl.ANY)] + weights(0) + weights(1),
        out_specs=pl.BlockSpec(memory_space=pl.ANY),
        scratch_shapes=[pltpu.VMEM((t, d // 2), jnp.uint32)] * 4
                       + [pltpu.SemaphoreType.DMA((2,)), pltpu.SemaphoreType.DMA((2,))])
    return pl.pallas_call(
        functools.partial(_moe_routed_kernel, f=f, t=t, dump0=TOP_K * n_rows, n_dump=n_dump),
        out_shape=jax.ShapeDtypeStruct((TOP_K * n_rows + n_dump * t, d // 2), jnp.uint32),
        grid_spec=grid_spec,
        compiler_params=pltpu.CompilerParams(dimension_semantics=("arbitrary",),
                                             vmem_limit_bytes=VMEM_LIMIT, disable_bounds_checks=True),
        name="moe_routed",
    )(te, n_used, tok, tok, tok, dest, dest_prev, w, w, hp, wgu, bgu, wd, bd, wgu, bgu, wd, bd)


def _combine_kernel(x_ref, gt_ref, y0_ref, y1_ref, y2_ref, y3_ref, o_ref):
    half = x_ref.shape[1] // 2
    parts = [_unpack_halves(y[...]) for y in (y0_ref, y1_ref, y2_ref, y3_ref)]
    lo = (parts[0][0] + parts[1][0]) + (parts[2][0] + parts[3][0])
    hi = (parts[0][1] + parts[1][1]) + (parts[2][1] + parts[3][1])
    o_ref[:, :half] = x_ref[:, :half] + gt_ref[:, :half] * lo
    o_ref[:, half:] = x_ref[:, half:] + gt_ref[:, half:] * hi


def _moe_combine(y4, xs, mod4, geo, n_rows):
    d = geo.d
    tm = geo.t_moe
    per = n_rows // tm
    bidx = geo.batch_of_tile(tm)

    def part(k):
        return pl.BlockSpec((tm, d // 2), lambda i: (k * per + i, 0))

    return pl.pallas_call(
        _combine_kernel,
        out_shape=jax.ShapeDtypeStruct((n_rows, d), F32),
        grid=(per,),
        in_specs=[pl.BlockSpec((tm, d), lambda i: (i, 0)),
                  pl.BlockSpec((None, None, 1, d), lambda i: (bidx(i), 5, 0, 0)),
                  part(0), part(1), part(2), part(3)],
        out_specs=pl.BlockSpec((tm, d), lambda i: (i, 0)),
        compiler_params=_cparams("parallel"),
        name="moe_combine",
    )(xs, mod4, y4, y4, y4, y4)


class _Geometry:
    def __init__(self, b, s, lc, d):
        self.b, self.s, self.lc, self.d = b, s, lc, d
        self.r_lat, self.r_ctx = b * s, b * lc
        self.rows = self.r_lat + self.r_ctx
        self.col0 = 3 * d
        assert self.col0 % 1024 == 0, "gate columns must end on a 1024-column boundary"
        self.tm = min(1024, s, self.r_ctx)
        self.tp = min(256, s, lc)
        self.tq = min(512, s)
        self.chunk = min(WKV_CHUNK, lc, s)
        self.t_moe = min(256, self.tm)
        for t in (self.tm,):
            assert s % t == 0 and self.r_ctx % t == 0
        assert s % self.tp == 0 and lc % self.tp == 0 and self.tp % BF16_SUBLANES == 0
        assert s % self.chunk == 0 and lc % self.chunk == 0 and s % GRID_W == 0
        self.hsum = jnp.asarray(np.kron(np.eye(RWKV_HEADS), np.ones((RWKV_HEAD_DIM,) * 2)), BF16)

    def batch_of_tile(self, tm):
        per, nb = self.s // tm, self.b
        return lambda i: jnp.minimum(i // per, nb)


def _dft_tables(n):
    j = np.arange(n, dtype=np.int64)
    ang = 2.0 * np.pi * ((j[:, None] * j[None, :]) % n).astype(np.float64) / n
    return np.cos(ang) / np.sqrt(n), np.sin(ang) / np.sqrt(n)


def _rope_tables(s, t):
    pos = np.arange(s)
    inv_freq = ROPE_THETA ** (-np.arange(AXIS_FREQS, dtype=np.float32) / AXIS_FREQS)
    ang_r = (pos // GRID_W).astype(np.float32)[:, None] * inv_freq.astype(np.float32)
    ang_c = (pos % GRID_W).astype(np.float32)[:, None] * inv_freq.astype(np.float32)
    cr, sr, cc, sc = np.cos(ang_r), np.sin(ang_r), np.cos(ang_c), np.sin(ang_c)
    cos = np.concatenate([cr, cr, cc, cc], axis=1)
    sin = np.concatenate([-sr, sr, -sc, sc], axis=1)
    cos = np.concatenate([cos, np.ones((t, ATTN_HEAD_DIM))], axis=0)
    sin = np.concatenate([sin, np.zeros((t, ATTN_HEAD_DIM))], axis=0)
    return jnp.asarray(cos, F32), jnp.asarray(sin, F32)


def _permute_w_in(w_in, d):
    o = np.cumsum([0, D_RWKV, D_RWKV, LORA, LORA, LORA, LORA, D_ATTN_KV, D_ATTN_KV, D_RWKV, GATE_LORA,
                   D_ATTN, D_FOURIER, 3 * d])
    seg = lambda i: w_in[..., o[i]:o[i + 1]]
    pad = lambda a: jnp.pad(a, ((0, 0), (0, 0), (0, LORA_PAD - LORA)))
    parts = [seg(12), seg(0), seg(1), pad(seg(2)), pad(seg(3)), pad(seg(4)), pad(seg(5)), seg(8), seg(10),
             seg(11), seg(6), seg(7), seg(9)]
    w = jnp.concatenate(parts, axis=-1)
    n = w.shape[-1]
    n_pad = -(-n // W_IN_TN) * W_IN_TN
    return jnp.pad(w, ((0, 0), (0, 0), (0, n_pad - n))).astype(BF16)


def _split_hi_lo(w):
    hi = w.astype(BF16)
    return hi, (w - hi.astype(F32)).astype(BF16)


def kernel(x, c, ctx, c_ctx, ada_w, ada_b, norm1_g, norm2_g, w_in, rwkv_conv, w0, w_lora, a0, a_lora, g_lora, k_k, k_a, r_k, lnx_g, lnx_b, q_norm_g, k_norm_g, w_br_rwkv, w_br_fourier, w_br_attn, w_out, router_w, router_b, exp_w_gu, exp_b_gu, exp_w_down, exp_b_down):
    b, s, d = x.shape
    lc = ctx.shape[1]
    depth = w_in.shape[0]
    n_exp = router_w.shape[-1]
    geo = _Geometry(b, s, lc, d)

    w_in_p = _permute_w_in(w_in, d)
    lora_pad = ((0, 0), (0, 0), (0, LORA_PAD - LORA), (0, 0))
    w_lora_p = jnp.pad(w_lora, lora_pad).astype(BF16)
    a_lora_p = jnp.pad(a_lora, lora_pad).astype(BF16)
    g_lora_b = g_lora.astype(BF16)
    wbr_r, wbr_f, wbr_a, w_out_b = (w.astype(BF16) for w in (w_br_rwkv, w_br_fourier, w_br_attn, w_out))
    rw_p = jnp.pad(router_w, ((0, 0), (0, 0), (0, ROUTER_PAD - n_exp)))
    rw_hi, rw_lo = _split_hi_lo(rw_p)
    rb_p = jnp.pad(router_b, ((0, 0), (0, ROUTER_PAD - n_exp)), constant_values=NEG_BIG)
    wgu_b, wd_b = exp_w_gu.astype(BF16), exp_w_down.astype(BF16)
    cos_t, sin_t = _rope_tables(s, geo.tp)
    cc, sc = _dft_tables(FOURIER_GROUP_DIM)
    groups = D_FOURIER // FOURIER_GROUP_DIM
    wcs = jnp.asarray(np.concatenate([np.kron(np.eye(groups), cc), np.kron(np.eye(groups), sc)], axis=1), BF16)
    cl_lat, sl_lat = (jnp.asarray(m, BF16) for m in _dft_tables(s))
    cl_ctx, sl_ctx = (jnp.asarray(m, BF16) for m in _dft_tables(lc))

    mod_rows = -(-(b + 1) // BF16_SUBLANES) * BF16_SUBLANES
    c_all = jnp.concatenate([c, c_ctx[None], jnp.zeros((mod_rows - b - 1, d), F32)], axis=0)
    mod = _adaln(c_all, ada_w, ada_b).reshape(depth, mod_rows, 6, 1, d)

    xs = jnp.concatenate([x.reshape(b * s, d), ctx.reshape(b * lc, d)], axis=0)
    for l in range(depth):
        last = l == depth - 1
        mod4 = mod[l]
        p = dict(conv=rwkv_conv[l], w0=w0[l][:, None, :], w_lora=w_lora_p[l], a0=a0[l][:, None, :],
                 a_lora=a_lora_p[l], k_k=k_k[l][None], k_a=k_a[l][None], r_k=r_k[l][None],
                 lnx_g=lnx_g[l][None], lnx_b=lnx_b[l][None], g_lora=g_lora_b[l],
                 q_norm_g=q_norm_g[l][None], k_norm_g=k_norm_g[l][None],
                 w_br_rwkv=wbr_r[l], w_br_fourier=wbr_f[l], w_br_attn=wbr_a[l])
        n_rows = geo.r_lat if last else geo.rows

        z = _norm_in_proj(xs, norm1_g[l], mod4, w_in_p[l], geo)

        v_c, r_c, kkn, lw, kd, bb = _rwkv_prepare(z, p, geo)
        yf, yb = _wkv_scan(v_c, r_c, kkn, lw, kd, bb, geo)
        rw = _rwkv_output(yf, yb, r_c, v_c, kd, z, p, geo)

        qn, kn = _attn_prep(z, cos_t, sin_t, p, geo)
        att = _attention(qn, kn, z, geo, latent=True)

        xcs = _fourier_channels(z, wcs, geo)
        fo = _fourier_positions(xcs, cl_lat, sl_lat, s, 0, b, min(1024, s))
        if not last:
            att = jnp.concatenate([att, _attention(qn, kn, z, geo, latent=False)], axis=0)
            fo_c = _fourier_positions(xcs, cl_ctx, sl_ctx, lc, geo.r_lat // lc, b, lc)
            fo = jnp.concatenate([fo, fo_c], axis=0)

        ym = _merge(rw, fo, att, z, p, geo, n_rows)
        xs = _out_proj_residual(ym, w_out_b[l], xs, mod4, geo, n_rows)

        hp, sel, cnt = _norm_router(xs, norm2_g[l], mod4, rw_hi[l], rw_lo[l], rb_p[l][None], geo, n_rows)
        y4 = _moe_routed(hp, sel, cnt, wgu_b[l], exp_b_gu[l][:, None, :], wd_b[l], exp_b_down[l][:, None, :],
                         geo, n_rows)
        xs = _moe_combine(y4, xs, mod4, geo, n_rows)
    return xs[:geo.r_lat].reshape(b, s, d)
```

```python
import functools

import numpy as np
import jax
import jax.numpy as jnp
from jax import lax
from jax.experimental import pallas as pl
from jax.experimental.pallas import tpu as pltpu

F32 = jnp.float32
BF16 = jnp.bfloat16

NORM_EPS = 1e-6
GN_EPS = 64e-5
RWKV_HEADS = 8
RWKV_HEAD_DIM = 64
D_RWKV = RWKV_HEADS * RWKV_HEAD_DIM
LORA = 96
GATE_LORA = 256
D_FOURIER = 512
FOURIER_GROUP_DIM = 128
ATTN_HEADS = 8
ATTN_KV_HEADS = 2
ATTN_HEAD_DIM = 128
GQA_GROUP = ATTN_HEADS // ATTN_KV_HEADS
D_ATTN = ATTN_HEADS * ATTN_HEAD_DIM
D_ATTN_KV = ATTN_KV_HEADS * ATTN_HEAD_DIM
ATTN_SCALE = ATTN_HEAD_DIM ** -0.5
GRID_W = 64
ROPE_THETA = 10000.0
AXIS_FREQS = ATTN_HEAD_DIM // 4
TOP_K = 4
SWIGLU_LIMIT = 7.0
SWIGLU_ALPHA = 1.702

LANES = 128
BF16_SUBLANES = 16
VMEM_LIMIT = 56 * 1024 * 1024

LORA_PAD = LANES
W_IN_TN = 1536
WKV_CHUNK = 64
PAIR = 2 * RWKV_HEAD_DIM
ROUTER_PAD = LANES
NEG_BIG = -1e30

OFF_K, OFF_V, OFF_LORA, OFF_R, OFF_Q, OFF_FOUR, OFF_KA, OFF_VA, OFF_GD, OFF_END = (
    0, 512, 1024, 1536, 2048, 3072, 3584, 3840, 4096, 4352)


def _cparams(*sem):
    return pltpu.CompilerParams(dimension_semantics=sem, vmem_limit_bytes=VMEM_LIMIT)


def _dot(a, b):
    return jnp.dot(a, b, preferred_element_type=F32)


def _dot_nt(a, b):
    return lax.dot_general(a, b, (((1,), (1,)), ((), ())), preferred_element_type=F32)


def _dot_tn(a, b):
    return lax.dot_general(a, b, (((0,), (0,)), ((), ())), preferred_element_type=F32)


def _split_dot(x, g):
    hi = x.astype(BF16)
    lo = (x - hi.astype(F32)).astype(BF16)
    return _dot(hi, g) + _dot(lo, g)


def _sigmoid(x):
    return 1.0 / (1.0 + jnp.exp(-x))


def _adaln_kernel(c_ref, w_ref, b_ref, o_ref):
    c = c_ref[...]
    s = (c * _sigmoid(c)).astype(BF16)
    o_ref[...] = _dot(s, w_ref[...].astype(BF16)) + b_ref[...]


def _adaln(c_all, ada_w, ada_b):
    depth, d, n = ada_w.shape
    rows = c_all.shape[0]
    tn = 1024 if n % 1024 == 0 else n
    return pl.pallas_call(
        _adaln_kernel,
        out_shape=jax.ShapeDtypeStruct((depth, rows, n), F32),
        grid=(depth, n // tn),
        in_specs=[
            pl.BlockSpec((rows, d), lambda l, j: (0, 0)),
            pl.BlockSpec((None, d, tn), lambda l, j: (l, 0, j)),
            pl.BlockSpec((None, 1, tn), lambda l, j: (l, 0, j)),
        ],
        out_specs=pl.BlockSpec((None, rows, tn), lambda l, j: (l, 0, j)),
        compiler_params=_cparams("parallel", "parallel"),
        name="adaln",
    )(c_all, ada_w, ada_b.reshape(depth, 1, n))


def _modulated_norm(x, g, sc, sh):
    ms = jnp.mean(x * x, axis=-1, keepdims=True)
    return x * lax.rsqrt(ms + NORM_EPS) * g * (1.0 + sc) + sh


def _normmm_kernel(x_ref, g_ref, sc_ref, sh_ref, w_ref, o_ref, h_ref):
    @pl.when(pl.program_id(1) == 0)
    def _():
        h_ref[...] = _modulated_norm(x_ref[...], g_ref[...], sc_ref[...], sh_ref[...]).astype(BF16)

    o_ref[...] = _dot(h_ref[...], w_ref[...]).astype(o_ref.dtype)


def _norm_in_proj(xs, gain, mod4, w, geo):
    rows, d = xs.shape
    n = w.shape[1]
    tm, tn = geo.tm, W_IN_TN
    bidx = geo.batch_of_tile(tm)
    return pl.pallas_call(
        _normmm_kernel,
        out_shape=jax.ShapeDtypeStruct((rows, n), BF16),
        grid=(rows // tm, n // tn),
        in_specs=[
            pl.BlockSpec((tm, d), lambda i, j: (i, 0)),
            pl.BlockSpec((1, d), lambda i, j: (0, 0)),
            pl.BlockSpec((None, None, 1, d), lambda i, j: (bidx(i), 1, 0, 0)),
            pl.BlockSpec((None, None, 1, d), lambda i, j: (bidx(i), 0, 0, 0)),
            pl.BlockSpec((d, tn), lambda i, j: (0, j)),
        ],
        out_specs=pl.BlockSpec((tm, tn), lambda i, j: (i, j)),
        scratch_shapes=[pltpu.VMEM((tm, d), BF16)],
        compiler_params=_cparams("parallel", "arbitrary"),
        name="norm_in_proj",
    )(xs, gain.reshape(1, d), mod4, mod4, w)


def _pack_halves(x):
    half = x.shape[1] // 2
    xb = x.astype(BF16).astype(F32)
    lo = lax.shift_right_logical(lax.bitcast_convert_type(xb[:, :half], jnp.uint32), jnp.uint32(16))
    hi = lax.bitcast_convert_type(xb[:, half:], jnp.uint32) & jnp.uint32(0xFFFF0000)
    return lo | hi


def _unpack_halves(words):
    lo = lax.bitcast_convert_type(lax.shift_left(words, jnp.uint32(16)), F32)
    hi = lax.bitcast_convert_type(words & jnp.uint32(0xFFFF0000), F32)
    return lo, hi


def _norm_router_kernel(x_ref, g_ref, sc_ref, sh_ref, wh_ref, wl_ref, rb_ref, h_ref, sel_ref, cnt_ref):
    @pl.when(pl.program_id(0) == 0)
    def _():
        cnt_ref[...] = jnp.zeros_like(cnt_ref)

    h = _modulated_norm(x_ref[...], g_ref[...], sc_ref[...], sh_ref[...])
    h_ref[...] = _pack_halves(h)
    hi = h.astype(BF16)
    lo = (h - hi.astype(F32)).astype(BF16)
    logits = _dot(hi, wh_ref[...]) + _dot(lo, wh_ref[...]) + _dot(hi, wl_ref[...]) + rb_ref[...]
    lane = lax.broadcasted_iota(jnp.int32, logits.shape, 1).astype(F32)
    work = logits
    vals, firsts = [], []
    for _ in range(TOP_K):
        m = jnp.max(work, axis=-1, keepdims=True)
        first = jnp.min(jnp.where(work == m, lane, float(ROUTER_PAD)), axis=-1, keepdims=True)
        vals.append(m)
        firsts.append(first)
        work = jnp.where(lane == first, 2.0 * NEG_BIG, work)
    exps = [jnp.exp(v - vals[0]) for v in vals]
    denom = exps[0] + exps[1] + exps[2] + exps[3]
    table = jnp.zeros_like(logits)
    picked = jnp.zeros_like(logits)
    for k in range(TOP_K):
        table = jnp.where(lane == float(k), firsts[k], table)
        table = jnp.where(lane == float(TOP_K + k), exps[k] / denom, table)
        picked = picked + jnp.where(lane == firsts[k], 1.0, 0.0)
    sel_ref[...] = table
    cnt_ref[...] += jnp.sum(picked, axis=0, keepdims=True)


def _norm_router(xs, gain, mod4, rw_hi, rw_lo, rb, geo, n_rows):
    d = xs.shape[1]
    tm = geo.tm
    bidx = geo.batch_of_tile(tm)
    return pl.pallas_call(
        _norm_router_kernel,
        out_shape=(jax.ShapeDtypeStruct((n_rows, d // 2), jnp.uint32),
                   jax.ShapeDtypeStruct((n_rows, ROUTER_PAD), F32),
                   jax.ShapeDtypeStruct((1, ROUTER_PAD), F32)),
        grid=(n_rows // tm,),
        in_specs=[
            pl.BlockSpec((tm, d), lambda i: (i, 0)),
            pl.BlockSpec((1, d), lambda i: (0, 0)),
            pl.BlockSpec((None, None, 1, d), lambda i: (bidx(i), 4, 0, 0)),
            pl.BlockSpec((None, None, 1, d), lambda i: (bidx(i), 3, 0, 0)),
            pl.BlockSpec((d, ROUTER_PAD), lambda i: (0, 0)),
            pl.BlockSpec((d, ROUTER_PAD), lambda i: (0, 0)),
            pl.BlockSpec((1, ROUTER_PAD), lambda i: (0, 0)),
        ],
        out_specs=(pl.BlockSpec((tm, d // 2), lambda i: (i, 0)),
                   pl.BlockSpec((tm, ROUTER_PAD), lambda i: (i, 0)),
                   pl.BlockSpec((1, ROUTER_PAD), lambda i: (0, 0))),
        compiler_params=_cparams("arbitrary"),
        name="norm_router",
    )(xs, gain.reshape(1, d), mod4, mod4, rw_hi, rw_lo, rb)


def _prep_kernel(zk_ref, zv_ref, zl_ref, zr_ref, pk_ref, pv_ref, pr_ref, nk_ref, nv_ref, nr_ref,
                 conv_ref, w0_ref, wl_ref, a0_ref, al_ref, kk_ref, ka_ref, hsum_ref,
                 v_out, r_out, kkn_out, lw_out, kd_out, bb_out, *, t, lat_tiles, lat_per, ctx_per):
    i = pl.program_id(0)
    is_lat = i < lat_tiles
    per = jnp.where(is_lat, lat_per, ctx_per)
    j = jnp.where(is_lat, i, i - lat_tiles) % per
    first = j == 0
    last = j == per - 1
    row = lax.broadcasted_iota(jnp.int32, (t, 1), 0)

    def conv(z_ref, p_ref, n_ref, which):
        z = z_ref[...].astype(F32)
        prev_row = p_ref[...].astype(F32)[BF16_SUBLANES - 1:BF16_SUBLANES, :]
        next_row = n_ref[...].astype(F32)[0:1, :]
        prev_row = jnp.where(first, 0.0, prev_row)
        next_row = jnp.where(last, 0.0, next_row)
        zm = jnp.where(row == 0, prev_row, pltpu.roll(z, 1, 0))
        zp = jnp.where(row == t - 1, next_row, pltpu.roll(z, t - 1, 0))
        w = conv_ref[which]
        return zm * w[0:1] + z * w[1:2] + zp * w[2:3]

    k = conv(zk_ref, pk_ref, nk_ref, 0)
    v = conv(zv_ref, pv_ref, nv_ref, 1)
    r = conv(zr_ref, pr_ref, nr_ref, 2)
    v_out[...] = v.astype(BF16)
    r_out[...] = r.astype(BF16)
    kkv = k * kk_ref[...]
    ss = _split_dot(kkv * kkv, hsum_ref[...])
    kkn = kkv * lax.rsqrt(jnp.maximum(ss, 1e-24))
    kkn_out[...] = kkn.astype(BF16)
    zl = zl_ref[...]
    for d in range(2):
        wd = zl[:, d * LORA_PAD:(d + 1) * LORA_PAD].astype(F32)
        w_raw = w0_ref[d] + _dot(jnp.tanh(wd).astype(BF16), wl_ref[d])
        lw_out[d] = -_sigmoid(w_raw) * float(np.exp(-0.5))
        ad = zl[:, (2 + d) * LORA_PAD:(3 + d) * LORA_PAD]
        a = _sigmoid(a0_ref[d] + _dot(ad, al_ref[d]))
        kd_out[d] = (k * (1.0 + (a - 1.0) * ka_ref[...])).astype(BF16)
        bb_out[d] = (kkn * a).astype(BF16)


def _rwkv_prepare(z, p, geo):
    rows = z.shape[0]
    t = geo.tp
    base = geo.col0
    hb = t // BF16_SUBLANES
    n_halo = rows // BF16_SUBLANES

    def zcol(off):
        cb = (base + off) // D_RWKV
        return pl.BlockSpec((t, D_RWKV), lambda i: (i, cb))

    def prev(off):
        cb = (base + off) // D_RWKV
        return pl.BlockSpec((BF16_SUBLANES, D_RWKV), lambda i: (jnp.maximum(i * hb - 1, 0), cb))

    def nxt(off):
        cb = (base + off) // D_RWKV
        return pl.BlockSpec((BF16_SUBLANES, D_RWKV),
                            lambda i: (jnp.minimum((i + 1) * hb, n_halo - 1), cb))

    def full(a):
        nd = a.ndim
        return pl.BlockSpec(a.shape, lambda i: (0,) * nd)

    params = (p['conv'], p['w0'], p['w_lora'], p['a0'], p['a_lora'], p['k_k'], p['k_a'], geo.hsum)
    row_spec = pl.BlockSpec((t, D_RWKV), lambda i: (i, 0))
    dir_spec = pl.BlockSpec((2, t, D_RWKV), lambda i: (0, i, 0))
    kern = functools.partial(_prep_kernel, t=t, lat_tiles=geo.r_lat // t, lat_per=geo.s // t,
                             ctx_per=geo.lc // t)
    return pl.pallas_call(
        kern,
        out_shape=(jax.ShapeDtypeStruct((rows, D_RWKV), BF16),
                   jax.ShapeDtypeStruct((rows, D_RWKV), BF16),
                   jax.ShapeDtypeStruct((rows, D_RWKV), BF16),
                   jax.ShapeDtypeStruct((2, rows, D_RWKV), F32),
                   jax.ShapeDtypeStruct((2, rows, D_RWKV), BF16),
                   jax.ShapeDtypeStruct((2, rows, D_RWKV), BF16)),
        grid=(rows // t,),
        in_specs=[zcol(OFF_K), zcol(OFF_V), zcol(OFF_LORA), zcol(OFF_R),
                  prev(OFF_K), prev(OFF_V), prev(OFF_R), nxt(OFF_K), nxt(OFF_V), nxt(OFF_R)]
                 + [full(a) for a in params],
        out_specs=(row_spec, row_spec, row_spec, dir_spec, dir_spec, dir_spec),
        compiler_params=_cparams("parallel"),
        name="rwkv_prepare",
    )(z, z, z, z, z, z, z, z, z, z, *params)


def _wkv_kernel(vf_ref, rf_ref, kf_ref, lwf_ref, kdf_ref, bbf_ref,
                vb_ref, rb_ref, kb_ref, lwb_ref, kdb_ref, bbb_ref, yf_ref, yb_ref, s_ref, *, c):
    @pl.when(pl.program_id(1) == 0)
    def _():
        s_ref[...] = jnp.zeros_like(s_ref)

    c2 = 2 * c
    n_pairs = RWKV_HEADS // 2
    ti = lax.broadcasted_iota(jnp.int32, (c, c), 0)
    ii = lax.broadcasted_iota(jnp.int32, (c, c), 1)
    rt = lax.broadcasted_iota(jnp.int32, (c2, c2), 0)
    ci = lax.broadcasted_iota(jnp.int32, (c2, c2), 1)
    same = (rt // c) == (ci // c)
    tri, strict, incl = [], [], []
    for sign in (1, -1):
        tri.append(jnp.where((ti - ii) * sign >= 0, 1.0, 0.0).astype(BF16))
        before = (rt % c - ci % c) * sign
        strict.append(same & (before > 0))
        incl.append(same & (before >= 0))
    head0 = lax.broadcasted_iota(jnp.int32, (c, PAIR), 1) < RWKV_HEAD_DIM
    eye = (lax.broadcasted_iota(jnp.int32, (PAIR, PAIR), 0)
           == lax.broadcasted_iota(jnp.int32, (PAIR, PAIR), 1)).astype(F32)

    def stack(x):
        return jnp.concatenate([jnp.where(head0, x, 0.0), jnp.where(head0, 0.0, x)], axis=0).astype(BF16)

    refs = ((vf_ref, rf_ref, kf_ref, lwf_ref, kdf_ref, bbf_ref, yf_ref),
            (vb_ref, rb_ref, kb_ref, lwb_ref, kdb_ref, bbb_ref, yb_ref))
    chains = [(d, p) for d in range(2) for p in range(n_pairs)]
    sls = [slice(p * PAIR, (p + 1) * PAIR) for _, p in chains]
    n = len(chains)

    lws = [refs[d][3][:, sl] for (d, _), sl in zip(chains, sls)]
    cums = []
    for (d, _), lw in zip(chains, lws):
        hi = lw.astype(BF16)
        lo = (lw - hi.astype(F32)).astype(BF16)
        cums.append(_dot(tri[d], hi) + _dot(tri[d], lo))
    g_end = [jnp.exp(jnp.sum(lw, axis=0, keepdims=True)) for lw in lws]
    a_s, b_s, k_s, r_s, v_s = [], [], [], [], []
    for (d, _), sl, lw, cum in zip(chains, sls, lws, cums):
        v_ref, r_ref, kk_ref, _, kd_ref, bb_ref, _ = refs[d]
        g_inv = jnp.exp(-cum)
        a_s.append(stack(-kk_ref[:, sl].astype(F32) * jnp.exp(cum - lw)))
        b_s.append(stack(bb_ref[:, sl].astype(F32) * g_inv))
        k_s.append(stack(kd_ref[:, sl].astype(F32) * g_inv))
        r_s.append(stack(r_ref[:, sl].astype(F32) * jnp.exp(cum)))
        v_s.append(stack(v_ref[:, sl].astype(F32)))
    gs = [_dot_nt(jnp.concatenate([a_s[i], r_s[i]], axis=0), jnp.concatenate([b_s[i], k_s[i]], axis=0))
          for i in range(n)]
    l_ab = [jnp.where(strict[d], g[:c2, :c2], 0.0).astype(BF16) for (d, _), g in zip(chains, gs)]
    l_ak = [jnp.where(strict[d], g[:c2, c2:], 0.0).astype(BF16) for (d, _), g in zip(chains, gs)]
    m_rb = [jnp.where(incl[d], g[c2:, :c2], 0.0).astype(BF16) for (d, _), g in zip(chains, gs)]
    m_rk = [jnp.where(incl[d], g[c2:, c2:], 0.0).astype(BF16) for (d, _), g in zip(chains, gs)]

    eye2 = (rt == ci).astype(F32)
    ts = [eye2 + l.astype(F32) for l in l_ab]
    lps = l_ab
    for _ in range(int(np.log2(c)) - 1):
        lps = [_dot(lp, lp).astype(BF16) for lp in lps]
        ts = [tm + _dot(lp, tm.astype(BF16)) for tm, lp in zip(ts, lps)]
    xbs = [_dot(ts[i].astype(BF16),
                jnp.concatenate([a_s[i], _dot(l_ak[i], v_s[i]).astype(BF16)], axis=1)).astype(BF16)
           for i in range(n)]
    mxs = [_dot(m_rb[i], xbs[i]) for i in range(n)]
    gqs = [(r_s[i].astype(F32) + mxs[i][:, :PAIR]).astype(BF16) for i in range(n)]
    yqs = [mxs[i][:, PAIR:] + _dot(m_rk[i], v_s[i]) for i in range(n)]
    wbs = [_dot_tn(xbs[i], b_s[i]) for i in range(n)]
    vks = [_dot_tn(v_s[i], k_s[i]) for i in range(n)]
    p_mats = [((eye + wbs[i][:PAIR]) * g_end[i]).astype(BF16) for i in range(n)]
    q_mats = [(wbs[i][PAIR:] + vks[i]) * g_end[i] for i in range(n)]
    s0b = [s_ref[i].astype(BF16) for i in range(n)]
    for i, ((d, _), sl) in enumerate(zip(chains, sls)):
        ys = _dot_nt(gqs[i], s0b[i]) + yqs[i]
        refs[d][6][:, sl] = ys[:c] + ys[c:]
        s_ref[i] = _dot(s0b[i], p_mats[i]) + q_mats[i]


def _wkv_scan(v, r, kkn, lw, kd, bb, geo):
    rows = v.shape[0]
    c = geo.chunk
    n_ctx, n_lat = geo.lc // c, geo.s // c
    lat_blocks = geo.r_lat // c

    def rowblk(d):
        def f(b, s):
            ctx_j = s if d == 0 else n_ctx - 1 - s
            lat_j = s - n_ctx if d == 0 else n_lat - 1 - (s - n_ctx)
            return jnp.where(s < n_ctx, lat_blocks + b * n_ctx + ctx_j, b * n_lat + lat_j)
        return f

    def shared(d):
        f = rowblk(d)
        return pl.BlockSpec((c, D_RWKV), lambda b, s: (f(b, s), 0))

    def perdir(d):
        f = rowblk(d)
        return pl.BlockSpec((None, c, D_RWKV), lambda b, s: (d, f(b, s), 0))

    in_specs = []
    for d in range(2):
        in_specs += [shared(d), shared(d), shared(d), perdir(d), perdir(d), perdir(d)]
    return pl.pallas_call(
        functools.partial(_wkv_kernel, c=c),
        out_shape=(jax.ShapeDtypeStruct((rows, D_RWKV), F32),) * 2,
        grid=(geo.b, n_ctx + n_lat),
        in_specs=in_specs,
        out_specs=(shared(0), shared(1)),
        scratch_shapes=[pltpu.VMEM((RWKV_HEADS, PAIR, PAIR), F32)],
        compiler_params=_cparams("parallel", "arbitrary"),
        name="wkv_scan",
    )(v, r, kkn, lw, kd, bb, v, r, kkn, lw, kd, bb)


def _rwkv_out_kernel(yf_ref, yb_ref, r_ref, v_ref, kd_ref, gd_ref, lg_ref, lb_ref, rk_ref, gl_ref,
                     hsum_ref, o_ref):
    hsum = hsum_ref[...]
    inv_n = 1.0 / RWKV_HEAD_DIM
    y = yf_ref[...] + yb_ref[...]
    mu = _split_dot(y, hsum) * inv_n
    dlt = y - mu
    var = _split_dot(dlt * dlt, hsum) * inv_n
    yn = dlt * lax.rsqrt(var + GN_EPS) * lg_ref[...] + lb_ref[...]
    k_bonus = 0.5 * (kd_ref[0].astype(F32) + kd_ref[1].astype(F32))
    rsum = _split_dot(r_ref[...].astype(F32) * k_bonus * rk_ref[...], hsum)
    bonus = rsum * v_ref[...].astype(F32)
    gate = _dot(_sigmoid(gd_ref[...].astype(F32)).astype(BF16), gl_ref[...])
    o_ref[...] = ((yn + bonus) * gate).astype(BF16)


def _rwkv_output(yf, yb, r, v, kd, z, p, geo):
    rows = r.shape[0]
    t = geo.tp
    gcb = (geo.col0 + OFF_GD) // GATE_LORA
    row_spec = pl.BlockSpec((t, D_RWKV), lambda i: (i, 0))
    dir_spec = pl.BlockSpec((2, t, D_RWKV), lambda i: (0, i, 0))
    vec = pl.BlockSpec((1, D_RWKV), lambda i: (0, 0))
    return pl.pallas_call(
        _rwkv_out_kernel,
        out_shape=jax.ShapeDtypeStruct((rows, D_RWKV), BF16),
        grid=(rows // t,),
        in_specs=[row_spec, row_spec, row_spec, row_spec, dir_spec,
                  pl.BlockSpec((t, GATE_LORA), lambda i: (i, gcb)),
                  vec, vec, vec,
                  pl.BlockSpec((GATE_LORA, D_RWKV), lambda i: (0, 0)),
                  pl.BlockSpec((D_RWKV, D_RWKV), lambda i: (0, 0))],
        out_specs=row_spec,
        compiler_params=_cparams("parallel"),
        name="rwkv_output",
    )(yf, yb, r, v, kd, z, p['lnx_g'], p['lnx_b'], p['r_k'], p['g_lora'], geo.hsum)


def _attn_prep_kernel(q_ref, k_ref, cos_ref, sin_ref, qg_ref, kg_ref, qo_ref, ko_ref):
    cos = cos_ref[...]
    sin = sin_ref[...]
    lane = lax.broadcasted_iota(jnp.int32, cos.shape, 1)
    low_half = (lane % (2 * AXIS_FREQS)) < AXIS_FREQS

    def norm_rope(x, g, scale):
        x = x.astype(F32)
        xn = x * lax.rsqrt(jnp.mean(x * x, axis=-1, keepdims=True) + NORM_EPS) * g
        partner = jnp.where(low_half, pltpu.roll(xn, ATTN_HEAD_DIM - AXIS_FREQS, 1),
                            pltpu.roll(xn, AXIS_FREQS, 1))
        return ((xn * cos + partner * sin) * scale).astype(BF16)

    for h in range(ATTN_HEADS):
        sl = slice(h * ATTN_HEAD_DIM, (h + 1) * ATTN_HEAD_DIM)
        qo_ref[:, sl] = norm_rope(q_ref[:, sl], qg_ref[...], ATTN_SCALE)
    for h in range(ATTN_KV_HEADS):
        sl = slice(h * ATTN_HEAD_DIM, (h + 1) * ATTN_HEAD_DIM)
        ko_ref[:, sl] = norm_rope(k_ref[:, sl], kg_ref[...], 1.0)


def _attn_prep(z, cos_t, sin_t, p, geo):
    rows = z.shape[0]
    t = geo.tp
    qcb = (geo.col0 + OFF_Q) // D_ATTN
    kcb = (geo.col0 + OFF_KA) // D_ATTN_KV
    lat_tiles, lat_per = geo.r_lat // t, geo.s // t

    def tab(i):
        return (jnp.where(i < lat_tiles, i % lat_per, lat_per), 0)

    vec = pl.BlockSpec((1, ATTN_HEAD_DIM), lambda i: (0, 0))
    return pl.pallas_call(
        _attn_prep_kernel,
        out_shape=(jax.ShapeDtypeStruct((rows, D_ATTN), BF16),
                   jax.ShapeDtypeStruct((rows, D_ATTN_KV), BF16)),
        grid=(rows // t,),
        in_specs=[pl.BlockSpec((t, D_ATTN), lambda i: (i, qcb)),
                  pl.BlockSpec((t, D_ATTN_KV), lambda i: (i, kcb)),
                  pl.BlockSpec((t, ATTN_HEAD_DIM), tab),
                  pl.BlockSpec((t, ATTN_HEAD_DIM), tab),
                  vec, vec],
        out_specs=(pl.BlockSpec((t, D_ATTN), lambda i: (i, 0)),
                   pl.BlockSpec((t, D_ATTN_KV), lambda i: (i, 0))),
        compiler_params=_cparams("parallel"),
        name="attn_prep",
    )(z, z, cos_t, sin_t, p['q_norm_g'], p['k_norm_g'])


def _attn_kernel(*refs, n_seg):
    q_ref = refs[0]
    k_refs = refs[1:1 + n_seg]
    v_refs = refs[1 + n_seg:1 + 2 * n_seg]
    o_ref = refs[1 + 2 * n_seg]
    for g in range(GQA_GROUP):
        sl = slice(g * ATTN_HEAD_DIM, (g + 1) * ATTN_HEAD_DIM)
        q = q_ref[:, sl]
        scores = [_dot_nt(q, k_ref[...]) for k_ref in k_refs]
        m = scores[0].max(axis=-1, keepdims=True)
        for s in scores[1:]:
            m = jnp.maximum(m, s.max(axis=-1, keepdims=True))
        denom = None
        acc = None
        for s, v_ref in zip(scores, v_refs):
            e = jnp.exp(s - m)
            es = e.sum(axis=-1, keepdims=True)
            pv = _dot(e.astype(BF16), v_ref[...])
            denom = es if denom is None else denom + es
            acc = pv if acc is None else acc + pv
        o_ref[:, sl] = (acc / denom).astype(BF16)


def _attention(qn, kn, z, geo, latent):
    gw = GQA_GROUP * ATTN_HEAD_DIM
    vcb = (geo.col0 + OFF_VA) // ATTN_HEAD_DIM
    ctx_blk0 = geo.r_lat // geo.lc
    k_ctx = pl.BlockSpec((geo.lc, ATTN_HEAD_DIM), lambda b, h, i: (ctx_blk0 + b, h))
    v_ctx = pl.BlockSpec((geo.lc, ATTN_HEAD_DIM), lambda b, h, i: (ctx_blk0 + b, vcb + h))
    if latent:
        tq = geo.tq
        per = geo.s // tq
        q_spec = pl.BlockSpec((tq, gw), lambda b, h, i: (b * per + i, h))
        k_lat = pl.BlockSpec((geo.s, ATTN_HEAD_DIM), lambda b, h, i: (b, h))
        v_lat = pl.BlockSpec((geo.s, ATTN_HEAD_DIM), lambda b, h, i: (b, vcb + h))
        in_specs = [q_spec, k_ctx, k_lat, v_ctx, v_lat]
        args = (qn, kn, kn, z, z)
        n_seg, out_rows = 2, geo.r_lat
        o_spec = q_spec
    else:
        tq, per = geo.lc, 1
        q_spec = pl.BlockSpec((tq, gw), lambda b, h, i: (ctx_blk0 + b, h))
        in_specs = [q_spec, k_ctx, v_ctx]
        args = (qn, kn, z)
        n_seg, out_rows = 1, geo.r_ctx
        o_spec = pl.BlockSpec((tq, gw), lambda b, h, i: (b, h))
    return pl.pallas_call(
        functools.partial(_attn_kernel, n_seg=n_seg),
        out_shape=jax.ShapeDtypeStruct((out_rows, D_ATTN), BF16),
        grid=(geo.b, ATTN_KV_HEADS, per),
        in_specs=in_specs,
        out_specs=o_spec,
        compiler_params=_cparams("parallel", "parallel", "parallel"),
        name="attention_lat" if latent else "attention_ctx",
    )(*args)


def _mm_kernel(x_ref, w_ref, o_ref):
    o_ref[...] = _dot(x_ref[...], w_ref[...]).astype(o_ref.dtype)


def _fourier_channels(z, wcs, geo):
    rows = z.shape[0]
    t = geo.tp
    fcb = (geo.col0 + OFF_FOUR) // D_FOURIER
    return pl.pallas_call(
        _mm_kernel,
        out_shape=jax.ShapeDtypeStruct((rows, 2 * D_FOURIER), BF16),
        grid=(rows // t,),
        in_specs=[pl.BlockSpec((t, D_FOURIER), lambda i: (i, fcb)),
                  pl.BlockSpec((D_FOURIER, 2 * D_FOURIER), lambda i: (0, 0))],
        out_specs=pl.BlockSpec((t, 2 * D_FOURIER), lambda i: (i, 0)),
        compiler_params=_cparams("parallel"),
        name="fourier_channels",
    )(z, wcs)


def _dft_kernel(cl_ref, sl_ref, xc_ref, xs_ref, o_ref):
    o_ref[...] = (_dot(cl_ref[...], xc_ref[...]) - _dot(sl_ref[...], xs_ref[...])).astype(BF16)


def _fourier_positions(xcs, cl, sl, n, blk0, nb, tmf):
    per = n // tmf
    return pl.pallas_call(
        _dft_kernel,
        out_shape=jax.ShapeDtypeStruct((nb * n, D_FOURIER), BF16),
        grid=(per, nb),
        in_specs=[pl.BlockSpec((tmf, n), lambda i, b: (i, 0)),
                  pl.BlockSpec((tmf, n), lambda i, b: (i, 0)),
                  pl.BlockSpec((n, D_FOURIER), lambda i, b: (blk0 + b, 0)),
                  pl.BlockSpec((n, D_FOURIER), lambda i, b: (blk0 + b, 1))],
        out_specs=pl.BlockSpec((tmf, D_FOURIER), lambda i, b: (b * per + i, 0)),
        compiler_params=_cparams("parallel", "parallel"),
        name="fourier_positions",
    )(cl, sl, xcs, xcs)


def _merge_kernel(rw_ref, fo_ref, at_ref, ga_ref, gf_ref, gc_ref, wr_ref, wf_ref, wa_ref, o_ref):
    y = _sigmoid(ga_ref[...].astype(F32)) * _dot(rw_ref[...], wr_ref[...])
    y += _sigmoid(gf_ref[...].astype(F32)) * _dot(fo_ref[...], wf_ref[...])
    y += _sigmoid(gc_ref[...].astype(F32)) * _dot(at_ref[...], wa_ref[...])
    o_ref[...] = y.astype(BF16)


def _merge(rw, fo, at, z, p, geo, n_rows):
    d = geo.d
    tm = geo.tm
    tn = min(1024, d)
    nj = d // tn
    return pl.pallas_call(
        _merge_kernel,
        out_shape=jax.ShapeDtypeStruct((n_rows, d), BF16),
        grid=(nj, n_rows // tm),
        in_specs=[pl.BlockSpec((tm, D_RWKV), lambda j, i: (i, 0)),
                  pl.BlockSpec((tm, D_FOURIER), lambda j, i: (i, 0)),
                  pl.BlockSpec((tm, D_ATTN), lambda j, i: (i, 0)),
                  pl.BlockSpec((tm, tn), lambda j, i: (i, j)),
                  pl.BlockSpec((tm, tn), lambda j, i: (i, nj + j)),
                  pl.BlockSpec((tm, tn), lambda j, i: (i, 2 * nj + j)),
                  pl.BlockSpec((D_RWKV, tn), lambda j, i: (0, j)),
                  pl.BlockSpec((D_FOURIER, tn), lambda j, i: (0, j)),
                  pl.BlockSpec((D_ATTN, tn), lambda j, i: (0, j))],
        out_specs=pl.BlockSpec((tm, tn), lambda j, i: (i, j)),
        compiler_params=_cparams("parallel", "parallel"),
        name="merge",
    )(rw, fo, at, z, z, z, p['w_br_rwkv'], p['w_br_fourier'], p['w_br_attn'])


def _outres_kernel(y_ref, w_ref, x_ref, gt_ref, o_ref):
    o_ref[...] = x_ref[...] + gt_ref[...] * _dot(y_ref[...], w_ref[...])


def _out_proj_residual(y, w, xs, mod4, geo, n_rows):
    d = geo.d
    tm = geo.tm
    tn = min(1024, d)
    bidx = geo.batch_of_tile(tm)
    return pl.pallas_call(
        _outres_kernel,
        out_shape=jax.ShapeDtypeStruct((n_rows, d), F32),
        grid=(d // tn, n_rows // tm),
        in_specs=[pl.BlockSpec((tm, d), lambda j, i: (i, 0)),
                  pl.BlockSpec((d, tn), lambda j, i: (0, j)),
                  pl.BlockSpec((tm, tn), lambda j, i: (i, j)),
                  pl.BlockSpec((None, None, 1, tn), lambda j, i: (bidx(i), 2, 0, j))],
        out_specs=pl.BlockSpec((tm, tn), lambda j, i: (i, j)),
        compiler_params=_cparams("parallel", "parallel"),
        name="out_proj_residual",
    )(y, w, xs, mod4)


def _expert_ffn(h_lo, h_hi, wgu_ref, bgu_ref, wd_ref, bd_ref, f):
    half = wgu_ref.shape[0] // 2
    gu = _dot(h_lo, wgu_ref[:half, :]) + _dot(h_hi, wgu_ref[half:, :]) + bgu_ref[...]
    gate = jnp.minimum(gu[:, :f], SWIGLU_LIMIT)
    up = jnp.clip(gu[:, f:], -SWIGLU_LIMIT, SWIGLU_LIMIT)
    act = (up + 1.0) * gate * _sigmoid(SWIGLU_ALPHA * gate)
    return _dot(act.astype(BF16), wd_ref[...]) + bd_ref[...]


def _route(sel, cnt, n, t, n_exp):
    p = TOP_K * n
    max_used = p // t + n_exp
    assert max_used % 2 == 0
    n_steps = max_used // 2 + 1
    n_tiles = 2 * n_steps
    unused_key = 2 * n_exp
    counts = cnt[0, :n_exp].astype(jnp.int32)
    ptiles = (counts + t - 1) // t
    n_used = jnp.sum(ptiles)
    pad = ptiles * t - counts
    experts = jnp.arange(n_exp, dtype=jnp.int32)
    last_e = jnp.max(jnp.where(counts > 0, experts, 0))

    e_real = sel[:, :TOP_K].astype(jnp.int32).reshape(p)
    w_real = sel[:, TOP_K:2 * TOP_K].reshape(p)
    pair = jnp.arange(p, dtype=jnp.int32)
    tok_real, k_real = pair // TOP_K, pair % TOP_K
    e_cand = jnp.repeat(experts, t)
    c_cand = jnp.tile(jnp.arange(t, dtype=jnp.int32), n_exp)
    key_cand = jnp.where(c_cand < jnp.repeat(pad, t), 2 * e_cand + 1, unused_key)
    n_extra = n_tiles * t - p - n_exp * t
    zeros_i = jnp.zeros((n_exp * t + n_extra,), jnp.int32)
    keys = jnp.concatenate([2 * e_real, key_cand, jnp.full((n_extra,), unused_key, jnp.int32)])
    dests = jnp.concatenate([k_real * n + tok_real, p + e_cand * t + c_cand, jnp.zeros((n_extra,), jnp.int32)])
    toks = jnp.concatenate([tok_real, zeros_i])
    ws = jnp.concatenate([w_real, jnp.zeros((n_exp * t + n_extra,), F32)])
    key_s, dest_s, tok_s, w_s = lax.sort((keys, dests, toks, ws), num_keys=1, is_stable=True)

    lane = jnp.arange(t, dtype=jnp.int32)[None, :]
    spare = p + n_exp * t + lane
    key_s, dest_s = key_s.reshape(n_tiles, t), dest_s.reshape(n_tiles, t)
    dest = jnp.where(key_s == unused_key, spare, dest_s)
    te = jnp.where(jnp.arange(n_tiles) < n_used, jnp.minimum(key_s[:, 0] // 2, n_exp - 1), last_e)
    dest_prev = jnp.concatenate([spare, dest[1:-1:2]], axis=0)
    return (te.astype(jnp.int32), n_used.reshape(1).astype(jnp.int32), tok_s.reshape(n_tiles, 1, t),
            dest.reshape(n_tiles, 1, t), dest_prev.reshape(n_steps, 1, t), w_s.reshape(n_tiles * t, 1))


def _moe_routed_kernel(te_ref, nu_ref, toka_ref, tokb_ref, tokn_ref, dsta_ref, dstp_ref, wa_ref, wb_ref,
                       h_hbm, wgu_a, bgu_a, wd_a, bd_a, wgu_b, bgu_b, wd_b, bd_b, o_hbm,
                       ga, gb, oa, ob, gsem, ssem, *, f, t, dump0, n_dump):
    j = pl.program_id(0)
    n_used = nu_ref[0]

    def gather(idx_ref, buf, sem):
        @pl.when(n_used > 0)
        def _():
            for r in range(t):
                pltpu.make_async_copy(h_hbm.at[pl.ds(idx_ref[0, 0, r], 1)], buf.at[pl.ds(r, 1)], sem).start()

    def scatter(buf, idx_ref, sem):
        @pl.when(n_used > 0)
        def _():
            for r in range(t):
                pltpu.make_async_copy(buf.at[pl.ds(r, 1)], o_hbm.at[pl.ds(idx_ref[0, 0, r], 1)], sem).start()

    def wait_rows_in(buf, sem):
        pltpu.make_async_copy(h_hbm.at[pl.ds(0, t)], buf, sem).wait()

    def wait_rows_out(buf, sem):
        pltpu.make_async_copy(buf, o_hbm.at[pl.ds(0, t)], sem).wait()

    def experts(buf, wgu, bgu, wd, bd, w_ref):
        lo, hi = _unpack_halves(buf[...])
        y = _expert_ffn(lo.astype(BF16), hi.astype(BF16), wgu, bgu, wd, bd, f)
        return _pack_halves(y * w_ref[...])

    @pl.when(j == 0)
    def _():
        gather(toka_ref, ga, gsem.at[0])
        ob[...] = jnp.zeros_like(ob)
        fills = [pltpu.make_async_copy(ob, o_hbm.at[pl.ds(dump0 + e * t, t)], ssem.at[1])
                 for e in range(n_dump - 1)]
        for c in fills:
            c.start()
        for c in fills:
            c.wait()
        pltpu.make_async_copy(ob, o_hbm.at[pl.ds(dump0 + (n_dump - 1) * t, t)], ssem.at[0]).start()

    @pl.when(2 * j <= n_used)
    def _():
        wait_rows_in(ga, gsem.at[0])
        scatter(ob, dstp_ref, ssem.at[1])
        gather(tokb_ref, gb, gsem.at[1])
        ya = experts(ga, wgu_a, bgu_a, wd_a, bd_a, wa_ref)
        wait_rows_out(oa, ssem.at[0])
        oa[...] = ya
        wait_rows_in(gb, gsem.at[1])
        scatter(oa, dsta_ref, ssem.at[0])
        gather(tokn_ref, ga, gsem.at[0])
        yb = experts(gb, wgu_b, bgu_b, wd_b, bd_b, wb_ref)
        wait_rows_out(ob, ssem.at[1])
        ob[...] = yb

    @pl.when(j == pl.num_programs(0) - 1)
    def _():
        wait_rows_in(ga, gsem.at[0])
        wait_rows_out(oa, ssem.at[0])


def _moe_routed(hp, sel, cnt, wgu, bgu, wd, bd, geo, n_rows):
    d = geo.d
    n_exp, _, f2 = wgu.shape
    f = f2 // 2
    t = geo.t_moe
    te, n_used, tok, dest, dest_prev, w = _route(sel, cnt, n_rows, t, n_exp)
    n_tiles, n_steps = tok.shape[0], dest_prev.shape[0]
    n_dump = n_exp + 2
    smem = functools.partial(pl.BlockSpec, (1, 1, t), memory_space=pltpu.SMEM)

    def weights(which):
        return [pl.BlockSpec((None, d, f2), lambda j, te, nu: (te[2 * j + which], 0, 0)),
                pl.BlockSpec((None, 1, f2), lambda j, te, nu: (te[2 * j + which], 0, 0)),
                pl.BlockSpec((None, f, d), lambda j, te, nu: (te[2 * j + which], 0, 0)),
                pl.BlockSpec((None, 1, d), lambda j, te, nu: (te[2 * j + which], 0, 0))]

    grid_spec = pltpu.PrefetchScalarGridSpec(
        num_scalar_prefetch=2,
        grid=(n_steps,),
        in_specs=[smem(lambda j, te, nu: (2 * j, 0, 0)),
                  smem(lambda j, te, nu: (2 * j + 1, 0, 0)),
                  smem(lambda j, te, nu: (jnp.minimum(2 * j + 2, n_tiles - 1), 0, 0)),
                  smem(lambda j, te, nu: (2 * j, 0, 0)),
                  smem(lambda j, te, nu: (j, 0, 0)),
                  pl.BlockSpec((t, 1), lambda j, te, nu: (2 * j, 0)),
                  pl.BlockSpec((t, 1), lambda j, te, nu: (2 * j + 1, 0)),
                  pl.BlockSpec(memory_space=pl.ANY)] + weights(0) + weights(1),
        out_specs=pl.BlockSpec(memory_space=pl.ANY),
        scratch_shapes=[pltpu.VMEM((t, d // 2), jnp.uint32)] * 4
                       + [pltpu.SemaphoreType.DMA((2,)), pltpu.SemaphoreType.DMA((2,))])
    return pl.pallas_call(
        functools.partial(_moe_routed_kernel, f=f, t=t, dump0=TOP_K * n_rows, n_dump=n_dump),
        out_shape=jax.ShapeDtypeStruct((TOP_K * n_rows + n_dump * t, d // 2), jnp.uint32),
        grid_spec=grid_spec,
        compiler_params=pltpu.CompilerParams(dimension_semantics=("arbitrary",),
                                             vmem_limit_bytes=VMEM_LIMIT, disable_bounds_checks=True),
        name="moe_routed",
    )(te, n_used, tok, tok, tok, dest, dest_prev, w, w, hp, wgu, bgu, wd, bd, wgu, bgu, wd, bd)


def _combine_kernel(x_ref, gt_ref, y0_ref, y1_ref, y2_ref, y3_ref, o_ref):
    half = x_ref.shape[1] // 2
    parts = [_unpack_halves(y[...]) for y in (y0_ref, y1_ref, y2_ref, y3_ref)]
    lo = (parts[0][0] + parts[1][0]) + (parts[2][0] + parts[3][0])
    hi = (parts[0][1] + parts[1][1]) + (parts[2][1] + parts[3][1])
    o_ref[:, :half] = x_ref[:, :half] + gt_ref[:, :half] * lo
    o_ref[:, half:] = x_ref[:, half:] + gt_ref[:, half:] * hi


def _moe_combine(y4, xs, mod4, geo, n_rows):
    d = geo.d
    tm = geo.t_moe
    per = n_rows // tm
    bidx = geo.batch_of_tile(tm)

    def part(k):
        return pl.BlockSpec((tm, d // 2), lambda i: (k * per + i, 0))

    return pl.pallas_call(
        _combine_kernel,
        out_shape=jax.ShapeDtypeStruct((n_rows, d), F32),
        grid=(per,),
        in_specs=[pl.BlockSpec((tm, d), lambda i: (i, 0)),
                  pl.BlockSpec((None, None, 1, d), lambda i: (bidx(i), 5, 0, 0)),
                  part(0), part(1), part(2), part(3)],
        out_specs=pl.BlockSpec((tm, d), lambda i: (i, 0)),
        compiler_params=_cparams("parallel"),
        name="moe_combine",
    )(xs, mod4, y4, y4, y4, y4)


class _Geometry:
    def __init__(self, b, s, lc, d):
        self.b, self.s, self.lc, self.d = b, s, lc, d
        self.r_lat, self.r_ctx = b * s, b * lc
        self.rows = self.r_lat + self.r_ctx
        self.col0 = 3 * d
        assert self.col0 % 1024 == 0, "gate columns must end on a 1024-column boundary"
        self.tm = min(1024, s, self.r_ctx)
        self.tp = min(256, s, lc)
        self.tq = min(512, s)
        self.chunk = min(WKV_CHUNK, lc, s)
        self.t_moe = min(256, self.tm)
        for t in (self.tm,):
            assert s % t == 0 and self.r_ctx % t == 0
        assert s % self.tp == 0 and lc % self.tp == 0 and self.tp % BF16_SUBLANES == 0
        assert s % self.chunk == 0 and lc % self.chunk == 0 and s % GRID_W == 0
        self.hsum = jnp.asarray(np.kron(np.eye(RWKV_HEADS), np.ones((RWKV_HEAD_DIM,) * 2)), BF16)

    def batch_of_tile(self, tm):
        per, nb = self.s // tm, self.b
        return lambda i: jnp.minimum(i // per, nb)


def _dft_tables(n):
    j = np.arange(n, dtype=np.int64)
    ang = 2.0 * np.pi * ((j[:, None] * j[None, :]) % n).astype(np.float64) / n
    return np.cos(ang) / np.sqrt(n), np.sin(ang) / np.sqrt(n)


def _rope_tables(s, t):
    pos = np.arange(s)
    inv_freq = ROPE_THETA ** (-np.arange(AXIS_FREQS, dtype=np.float32) / AXIS_FREQS)
    ang_r = (pos // GRID_W).astype(np.float32)[:, None] * inv_freq.astype(np.float32)
    ang_c = (pos % GRID_W).astype(np.float32)[:, None] * inv_freq.astype(np.float32)
    cr, sr, cc, sc = np.cos(ang_r), np.sin(ang_r), np.cos(ang_c), np.sin(ang_c)
    cos = np.concatenate([cr, cr, cc, cc], axis=1)
    sin = np.concatenate([-sr, sr, -sc, sc], axis=1)
    cos = np.concatenate([cos, np.ones((t, ATTN_HEAD_DIM))], axis=0)
    sin = np.concatenate([sin, np.zeros((t, ATTN_HEAD_DIM))], axis=0)
    return jnp.asarray(cos, F32), jnp.asarray(sin, F32)


def _permute_w_in(w_in, d):
    o = np.cumsum([0, D_RWKV, D_RWKV, LORA, LORA, LORA, LORA, D_ATTN_KV, D_ATTN_KV, D_RWKV, GATE_LORA,
                   D_ATTN, D_FOURIER, 3 * d])
    seg = lambda i: w_in[..., o[i]:o[i + 1]]
    pad = lambda a: jnp.pad(a, ((0, 0), (0, 0), (0, LORA_PAD - LORA)))
    parts = [seg(12), seg(0), seg(1), pad(seg(2)), pad(seg(3)), pad(seg(4)), pad(seg(5)), seg(8), seg(10),
             seg(11), seg(6), seg(7), seg(9)]
    w = jnp.concatenate(parts, axis=-1)
    n = w.shape[-1]
    n_pad = -(-n // W_IN_TN) * W_IN_TN
    return jnp.pad(w, ((0, 0), (0, 0), (0, n_pad - n))).astype(BF16)


def _split_hi_lo(w):
    hi = w.astype(BF16)
    return hi, (w - hi.astype(F32)).astype(BF16)


def kernel(x, c, ctx, c_ctx, ada_w, ada_b, norm1_g, norm2_g, w_in, rwkv_conv, w0, w_lora, a0, a_lora, g_lora, k_k, k_a, r_k, lnx_g, lnx_b, q_norm_g, k_norm_g, w_br_rwkv, w_br_fourier, w_br_attn, w_out, router_w, router_b, exp_w_gu, exp_b_gu, exp_w_down, exp_b_down):
    b, s, d = x.shape
    lc = ctx.shape[1]
    depth = w_in.shape[0]
    n_exp = router_w.shape[-1]
    geo = _Geometry(b, s, lc, d)

    w_in_p = _permute_w_in(w_in, d)
    lora_pad = ((0, 0), (0, 0), (0, LORA_PAD - LORA), (0, 0))
    w_lora_p = jnp.pad(w_lora, lora_pad).astype(BF16)
    a_lora_p = jnp.pad(a_lora, lora_pad).astype(BF16)
    g_lora_b = g_lora.astype(BF16)
    wbr_r, wbr_f, wbr_a, w_out_b = (w.astype(BF16) for w in (w_br_rwkv, w_br_fourier, w_br_attn, w_out))
    rw_p = jnp.pad(router_w, ((0, 0), (0, 0), (0, ROUTER_PAD - n_exp)))
    rw_hi, rw_lo = _split_hi_lo(rw_p)
    rb_p = jnp.pad(router_b, ((0, 0), (0, ROUTER_PAD - n_exp)), constant_values=NEG_BIG)
    wgu_b, wd_b = exp_w_gu.astype(BF16), exp_w_down.astype(BF16)
    cos_t, sin_t = _rope_tables(s, geo.tp)
    cc, sc = _dft_tables(FOURIER_GROUP_DIM)
    groups = D_FOURIER // FOURIER_GROUP_DIM
    wcs = jnp.asarray(np.concatenate([np.kron(np.eye(groups), cc), np.kron(np.eye(groups), sc)], axis=1), BF16)
    cl_lat, sl_lat = (jnp.asarray(m, BF16) for m in _dft_tables(s))
    cl_ctx, sl_ctx = (jnp.asarray(m, BF16) for m in _dft_tables(lc))

    mod_rows = -(-(b + 1) // BF16_SUBLANES) * BF16_SUBLANES
    c_all = jnp.concatenate([c, c_ctx[None], jnp.zeros((mod_rows - b - 1, d), F32)], axis=0)
    mod = _adaln(c_all, ada_w, ada_b).reshape(depth, mod_rows, 6, 1, d)

    xs = jnp.concatenate([x.reshape(b * s, d), ctx.reshape(b * lc, d)], axis=0)
    for l in range(depth):
        last = l == depth - 1
        mod4 = mod[l]
        p = dict(conv=rwkv_conv[l], w0=w0[l][:, None, :], w_lora=w_lora_p[l], a0=a0[l][:, None, :],
                 a_lora=a_lora_p[l], k_k=k_k[l][None], k_a=k_a[l][None], r_k=r_k[l][None],
                 lnx_g=lnx_g[l][None], lnx_b=lnx_b[l][None], g_lora=g_lora_b[l],
                 q_norm_g=q_norm_g[l][None], k_norm_g=k_norm_g[l][None],
                 w_br_rwkv=wbr_r[l], w_br_fourier=wbr_f[l], w_br_attn=wbr_a[l])
        n_rows = geo.r_lat if last else geo.rows

        z = _norm_in_proj(xs, norm1_g[l], mod4, w_in_p[l], geo)

        v_c, r_c, kkn, lw, kd, bb = _rwkv_prepare(z, p, geo)
        yf, yb = _wkv_scan(v_c, r_c, kkn, lw, kd, bb, geo)
        rw = _rwkv_output(yf, yb, r_c, v_c, kd, z, p, geo)

        qn, kn = _attn_prep(z, cos_t, sin_t, p, geo)
        att = _attention(qn, kn, z, geo, latent=True)

        xcs = _fourier_channels(z, wcs, geo)
        fo = _fourier_positions(xcs, cl_lat, sl_lat, s, 0, b, min(1024, s))
        if not last:
            att = jnp.concatenate([att, _attention(qn, kn, z, geo, latent=False)], axis=0)
            fo_c = _fourier_positions(xcs, cl_ctx, sl_ctx, lc, geo.r_lat // lc, b, lc)
            fo = jnp.concatenate([fo, fo_c], axis=0)

        ym = _merge(rw, fo, att, z, p, geo, n_rows)
        xs = _out_proj_residual(ym, w_out_b[l], xs, mod4, geo, n_rows)

        hp, sel, cnt = _norm_router(xs, norm2_g[l], mod4, rw_hi[l], rw_lo[l], rb_p[l][None], geo, n_rows)
        y4 = _moe_routed(hp, sel, cnt, wgu_b[l], exp_b_gu[l][:, None, :], wd_b[l], exp_b_down[l][:, None, :],
                         geo, n_rows)
        xs = _moe_combine(y4, xs, mod4, geo, n_rows)
    return xs[:geo.r_lat].reshape(b, s, d)
```

```python
import functools

import numpy as np
import jax
import jax.numpy as jnp
from jax import lax
from jax.experimental import pallas as pl
from jax.experimental.pallas import tpu as pltpu

F32 = jnp.float32
BF16 = jnp.bfloat16

NORM_EPS = 1e-6
GN_EPS = 64e-5
RWKV_HEADS = 8
RWKV_HEAD_DIM = 64
D_RWKV = RWKV_HEADS * RWKV_HEAD_DIM
LORA = 96
GATE_LORA = 256
D_FOURIER = 512
FOURIER_GROUP_DIM = 128
ATTN_HEADS = 8
ATTN_KV_HEADS = 2
ATTN_HEAD_DIM = 128
GQA_GROUP = ATTN_HEADS // ATTN_KV_HEADS
D_ATTN = ATTN_HEADS * ATTN_HEAD_DIM
D_ATTN_KV = ATTN_KV_HEADS * ATTN_HEAD_DIM
ATTN_SCALE = ATTN_HEAD_DIM ** -0.5
GRID_W = 64
ROPE_THETA = 10000.0
AXIS_FREQS = ATTN_HEAD_DIM // 4
TOP_K = 4
SWIGLU_LIMIT = 7.0
SWIGLU_ALPHA = 1.702

LANES = 128
BF16_SUBLANES = 16
VMEM_LIMIT = 56 * 1024 * 1024

LORA_PAD = LANES
W_IN_TN = 1536
WKV_CHUNK = 64
PAIR = 2 * RWKV_HEAD_DIM
ROUTER_PAD = LANES
NEG_BIG = -1e30

OFF_K, OFF_V, OFF_LORA, OFF_R, OFF_Q, OFF_FOUR, OFF_KA, OFF_VA, OFF_GD, OFF_END = (
    0, 512, 1024, 1536, 2048, 3072, 3584, 3840, 4096, 4352)


def _cparams(*sem):
    return pltpu.CompilerParams(dimension_semantics=sem, vmem_limit_bytes=VMEM_LIMIT)


def _dot(a, b):
    return jnp.dot(a, b, preferred_element_type=F32)


def _dot_nt(a, b):
    return lax.dot_general(a, b, (((1,), (1,)), ((), ())), preferred_element_type=F32)


def _dot_tn(a, b):
    return lax.dot_general(a, b, (((0,), (0,)), ((), ())), preferred_element_type=F32)


def _split_dot(x, g):
    hi = x.astype(BF16)
    lo = (x - hi.astype(F32)).astype(BF16)
    return _dot(hi, g) + _dot(lo, g)


def _sigmoid(x):
    return 1.0 / (1.0 + jnp.exp(-x))


def _adaln_kernel(c_ref, w_ref, b_ref, o_ref):
    c = c_ref[...]
    s = (c * _sigmoid(c)).astype(BF16)
    o_ref[...] = _dot(s, w_ref[...].astype(BF16)) + b_ref[...]


def _adaln(c_all, ada_w, ada_b):
    depth, d, n = ada_w.shape
    rows = c_all.shape[0]
    tn = 1024 if n % 1024 == 0 else n
    return pl.pallas_call(
        _adaln_kernel,
        out_shape=jax.ShapeDtypeStruct((depth, rows, n), F32),
        grid=(depth, n // tn),
        in_specs=[
            pl.BlockSpec((rows, d), lambda l, j: (0, 0)),
            pl.BlockSpec((None, d, tn), lambda l, j: (l, 0, j)),
            pl.BlockSpec((None, 1, tn), lambda l, j: (l, 0, j)),
        ],
        out_specs=pl.BlockSpec((None, rows, tn), lambda l, j: (l, 0, j)),
        compiler_params=_cparams("parallel", "parallel"),
        name="adaln",
    )(c_all, ada_w, ada_b.reshape(depth, 1, n))


def _modulated_norm(x, g, sc, sh):
    ms = jnp.mean(x * x, axis=-1, keepdims=True)
    return x * lax.rsqrt(ms + NORM_EPS) * g * (1.0 + sc) + sh


def _normmm_kernel(x_ref, g_ref, sc_ref, sh_ref, w_ref, o_ref, h_ref):
    @pl.when(pl.program_id(1) == 0)
    def _():
        h_ref[...] = _modulated_norm(x_ref[...], g_ref[...], sc_ref[...], sh_ref[...]).astype(BF16)

    o_ref[...] = _dot(h_ref[...], w_ref[...]).astype(o_ref.dtype)


def _norm_in_proj(xs, gain, mod4, w, geo):
    rows, d = xs.shape
    n = w.shape[1]
    tm, tn = geo.tm, W_IN_TN
    bidx = geo.batch_of_tile(tm)
    return pl.pallas_call(
        _normmm_kernel,
        out_shape=jax.ShapeDtypeStruct((rows, n), BF16),
        grid=(rows // tm, n // tn),
        in_specs=[
            pl.BlockSpec((tm, d), lambda i, j: (i, 0)),
            pl.BlockSpec((1, d), lambda i, j: (0, 0)),
            pl.BlockSpec((None, None, 1, d), lambda i, j: (bidx(i), 1, 0, 0)),
            pl.BlockSpec((None, None, 1, d), lambda i, j: (bidx(i), 0, 0, 0)),
            pl.BlockSpec((d, tn), lambda i, j: (0, j)),
        ],
        out_specs=pl.BlockSpec((tm, tn), lambda i, j: (i, j)),
        scratch_shapes=[pltpu.VMEM((tm, d), BF16)],
        compiler_params=_cparams("parallel", "arbitrary"),
        name="norm_in_proj",
    )(xs, gain.reshape(1, d), mod4, mod4, w)


def _pack_halves(x):
    half = x.shape[1] // 2
    xb = x.astype(BF16).astype(F32)
    lo = lax.shift_right_logical(lax.bitcast_convert_type(xb[:, :half], jnp.uint32), jnp.uint32(16))
    hi = lax.bitcast_convert_type(xb[:, half:], jnp.uint32) & jnp.uint32(0xFFFF0000)
    return lo | hi


def _unpack_halves(words):
    lo = lax.bitcast_convert_type(lax.shift_left(words, jnp.uint32(16)), F32)
    hi = lax.bitcast_convert_type(words & jnp.uint32(0xFFFF0000), F32)
    return lo, hi


def _norm_router_kernel(x_ref, g_ref, sc_ref, sh_ref, wh_ref, wl_ref, rb_ref, h_ref, sel_ref, cnt_ref):
    @pl.when(pl.program_id(0) == 0)
    def _():
        cnt_ref[...] = jnp.zeros_like(cnt_ref)

    h = _modulated_norm(x_ref[...], g_ref[...], sc_ref[...], sh_ref[...])
    h_ref[...] = _pack_halves(h)
    hi = h.astype(BF16)
    lo = (h - hi.astype(F32)).astype(BF16)
    logits = _dot(hi, wh_ref[...]) + _dot(lo, wh_ref[...]) + _dot(hi, wl_ref[...]) + rb_ref[...]
    lane = lax.broadcasted_iota(jnp.int32, logits.shape, 1).astype(F32)
    work = logits
    vals, firsts = [], []
    for _ in range(TOP_K):
        m = jnp.max(work, axis=-1, keepdims=True)
        first = jnp.min(jnp.where(work == m, lane, float(ROUTER_PAD)), axis=-1, keepdims=True)
        vals.append(m)
        firsts.append(first)
        work = jnp.where(lane == first, 2.0 * NEG_BIG, work)
    exps = [jnp.exp(v - vals[0]) for v in vals]
    denom = exps[0] + exps[1] + exps[2] + exps[3]
    table = jnp.zeros_like(logits)
    picked = jnp.zeros_like(logits)
    for k in range(TOP_K):
        table = jnp.where(lane == float(k), firsts[k], table)
        table = jnp.where(lane == float(TOP_K + k), exps[k] / denom, table)
        picked = picked + jnp.where(lane == firsts[k], 1.0, 0.0)
    sel_ref[...] = table
    cnt_ref[...] += jnp.sum(picked, axis=0, keepdims=True)


def _norm_router(xs, gain, mod4, rw_hi, rw_lo, rb, geo, n_rows):
    d = xs.shape[1]
    tm = geo.tm
    bidx = geo.batch_of_tile(tm)
    return pl.pallas_call(
        _norm_router_kernel,
        out_shape=(jax.ShapeDtypeStruct((n_rows, d // 2), jnp.uint32),
                   jax.ShapeDtypeStruct((n_rows, ROUTER_PAD), F32),
                   jax.ShapeDtypeStruct((1, ROUTER_PAD), F32)),
        grid=(n_rows // tm,),
        in_specs=[
            pl.BlockSpec((tm, d), lambda i: (i, 0)),
            pl.BlockSpec((1, d), lambda i: (0, 0)),
            pl.BlockSpec((None, None, 1, d), lambda i: (bidx(i), 4, 0, 0)),
            pl.BlockSpec((None, None, 1, d), lambda i: (bidx(i), 3, 0, 0)),
            pl.BlockSpec((d, ROUTER_PAD), lambda i: (0, 0)),
            pl.BlockSpec((d, ROUTER_PAD), lambda i: (0, 0)),
            pl.BlockSpec((1, ROUTER_PAD), lambda i: (0, 0)),
        ],
        out_specs=(pl.BlockSpec((tm, d // 2), lambda i: (i, 0)),
                   pl.BlockSpec((tm, ROUTER_PAD), lambda i: (i, 0)),
                   pl.BlockSpec((1, ROUTER_PAD), lambda i: (0, 0))),
        compiler_params=_cparams("arbitrary"),
        name="norm_router",
    )(xs, gain.reshape(1, d), mod4, mod4, rw_hi, rw_lo, rb)


def _prep_kernel(zk_ref, zv_ref, zl_ref, zr_ref, pk_ref, pv_ref, pr_ref, nk_ref, nv_ref, nr_ref,
                 conv_ref, w0_ref, wl_ref, a0_ref, al_ref, kk_ref, ka_ref, hsum_ref,
                 v_out, r_out, kkn_out, lw_out, kd_out, bb_out, *, t, lat_tiles, lat_per, ctx_per):
    i = pl.program_id(0)
    is_lat = i < lat_tiles
    per = jnp.where(is_lat, lat_per, ctx_per)
    j = jnp.where(is_lat, i, i - lat_tiles) % per
    first = j == 0
    last = j == per - 1
    row = lax.broadcasted_iota(jnp.int32, (t, 1), 0)

    def conv(z_ref, p_ref, n_ref, which):
        z = z_ref[...].astype(F32)
        prev_row = p_ref[...].astype(F32)[BF16_SUBLANES - 1:BF16_SUBLANES, :]
        next_row = n_ref[...].astype(F32)[0:1, :]
        prev_row = jnp.where(first, 0.0, prev_row)
        next_row = jnp.where(last, 0.0, next_row)
        zm = jnp.where(row == 0, prev_row, pltpu.roll(z, 1, 0))
        zp = jnp.where(row == t - 1, next_row, pltpu.roll(z, t - 1, 0))
        w = conv_ref[which]
        return zm * w[0:1] + z * w[1:2] + zp * w[2:3]

    k = conv(zk_ref, pk_ref, nk_ref, 0)
    v = conv(zv_ref, pv_ref, nv_ref, 1)
    r = conv(zr_ref, pr_ref, nr_ref, 2)
    v_out[...] = v.astype(BF16)
    r_out[...] = r.astype(BF16)
    kkv = k * kk_ref[...]
    ss = _split_dot(kkv * kkv, hsum_ref[...])
    kkn = kkv * lax.rsqrt(jnp.maximum(ss, 1e-24))
    kkn_out[...] = kkn.astype(BF16)
    zl = zl_ref[...]
    for d in range(2):
        wd = zl[:, d * LORA_PAD:(d + 1) * LORA_PAD].astype(F32)
        w_raw = w0_ref[d] + _dot(jnp.tanh(wd).astype(BF16), wl_ref[d])
        lw_out[d] = -_sigmoid(w_raw) * float(np.exp(-0.5))
        ad = zl[:, (2 + d) * LORA_PAD:(3 + d) * LORA_PAD]
        a = _sigmoid(a0_ref[d] + _dot(ad, al_ref[d]))
        kd_out[d] = (k * (1.0 + (a - 1.0) * ka_ref[...])).astype(BF16)
        bb_out[d] = (kkn * a).astype(BF16)


def _rwkv_prepare(z, p, geo):
    rows = z.shape[0]
    t = geo.tp
    base = geo.col0
    hb = t // BF16_SUBLANES
    n_halo = rows // BF16_SUBLANES

    def zcol(off):
        cb = (base + off) // D_RWKV
        return pl.BlockSpec((t, D_RWKV), lambda i: (i, cb))

    def prev(off):
        cb = (base + off) // D_RWKV
        return pl.BlockSpec((BF16_SUBLANES, D_RWKV), lambda i: (jnp.maximum(i * hb - 1, 0), cb))

    def nxt(off):
        cb = (base + off) // D_RWKV
        return pl.BlockSpec((BF16_SUBLANES, D_RWKV),
                            lambda i: (jnp.minimum((i + 1) * hb, n_halo - 1), cb))

    def full(a):
        nd = a.ndim
        return pl.BlockSpec(a.shape, lambda i: (0,) * nd)

    params = (p['conv'], p['w0'], p['w_lora'], p['a0'], p['a_lora'], p['k_k'], p['k_a'], geo.hsum)
    row_spec = pl.BlockSpec((t, D_RWKV), lambda i: (i, 0))
    dir_spec = pl.BlockSpec((2, t, D_RWKV), lambda i: (0, i, 0))
    kern = functools.partial(_prep_kernel, t=t, lat_tiles=geo.r_lat // t, lat_per=geo.s // t,
                             ctx_per=geo.lc // t)
    return pl.pallas_call(
        kern,
        out_shape=(jax.ShapeDtypeStruct((rows, D_RWKV), BF16),
                   jax.ShapeDtypeStruct((rows, D_RWKV), BF16),
                   jax.ShapeDtypeStruct((rows, D_RWKV), BF16),
                   jax.ShapeDtypeStruct((2, rows, D_RWKV), F32),
                   jax.ShapeDtypeStruct((2, rows, D_RWKV), BF16),
                   jax.ShapeDtypeStruct((2, rows, D_RWKV), BF16)),
        grid=(rows // t,),
        in_specs=[zcol(OFF_K), zcol(OFF_V), zcol(OFF_LORA), zcol(OFF_R),
                  prev(OFF_K), prev(OFF_V), prev(OFF_R), nxt(OFF_K), nxt(OFF_V), nxt(OFF_R)]
                 + [full(a) for a in params],
        out_specs=(row_spec, row_spec, row_spec, dir_spec, dir_spec, dir_spec),
        compiler_params=_cparams("parallel"),
        name="rwkv_prepare",
    )(z, z, z, z, z, z, z, z, z, z, *params)


def _wkv_kernel(vf_ref, rf_ref, kf_ref, lwf_ref, kdf_ref, bbf_ref,
                vb_ref, rb_ref, kb_ref, lwb_ref, kdb_ref, bbb_ref, yf_ref, yb_ref, s_ref, *, c, n_sub):
    @pl.when(pl.program_id(1) == 0)
    def _():
        s_ref[...] = jnp.zeros_like(s_ref)

    c2 = 2 * c
    n_pairs = RWKV_HEADS // 2
    ti = lax.broadcasted_iota(jnp.int32, (c, c), 0)
    ii = lax.broadcasted_iota(jnp.int32, (c, c), 1)
    rt = lax.broadcasted_iota(jnp.int32, (c2, c2), 0)
    ci = lax.broadcasted_iota(jnp.int32, (c2, c2), 1)
    same = (rt // c) == (ci // c)
    tri, strict, incl = [], [], []
    for sign in (1, -1):
        tri.append(jnp.where((ti - ii) * sign >= 0, 1.0, 0.0).astype(BF16))
        before = (rt % c - ci % c) * sign
        strict.append(same & (before > 0))
        incl.append(same & (before >= 0))
    head0 = lax.broadcasted_iota(jnp.int32, (c, PAIR), 1) < RWKV_HEAD_DIM
    eye = (lax.broadcasted_iota(jnp.int32, (PAIR, PAIR), 0)
           == lax.broadcasted_iota(jnp.int32, (PAIR, PAIR), 1)).astype(F32)

    def stack(x):
        return jnp.concatenate([jnp.where(head0, x, 0.0), jnp.where(head0, 0.0, x)], axis=0).astype(BF16)

    refs = ((vf_ref, rf_ref, kf_ref, lwf_ref, kdf_ref, bbf_ref, yf_ref),
            (vb_ref, rb_ref, kb_ref, lwb_ref, kdb_ref, bbb_ref, yb_ref))
    chains = [(u, d, p) for u in range(n_sub) for d in range(2) for p in range(n_pairs)]
    rws = [slice(u * c, (u + 1) * c) if d == 0 else slice((n_sub - 1 - u) * c, (n_sub - u) * c)
           for u, d, _ in chains]
    sls = [slice(p * PAIR, (p + 1) * PAIR) for _, _, p in chains]
    dirs = [d for _, d, _ in chains]
    n = len(chains)

    lws = [refs[d][3][rw, sl] for d, rw, sl in zip(dirs, rws, sls)]
    cums = []
    for d, lw in zip(dirs, lws):
        hi = lw.astype(BF16)
        lo = (lw - hi.astype(F32)).astype(BF16)
        cums.append(_dot(tri[d], hi) + _dot(tri[d], lo))
    g_end = [jnp.exp(jnp.sum(lw, axis=0, keepdims=True)) for lw in lws]
    a_s, b_s, k_s, r_s, v_s = [], [], [], [], []
    for d, rw, sl, lw, cum in zip(dirs, rws, sls, lws, cums):
        v_ref, r_ref, kk_ref, _, kd_ref, bb_ref, _ = refs[d]
        g_inv = jnp.exp(-cum)
        a_s.append(stack(-kk_ref[rw, sl].astype(F32) * jnp.exp(cum - lw)))
        b_s.append(stack(bb_ref[rw, sl].astype(F32) * g_inv))
        k_s.append(stack(kd_ref[rw, sl].astype(F32) * g_inv))
        r_s.append(stack(r_ref[rw, sl].astype(F32) * jnp.exp(cum)))
        v_s.append(stack(v_ref[rw, sl].astype(F32)))
    gs = [_dot_nt(jnp.concatenate([a_s[i], r_s[i]], axis=0), jnp.concatenate([b_s[i], k_s[i]], axis=0))
          for i in range(n)]
    l_ab = [jnp.where(strict[d], g[:c2, :c2], 0.0).astype(BF16) for d, g in zip(dirs, gs)]
    l_ak = [jnp.where(strict[d], g[:c2, c2:], 0.0).astype(BF16) for d, g in zip(dirs, gs)]
    m_rb = [jnp.where(incl[d], g[c2:, :c2], 0.0).astype(BF16) for d, g in zip(dirs, gs)]
    m_rk = [jnp.where(incl[d], g[c2:, c2:], 0.0).astype(BF16) for d, g in zip(dirs, gs)]

    eye2 = (rt == ci).astype(F32)
    ts = [eye2 + l.astype(F32) for l in l_ab]
    lps = l_ab
    for _ in range(int(np.log2(c)) - 1):
        lps = [_dot(lp, lp).astype(BF16) for lp in lps]
        ts = [tm + _dot(lp, tm.astype(BF16)) for tm, lp in zip(ts, lps)]
    xbs = [_dot(ts[i].astype(BF16),
                jnp.concatenate([a_s[i], _dot(l_ak[i], v_s[i]).astype(BF16)], axis=1)).astype(BF16)
           for i in range(n)]
    mxs = [_dot(m_rb[i], xbs[i]) for i in range(n)]
    gqs = [(r_s[i].astype(F32) + mxs[i][:, :PAIR]).astype(BF16) for i in range(n)]
    yqs = [mxs[i][:, PAIR:] + _dot(m_rk[i], v_s[i]) for i in range(n)]
    wbs = [_dot_tn(xbs[i], b_s[i]) for i in range(n)]
    vks = [_dot_tn(v_s[i], k_s[i]) for i in range(n)]
    p_mats = [((eye + wbs[i][:PAIR]) * g_end[i]).astype(BF16) for i in range(n)]
    q_mats = [(wbs[i][PAIR:] + vks[i]) * g_end[i] for i in range(n)]
    for i, ((_, d, p), rw, sl) in enumerate(zip(chains, rws, sls)):
        s0b = s_ref[d * n_pairs + p].astype(BF16)
        ys = _dot_nt(gqs[i], s0b) + yqs[i]
        refs[d][6][rw, sl] = ys[:c] + ys[c:]
        s_ref[d * n_pairs + p] = _dot(s0b, p_mats[i]) + q_mats[i]


def _wkv_scan(v, r, kkn, lw, kd, bb, geo):
    rows = v.shape[0]
    c = geo.chunk
    n_sub = 2 if (geo.lc // c) % 2 == 0 and (geo.s // c) % 2 == 0 else 1
    cb = n_sub * c
    n_ctx, n_lat = geo.lc // cb, geo.s // cb
    lat_blocks = geo.r_lat // cb

    def rowblk(d):
        def f(b, s):
            ctx_j = s if d == 0 else n_ctx - 1 - s
            lat_j = s - n_ctx if d == 0 else n_lat - 1 - (s - n_ctx)
            return jnp.where(s < n_ctx, lat_blocks + b * n_ctx + ctx_j, b * n_lat + lat_j)
        return f

    def shared(d):
        f = rowblk(d)
        return pl.BlockSpec((cb, D_RWKV), lambda b, s: (f(b, s), 0))

    def perdir(d):
        f = rowblk(d)
        return pl.BlockSpec((None, cb, D_RWKV), lambda b, s: (d, f(b, s), 0))

    in_specs = []
    for d in range(2):
        in_specs += [shared(d), shared(d), shared(d), perdir(d), perdir(d), perdir(d)]
    return pl.pallas_call(
        functools.partial(_wkv_kernel, c=c, n_sub=n_sub),
        out_shape=(jax.ShapeDtypeStruct((rows, D_RWKV), F32),) * 2,
        grid=(geo.b, n_ctx + n_lat),
        in_specs=in_specs,
        out_specs=(shared(0), shared(1)),
        scratch_shapes=[pltpu.VMEM((RWKV_HEADS, PAIR, PAIR), F32)],
        compiler_params=_cparams("parallel", "arbitrary"),
        name="wkv_scan",
    )(v, r, kkn, lw, kd, bb, v, r, kkn, lw, kd, bb)


def _rwkv_out_kernel(yf_ref, yb_ref, r_ref, v_ref, kd_ref, gd_ref, lg_ref, lb_ref, rk_ref, gl_ref,
                     hsum_ref, o_ref):
    hsum = hsum_ref[...]
    inv_n = 1.0 / RWKV_HEAD_DIM
    y = yf_ref[...] + yb_ref[...]
    mu = _split_dot(y, hsum) * inv_n
    dlt = y - mu
    var = _split_dot(dlt * dlt, hsum) * inv_n
    yn = dlt * lax.rsqrt(var + GN_EPS) * lg_ref[...] + lb_ref[...]
    k_bonus = 0.5 * (kd_ref[0].astype(F32) + kd_ref[1].astype(F32))
    rsum = _split_dot(r_ref[...].astype(F32) * k_bonus * rk_ref[...], hsum)
    bonus = rsum * v_ref[...].astype(F32)
    gate = _dot(_sigmoid(gd_ref[...].astype(F32)).astype(BF16), gl_ref[...])
    o_ref[...] = ((yn + bonus) * gate).astype(BF16)


def _rwkv_output(yf, yb, r, v, kd, z, p, geo):
    rows = r.shape[0]
    t = geo.tp
    gcb = (geo.col0 + OFF_GD) // GATE_LORA
    row_spec = pl.BlockSpec((t, D_RWKV), lambda i: (i, 0))
    dir_spec = pl.BlockSpec((2, t, D_RWKV), lambda i: (0, i, 0))
    vec = pl.BlockSpec((1, D_RWKV), lambda i: (0, 0))
    return pl.pallas_call(
        _rwkv_out_kernel,
        out_shape=jax.ShapeDtypeStruct((rows, D_RWKV), BF16),
        grid=(rows // t,),
        in_specs=[row_spec, row_spec, row_spec, row_spec, dir_spec,
                  pl.BlockSpec((t, GATE_LORA), lambda i: (i, gcb)),
                  vec, vec, vec,
                  pl.BlockSpec((GATE_LORA, D_RWKV), lambda i: (0, 0)),
                  pl.BlockSpec((D_RWKV, D_RWKV), lambda i: (0, 0))],
        out_specs=row_spec,
        compiler_params=_cparams("parallel"),
        name="rwkv_output",
    )(yf, yb, r, v, kd, z, p['lnx_g'], p['lnx_b'], p['r_k'], p['g_lora'], geo.hsum)


def _attn_prep_kernel(q_ref, k_ref, cos_ref, sin_ref, qg_ref, kg_ref, qo_ref, ko_ref):
    cos = cos_ref[...]
    sin = sin_ref[...]
    lane = lax.broadcasted_iota(jnp.int32, cos.shape, 1)
    low_half = (lane % (2 * AXIS_FREQS)) < AXIS_FREQS

    def norm_rope(x, g, scale):
        x = x.astype(F32)
        xn = x * lax.rsqrt(jnp.mean(x * x, axis=-1, keepdims=True) + NORM_EPS) * g
        partner = jnp.where(low_half, pltpu.roll(xn, ATTN_HEAD_DIM - AXIS_FREQS, 1),
                            pltpu.roll(xn, AXIS_FREQS, 1))
        return ((xn * cos + partner * sin) * scale).astype(BF16)

    for h in range(ATTN_HEADS):
        sl = slice(h * ATTN_HEAD_DIM, (h + 1) * ATTN_HEAD_DIM)
        qo_ref[:, sl] = norm_rope(q_ref[:, sl], qg_ref[...], ATTN_SCALE)
    for h in range(ATTN_KV_HEADS):
        sl = slice(h * ATTN_HEAD_DIM, (h + 1) * ATTN_HEAD_DIM)
        ko_ref[:, sl] = norm_rope(k_ref[:, sl], kg_ref[...], 1.0)


def _attn_prep(z, cos_t, sin_t, p, geo):
    rows = z.shape[0]
    t = geo.tp
    qcb = (geo.col0 + OFF_Q) // D_ATTN
    kcb = (geo.col0 + OFF_KA) // D_ATTN_KV
    lat_tiles, lat_per = geo.r_lat // t, geo.s // t

    def tab(i):
        return (jnp.where(i < lat_tiles, i % lat_per, lat_per), 0)

    vec = pl.BlockSpec((1, ATTN_HEAD_DIM), lambda i: (0, 0))
    return pl.pallas_call(
        _attn_prep_kernel,
        out_shape=(jax.ShapeDtypeStruct((rows, D_ATTN), BF16),
                   jax.ShapeDtypeStruct((rows, D_ATTN_KV), BF16)),
        grid=(rows // t,),
        in_specs=[pl.BlockSpec((t, D_ATTN), lambda i: (i, qcb)),
                  pl.BlockSpec((t, D_ATTN_KV), lambda i: (i, kcb)),
                  pl.BlockSpec((t, ATTN_HEAD_DIM), tab),
                  pl.BlockSpec((t, ATTN_HEAD_DIM), tab),
                  vec, vec],
        out_specs=(pl.BlockSpec((t, D_ATTN), lambda i: (i, 0)),
                   pl.BlockSpec((t, D_ATTN_KV), lambda i: (i, 0))),
        compiler_params=_cparams("parallel"),
        name="attn_prep",
    )(z, z, cos_t, sin_t, p['q_norm_g'], p['k_norm_g'])


def _attn_kernel(*refs, n_seg):
    q_ref = refs[0]
    k_refs = refs[1:1 + n_seg]
    v_refs = refs[1 + n_seg:1 + 2 * n_seg]
    o_ref = refs[1 + 2 * n_seg]
    for g in range(GQA_GROUP):
        sl = slice(g * ATTN_HEAD_DIM, (g + 1) * ATTN_HEAD_DIM)
        q = q_ref[:, sl]
        scores = [_dot_nt(q, k_ref[...]) for k_ref in k_refs]
        m = scores[0].max(axis=-1, keepdims=True)
        for s in scores[1:]:
            m = jnp.maximum(m, s.max(axis=-1, keepdims=True))
        denom = None
        acc = None
        for s, v_ref in zip(scores, v_refs):
            e = jnp.exp(s - m)
            es = e.sum(axis=-1, keepdims=True)
            pv = _dot(e.astype(BF16), v_ref[...])
            denom = es if denom is None else denom + es
            acc = pv if acc is None else acc + pv
        o_ref[:, sl] = (acc / denom).astype(BF16)


def _attention(qn, kn, z, geo, latent):
    gw = GQA_GROUP * ATTN_HEAD_DIM
    vcb = (geo.col0 + OFF_VA) // ATTN_HEAD_DIM
    ctx_blk0 = geo.r_lat // geo.lc
    k_ctx = pl.BlockSpec((geo.lc, ATTN_HEAD_DIM), lambda b, h, i: (ctx_blk0 + b, h))
    v_ctx = pl.BlockSpec((geo.lc, ATTN_HEAD_DIM), lambda b, h, i: (ctx_blk0 + b, vcb + h))
    if latent:
        tq = geo.tq
        per = geo.s // tq
        q_spec = pl.BlockSpec((tq, gw), lambda b, h, i: (b * per + i, h))
        k_lat = pl.BlockSpec((geo.s, ATTN_HEAD_DIM), lambda b, h, i: (b, h))
        v_lat = pl.BlockSpec((geo.s, ATTN_HEAD_DIM), lambda b, h, i: (b, vcb + h))
        in_specs = [q_spec, k_ctx, k_lat, v_ctx, v_lat]
        args = (qn, kn, kn, z, z)
        n_seg, out_rows = 2, geo.r_lat
        o_spec = q_spec
    else:
        tq, per = geo.lc, 1
        q_spec = pl.BlockSpec((tq, gw), lambda b, h, i: (ctx_blk0 + b, h))
        in_specs = [q_spec, k_ctx, v_ctx]
        args = (qn, kn, z)
        n_seg, out_rows = 1, geo.r_ctx
        o_spec = pl.BlockSpec((tq, gw), lambda b, h, i: (b, h))
    return pl.pallas_call(
        functools.partial(_attn_kernel, n_seg=n_seg),
        out_shape=jax.ShapeDtypeStruct((out_rows, D_ATTN), BF16),
        grid=(geo.b, ATTN_KV_HEADS, per),
        in_specs=in_specs,
        out_specs=o_spec,
        compiler_params=_cparams("parallel", "parallel", "parallel"),
        name="attention_lat" if latent else "attention_ctx",
    )(*args)


def _mm_kernel(x_ref, w_ref, o_ref):
    o_ref[...] = _dot(x_ref[...], w_ref[...]).astype(o_ref.dtype)


def _fourier_channels(z, wcs, geo):
    rows = z.shape[0]
    t = geo.tp
    fcb = (geo.col0 + OFF_FOUR) // D_FOURIER
    return pl.pallas_call(
        _mm_kernel,
        out_shape=jax.ShapeDtypeStruct((rows, 2 * D_FOURIER), BF16),
        grid=(rows // t,),
        in_specs=[pl.BlockSpec((t, D_FOURIER), lambda i: (i, fcb)),
                  pl.BlockSpec((D_FOURIER, 2 * D_FOURIER), lambda i: (0, 0))],
        out_specs=pl.BlockSpec((t, 2 * D_FOURIER), lambda i: (i, 0)),
        compiler_params=_cparams("parallel"),
        name="fourier_channels",
    )(z, wcs)


def _dft_kernel(cl_ref, sl_ref, xc_ref, xs_ref, o_ref):
    o_ref[...] = (_dot(cl_ref[...], xc_ref[...]) - _dot(sl_ref[...], xs_ref[...])).astype(BF16)


def _fourier_positions(xcs, cl, sl, n, blk0, nb, tmf):
    per = n // tmf
    return pl.pallas_call(
        _dft_kernel,
        out_shape=jax.ShapeDtypeStruct((nb * n, D_FOURIER), BF16),
        grid=(per, nb),
        in_specs=[pl.BlockSpec((tmf, n), lambda i, b: (i, 0)),
                  pl.BlockSpec((tmf, n), lambda i, b: (i, 0)),
                  pl.BlockSpec((n, D_FOURIER), lambda i, b: (blk0 + b, 0)),
                  pl.BlockSpec((n, D_FOURIER), lambda i, b: (blk0 + b, 1))],
        out_specs=pl.BlockSpec((tmf, D_FOURIER), lambda i, b: (b * per + i, 0)),
        compiler_params=_cparams("parallel", "parallel"),
        name="fourier_positions",
    )(cl, sl, xcs, xcs)


def _merge_kernel(rw_ref, fo_ref, at_ref, ga_ref, gf_ref, gc_ref, wr_ref, wf_ref, wa_ref, o_ref):
    y = _sigmoid(ga_ref[...].astype(F32)) * _dot(rw_ref[...], wr_ref[...])
    y += _sigmoid(gf_ref[...].astype(F32)) * _dot(fo_ref[...], wf_ref[...])
    y += _sigmoid(gc_ref[...].astype(F32)) * _dot(at_ref[...], wa_ref[...])
    o_ref[...] = y.astype(BF16)


def _merge(rw, fo, at, z, p, geo, n_rows):
    d = geo.d
    tm = geo.tm
    tn = min(1024, d)
    nj = d // tn
    return pl.pallas_call(
        _merge_kernel,
        out_shape=jax.ShapeDtypeStruct((n_rows, d), BF16),
        grid=(nj, n_rows // tm),
        in_specs=[pl.BlockSpec((tm, D_RWKV), lambda j, i: (i, 0)),
                  pl.BlockSpec((tm, D_FOURIER), lambda j, i: (i, 0)),
                  pl.BlockSpec((tm, D_ATTN), lambda j, i: (i, 0)),
                  pl.BlockSpec((tm, tn), lambda j, i: (i, j)),
                  pl.BlockSpec((tm, tn), lambda j, i: (i, nj + j)),
                  pl.BlockSpec((tm, tn), lambda j, i: (i, 2 * nj + j)),
                  pl.BlockSpec((D_RWKV, tn), lambda j, i: (0, j)),
                  pl.BlockSpec((D_FOURIER, tn), lambda j, i: (0, j)),
                  pl.BlockSpec((D_ATTN, tn), lambda j, i: (0, j))],
        out_specs=pl.BlockSpec((tm, tn), lambda j, i: (i, j)),
        compiler_params=_cparams("parallel", "parallel"),
        name="merge",
    )(rw, fo, at, z, z, z, p['w_br_rwkv'], p['w_br_fourier'], p['w_br_attn'])


def _outres_kernel(y_ref, w_ref, x_ref, gt_ref, o_ref):
    o_ref[...] = x_ref[...] + gt_ref[...] * _dot(y_ref[...], w_ref[...])


def _out_proj_residual(y, w, xs, mod4, geo, n_rows):
    d = geo.d
    tm = geo.tm
    tn = min(1024, d)
    bidx = geo.batch_of_tile(tm)
    return pl.pallas_call(
        _outres_kernel,
        out_shape=jax.ShapeDtypeStruct((n_rows, d), F32),
        grid=(d // tn, n_rows // tm),
        in_specs=[pl.BlockSpec((tm, d), lambda j, i: (i, 0)),
                  pl.BlockSpec((d, tn), lambda j, i: (0, j)),
                  pl.BlockSpec((tm, tn), lambda j, i: (i, j)),
                  pl.BlockSpec((None, None, 1, tn), lambda j, i: (bidx(i), 2, 0, j))],
        out_specs=pl.BlockSpec((tm, tn), lambda j, i: (i, j)),
        compiler_params=_cparams("parallel", "parallel"),
        name="out_proj_residual",
    )(y, w, xs, mod4)


def _expert_ffn(h_lo, h_hi, wgu_ref, bgu_ref, wd_ref, bd_ref, f):
    half = wgu_ref.shape[0] // 2
    gu = _dot(h_lo, wgu_ref[:half, :]) + _dot(h_hi, wgu_ref[half:, :]) + bgu_ref[...]
    gate = jnp.minimum(gu[:, :f], SWIGLU_LIMIT)
    up = jnp.clip(gu[:, f:], -SWIGLU_LIMIT, SWIGLU_LIMIT)
    act = (up + 1.0) * gate * _sigmoid(SWIGLU_ALPHA * gate)
    return _dot(act.astype(BF16), wd_ref[...]) + bd_ref[...]


def _route(sel, cnt, n, t, n_exp):
    p = TOP_K * n
    max_used = p // t + n_exp
    assert max_used % 2 == 0
    n_steps = max_used // 2 + 1
    n_tiles = 2 * n_steps
    unused_key = 2 * n_exp
    counts = cnt[0, :n_exp].astype(jnp.int32)
    ptiles = (counts + t - 1) // t
    n_used = jnp.sum(ptiles)
    pad = ptiles * t - counts
    experts = jnp.arange(n_exp, dtype=jnp.int32)
    last_e = jnp.max(jnp.where(counts > 0, experts, 0))

    e_real = sel[:, :TOP_K].astype(jnp.int32).reshape(p)
    w_real = sel[:, TOP_K:2 * TOP_K].reshape(p)
    pair = jnp.arange(p, dtype=jnp.int32)
    e_cand = jnp.repeat(experts, t)
    c_cand = jnp.tile(jnp.arange(t, dtype=jnp.int32), n_exp)
    key_cand = jnp.where(c_cand < jnp.repeat(pad, t), 2 * e_cand + 1, unused_key)
    n_extra = n_tiles * t - p - n_exp * t
    dest_bits = int(p + (n_exp + 2) * t - 1).bit_length()
    assert (unused_key + 1) << dest_bits < 2 ** 31
    words = jnp.concatenate([
        ((2 * e_real) << dest_bits) | ((pair % TOP_K) * n + pair // TOP_K),
        (key_cand << dest_bits) | (p + e_cand * t + c_cand),
        jnp.full((n_extra,), unused_key << dest_bits, jnp.int32)])
    ws = jnp.concatenate([w_real, jnp.zeros((n_exp * t + n_extra,), F32)])
    word_s, w_s = lax.sort((words, ws), num_keys=1, is_stable=True)

    lane = jnp.arange(t, dtype=jnp.int32)[None, :]
    spare = p + n_exp * t + lane
    word_s = word_s.reshape(n_tiles, t)
    key_s, dest_s = word_s >> dest_bits, word_s & ((1 << dest_bits) - 1)
    dest = jnp.where(key_s == unused_key, spare, dest_s)
    tok_s = jnp.where(key_s % 2 == 0, dest_s % n, 0)
    tok_s = jnp.where(key_s == unused_key, 0, tok_s)
    te = jnp.where(jnp.arange(n_tiles) < n_used, jnp.minimum(key_s[:, 0] // 2, n_exp - 1), last_e)
    dest_prev = jnp.concatenate([spare, dest[1:-1:2]], axis=0)
    return (te.astype(jnp.int32), n_used.reshape(1).astype(jnp.int32), tok_s.reshape(n_tiles, 1, t),
            dest.reshape(n_tiles, 1, t), dest_prev.reshape(n_steps, 1, t), w_s.reshape(n_tiles * t, 1))


def _moe_routed_kernel(te_ref, nu_ref, toka_ref, tokb_ref, tokn_ref, dsta_ref, dstp_ref, wa_ref, wb_ref,
                       h_hbm, wgu_a, bgu_a, wd_a, bd_a, wgu_b, bgu_b, wd_b, bd_b, o_hbm,
                       ga, gb, oa, ob, gsem, ssem, *, f, t, dump0, n_dump):
    j = pl.program_id(0)
    n_used = nu_ref[0]

    def gather(idx_ref, buf, sem):
        @pl.when(n_used > 0)
        def _():
            for r in range(t):
                pltpu.make_async_copy(h_hbm.at[pl.ds(idx_ref[0, 0, r], 1)], buf.at[pl.ds(r, 1)], sem).start()

    def scatter(buf, idx_ref, sem):
        @pl.when(n_used > 0)
        def _():
            for r in range(t):
                pltpu.make_async_copy(buf.at[pl.ds(r, 1)], o_hbm.at[pl.ds(idx_ref[0, 0, r], 1)], sem).start()

    def wait_rows_in(buf, sem):
        pltpu.make_async_copy(h_hbm.at[pl.ds(0, t)], buf, sem).wait()

    def wait_rows_out(buf, sem):
        pltpu.make_async_copy(buf, o_hbm.at[pl.ds(0, t)], sem).wait()

    def experts(buf, wgu, bgu, wd, bd, w_ref):
        lo, hi = _unpack_halves(buf[...])
        y = _expert_ffn(lo.astype(BF16), hi.astype(BF16), wgu, bgu, wd, bd, f)
        return _pack_halves(y * w_ref[...])

    @pl.when(j == 0)
    def _():
        gather(toka_ref, ga, gsem.at[0])
        ob[...] = jnp.zeros_like(ob)
        fills = [pltpu.make_async_copy(ob, o_hbm.at[pl.ds(dump0 + e * t, t)], ssem.at[1])
                 for e in range(n_dump - 1)]
        for c in fills:
            c.start()
        for c in fills:
            c.wait()
        pltpu.make_async_copy(ob, o_hbm.at[pl.ds(dump0 + (n_dump - 1) * t, t)], ssem.at[0]).start()

    @pl.when(2 * j <= n_used)
    def _():
        wait_rows_in(ga, gsem.at[0])
        scatter(ob, dstp_ref, ssem.at[1])
        gather(tokb_ref, gb, gsem.at[1])
        ya = experts(ga, wgu_a, bgu_a, wd_a, bd_a, wa_ref)
        wait_rows_out(oa, ssem.at[0])
        oa[...] = ya
        wait_rows_in(gb, gsem.at[1])
        scatter(oa, dsta_ref, ssem.at[0])
        gather(tokn_ref, ga, gsem.at[0])
        yb = experts(gb, wgu_b, bgu_b, wd_b, bd_b, wb_ref)
        wait_rows_out(ob, ssem.at[1])
        ob[...] = yb

    @pl.when(j == pl.num_programs(0) - 1)
    def _():
        wait_rows_in(ga, gsem.at[0])
        wait_rows_out(oa, ssem.at[0])


def _moe_routed(hp, sel, cnt, wgu, bgu, wd, bd, geo, n_rows):
    d = geo.d
    n_exp, _, f2 = wgu.shape
    f = f2 // 2
    t = geo.t_moe
    te, n_used, tok, dest, dest_prev, w = _route(sel, cnt, n_rows, t, n_exp)
    n_tiles, n_steps = tok.shape[0], dest_prev.shape[0]
    n_dump = n_exp + 2
    smem = functools.partial(pl.BlockSpec, (1, 1, t), memory_space=pltpu.SMEM)

    def weights(which):
        return [pl.BlockSpec((None, d, f2), lambda j, te, nu: (te[2 * j + which], 0, 0)),
                pl.BlockSpec((None, 1, f2), lambda j, te, nu: (te[2 * j + which], 0, 0)),
                pl.BlockSpec((None, f, d), lambda j, te, nu: (te[2 * j + which], 0, 0)),
                pl.BlockSpec((None, 1, d), lambda j, te, nu: (te[2 * j + which], 0, 0))]

    grid_spec = pltpu.PrefetchScalarGridSpec(
        num_scalar_prefetch=2,
        grid=(n_steps,),
        in_specs=[smem(lambda j, te, nu: (2 * j, 0, 0)),
                  smem(lambda j, te, nu: (2 * j + 1, 0, 0)),
                  smem(lambda j, te, nu: (jnp.minimum(2 * j + 2, n_tiles - 1), 0, 0)),
                  smem(lambda j, te, nu: (2 * j, 0, 0)),
                  smem(lambda j, te, nu: (j, 0, 0)),
                  pl.BlockSpec((t, 1), lambda j, te, nu: (2 * j, 0)),
                  pl.BlockSpec((t, 1), lambda j, te, nu: (2 * j + 1, 0)),
                  pl.BlockSpec(memory_space=pl.ANY)] + weights(0) + weights(1),
        out_specs=pl.BlockSpec(memory_space=pl.ANY),
        scratch_shapes=[pltpu.VMEM((t, d // 2), jnp.uint32)] * 4
                       + [pltpu.SemaphoreType.DMA((2,)), pltpu.SemaphoreType.DMA((2,))])
    return pl.pallas_call(
        functools.partial(_moe_routed_kernel, f=f, t=t, dump0=TOP_K * n_rows, n_dump=n_dump),
        out_shape=jax.ShapeDtypeStruct((TOP_K * n_rows + n_dump * t, d // 2), jnp.uint32),
        grid_spec=grid_spec,
        compiler_params=pltpu.CompilerParams(dimension_semantics=("arbitrary",),
                                             vmem_limit_bytes=VMEM_LIMIT, disable_bounds_checks=True),
        name="moe_routed",
    )(te, n_used, tok, tok, tok, dest, dest_prev, w, w, hp, wgu, bgu, wd, bd, wgu, bgu, wd, bd)


def _combine_kernel(x_ref, gt_ref, y0_ref, y1_ref, y2_ref, y3_ref, o_ref):
    half = x_ref.shape[1] // 2
    parts = [_unpack_halves(y[...]) for y in (y0_ref, y1_ref, y2_ref, y3_ref)]
    lo = (parts[0][0] + parts[1][0]) + (parts[2][0] + parts[3][0])
    hi = (parts[0][1] + parts[1][1]) + (parts[2][1] + parts[3][1])
    o_ref[:, :half] = x_ref[:, :half] + gt_ref[:, :half] * lo
    o_ref[:, half:] = x_ref[:, half:] + gt_ref[:, half:] * hi


def _moe_combine(y4, xs, mod4, geo, n_rows):
    d = geo.d
    tm = geo.t_moe
    per = n_rows // tm
    bidx = geo.batch_of_tile(tm)

    def part(k):
        return pl.BlockSpec((tm, d // 2), lambda i: (k * per + i, 0))

    return pl.pallas_call(
        _combine_kernel,
        out_shape=jax.ShapeDtypeStruct((n_rows, d), F32),
        grid=(per,),
        in_specs=[pl.BlockSpec((tm, d), lambda i: (i, 0)),
                  pl.BlockSpec((None, None, 1, d), lambda i: (bidx(i), 5, 0, 0)),
                  part(0), part(1), part(2), part(3)],
        out_specs=pl.BlockSpec((tm, d), lambda i: (i, 0)),
        compiler_params=_cparams("parallel"),
        name="moe_combine",
    )(xs, mod4, y4, y4, y4, y4)


class _Geometry:
    def __init__(self, b, s, lc, d):
        self.b, self.s, self.lc, self.d = b, s, lc, d
        self.r_lat, self.r_ctx = b * s, b * lc
        self.rows = self.r_lat + self.r_ctx
        self.col0 = 3 * d
        assert self.col0 % 1024 == 0, "gate columns must end on a 1024-column boundary"
        self.tm = min(1024, s, self.r_ctx)
        self.tp = min(256, s, lc)
        self.tq = min(512, s)
        self.chunk = min(WKV_CHUNK, lc, s)
        self.t_moe = min(256, self.tm)
        for t in (self.tm,):
            assert s % t == 0 and self.r_ctx % t == 0
        assert s % self.tp == 0 and lc % self.tp == 0 and self.tp % BF16_SUBLANES == 0
        assert s % self.chunk == 0 and lc % self.chunk == 0 and s % GRID_W == 0
        self.hsum = jnp.asarray(np.kron(np.eye(RWKV_HEADS), np.ones((RWKV_HEAD_DIM,) * 2)), BF16)

    def batch_of_tile(self, tm):
        per, nb = self.s // tm, self.b
        return lambda i: jnp.minimum(i // per, nb)


def _dft_tables(n):
    j = np.arange(n, dtype=np.int64)
    ang = 2.0 * np.pi * ((j[:, None] * j[None, :]) % n).astype(np.float64) / n
    return np.cos(ang) / np.sqrt(n), np.sin(ang) / np.sqrt(n)


def _rope_tables(s, t):
    pos = np.arange(s)
    inv_freq = ROPE_THETA ** (-np.arange(AXIS_FREQS, dtype=np.float32) / AXIS_FREQS)
    ang_r = (pos // GRID_W).astype(np.float32)[:, None] * inv_freq.astype(np.float32)
    ang_c = (pos % GRID_W).astype(np.float32)[:, None] * inv_freq.astype(np.float32)
    cr, sr, cc, sc = np.cos(ang_r), np.sin(ang_r), np.cos(ang_c), np.sin(ang_c)
    cos = np.concatenate([cr, cr, cc, cc], axis=1)
    sin = np.concatenate([-sr, sr, -sc, sc], axis=1)
    cos = np.concatenate([cos, np.ones((t, ATTN_HEAD_DIM))], axis=0)
    sin = np.concatenate([sin, np.zeros((t, ATTN_HEAD_DIM))], axis=0)
    return jnp.asarray(cos, F32), jnp.asarray(sin, F32)


def _permute_w_in(w_in, d):
    o = np.cumsum([0, D_RWKV, D_RWKV, LORA, LORA, LORA, LORA, D_ATTN_KV, D_ATTN_KV, D_RWKV, GATE_LORA,
                   D_ATTN, D_FOURIER, 3 * d])
    w_in = w_in.astype(BF16)
    seg = lambda i: w_in[..., o[i]:o[i + 1]]
    pad = lambda a: jnp.pad(a, ((0, 0), (0, 0), (0, LORA_PAD - LORA)))
    parts = [seg(12), seg(0), seg(1), pad(seg(2)), pad(seg(3)), pad(seg(4)), pad(seg(5)), seg(8), seg(10),
             seg(11), seg(6), seg(7), seg(9)]
    w = jnp.concatenate(parts, axis=-1)
    n = w.shape[-1]
    n_pad = -(-n // W_IN_TN) * W_IN_TN
    return jnp.pad(w, ((0, 0), (0, 0), (0, n_pad - n)))


def _split_hi_lo(w):
    hi = w.astype(BF16)
    return hi, (w - hi.astype(F32)).astype(BF16)


def kernel(x, c, ctx, c_ctx, ada_w, ada_b, norm1_g, norm2_g, w_in, rwkv_conv, w0, w_lora, a0, a_lora, g_lora, k_k, k_a, r_k, lnx_g, lnx_b, q_norm_g, k_norm_g, w_br_rwkv, w_br_fourier, w_br_attn, w_out, router_w, router_b, exp_w_gu, exp_b_gu, exp_w_down, exp_b_down):
    b, s, d = x.shape
    lc = ctx.shape[1]
    depth = w_in.shape[0]
    n_exp = router_w.shape[-1]
    geo = _Geometry(b, s, lc, d)

    w_in_p = _permute_w_in(w_in, d)
    lora_pad = ((0, 0), (0, 0), (0, LORA_PAD - LORA), (0, 0))
    w_lora_p = jnp.pad(w_lora, lora_pad).astype(BF16)
    a_lora_p = jnp.pad(a_lora, lora_pad).astype(BF16)
    g_lora_b = g_lora.astype(BF16)
    wbr_r, wbr_f, wbr_a, w_out_b = (w.astype(BF16) for w in (w_br_rwkv, w_br_fourier, w_br_attn, w_out))
    rw_p = jnp.pad(router_w, ((0, 0), (0, 0), (0, ROUTER_PAD - n_exp)))
    rw_hi, rw_lo = _split_hi_lo(rw_p)
    rb_p = jnp.pad(router_b, ((0, 0), (0, ROUTER_PAD - n_exp)), constant_values=NEG_BIG)
    wgu_b, wd_b = exp_w_gu.astype(BF16), exp_w_down.astype(BF16)
    cos_t, sin_t = _rope_tables(s, geo.tp)
    cc, sc = _dft_tables(FOURIER_GROUP_DIM)
    groups = D_FOURIER // FOURIER_GROUP_DIM
    wcs = jnp.asarray(np.concatenate([np.kron(np.eye(groups), cc), np.kron(np.eye(groups), sc)], axis=1), BF16)
    cl_lat, sl_lat = (jnp.asarray(m, BF16) for m in _dft_tables(s))
    cl_ctx, sl_ctx = (jnp.asarray(m, BF16) for m in _dft_tables(lc))

    mod_rows = -(-(b + 1) // BF16_SUBLANES) * BF16_SUBLANES
    c_all = jnp.concatenate([c, c_ctx[None], jnp.zeros((mod_rows - b - 1, d), F32)], axis=0)
    mod = _adaln(c_all, ada_w, ada_b).reshape(depth, mod_rows, 6, 1, d)

    xs = jnp.concatenate([x.reshape(b * s, d), ctx.reshape(b * lc, d)], axis=0)
    for l in range(depth):
        last = l == depth - 1
        mod4 = mod[l]
        p = dict(conv=rwkv_conv[l], w0=w0[l][:, None, :], w_lora=w_lora_p[l], a0=a0[l][:, None, :],
                 a_lora=a_lora_p[l], k_k=k_k[l][None], k_a=k_a[l][None], r_k=r_k[l][None],
                 lnx_g=lnx_g[l][None], lnx_b=lnx_b[l][None], g_lora=g_lora_b[l],
                 q_norm_g=q_norm_g[l][None], k_norm_g=k_norm_g[l][None],
                 w_br_rwkv=wbr_r[l], w_br_fourier=wbr_f[l], w_br_attn=wbr_a[l])
        n_rows = geo.r_lat if last else geo.rows

        z = _norm_in_proj(xs, norm1_g[l], mod4, w_in_p[l], geo)

        v_c, r_c, kkn, lw, kd, bb = _rwkv_prepare(z, p, geo)
        yf, yb = _wkv_scan(v_c, r_c, kkn, lw, kd, bb, geo)
        rw = _rwkv_output(yf, yb, r_c, v_c, kd, z, p, geo)

        qn, kn = _attn_prep(z, cos_t, sin_t, p, geo)
        att = _attention(qn, kn, z, geo, latent=True)

        xcs = _fourier_channels(z, wcs, geo)
        fo = _fourier_positions(xcs, cl_lat, sl_lat, s, 0, b, min(1024, s))
        if not last:
            att = jnp.concatenate([att, _attention(qn, kn, z, geo, latent=False)], axis=0)
            fo_c = _fourier_positions(xcs, cl_ctx, sl_ctx, lc, geo.r_lat // lc, b, lc)
            fo = jnp.concatenate([fo, fo_c], axis=0)

        ym = _merge(rw, fo, att, z, p, geo, n_rows)
        xs = _out_proj_residual(ym, w_out_b[l], xs, mod4, geo, n_rows)

        hp, sel, cnt = _norm_router(xs, norm2_g[l], mod4, rw_hi[l], rw_lo[l], rb_p[l][None], geo, n_rows)
        y4 = _moe_routed(hp, sel, cnt, wgu_b[l], exp_b_gu[l][:, None, :], wd_b[l], exp_b_down[l][:, None, :],
                         geo, n_rows)
        xs = _moe_combine(y4, xs, mod4, geo, n_rows)
    return xs[:geo.r_lat].reshape(b, s, d)
```

```python
import functools

import numpy as np
import jax
import jax.numpy as jnp
from jax import lax
from jax.experimental import pallas as pl
from jax.experimental.pallas import tpu as pltpu

F32 = jnp.float32
BF16 = jnp.bfloat16

NORM_EPS = 1e-6
GN_EPS = 64e-5
RWKV_HEADS = 8
RWKV_HEAD_DIM = 64
D_RWKV = RWKV_HEADS * RWKV_HEAD_DIM
LORA = 96
GATE_LORA = 256
D_FOURIER = 512
FOURIER_GROUP_DIM = 128
ATTN_HEADS = 8
ATTN_KV_HEADS = 2
ATTN_HEAD_DIM = 128
GQA_GROUP = ATTN_HEADS // ATTN_KV_HEADS
D_ATTN = ATTN_HEADS * ATTN_HEAD_DIM
D_ATTN_KV = ATTN_KV_HEADS * ATTN_HEAD_DIM
ATTN_SCALE = ATTN_HEAD_DIM ** -0.5
GRID_W = 64
ROPE_THETA = 10000.0
AXIS_FREQS = ATTN_HEAD_DIM // 4
TOP_K = 4
SWIGLU_LIMIT = 7.0
SWIGLU_ALPHA = 1.702

LANES = 128
BF16_SUBLANES = 16
VMEM_LIMIT = 56 * 1024 * 1024

LORA_PAD = LANES
W_IN_TN = 1536
WKV_CHUNK = 64
PAIR = 2 * RWKV_HEAD_DIM
ROUTER_PAD = LANES
NEG_BIG = -1e30

OFF_K, OFF_V, OFF_LORA, OFF_R, OFF_Q, OFF_FOUR, OFF_KA, OFF_VA, OFF_GD, OFF_END = (
    0, 512, 1024, 1536, 2048, 3072, 3584, 3840, 4096, 4352)


def _cparams(*sem):
    return pltpu.CompilerParams(dimension_semantics=sem, vmem_limit_bytes=VMEM_LIMIT)


def _dot(a, b):
    return jnp.dot(a, b, preferred_element_type=F32)


def _dot_nt(a, b):
    return lax.dot_general(a, b, (((1,), (1,)), ((), ())), preferred_element_type=F32)


def _dot_tn(a, b):
    return lax.dot_general(a, b, (((0,), (0,)), ((), ())), preferred_element_type=F32)


def _split_dot(x, g):
    hi = x.astype(BF16)
    lo = (x - hi.astype(F32)).astype(BF16)
    return _dot(hi, g) + _dot(lo, g)


def _sigmoid(x):
    return 1.0 / (1.0 + jnp.exp(-x))


def _adaln_kernel(c_ref, w_ref, b_ref, o_ref):
    c = c_ref[...]
    s = (c * _sigmoid(c)).astype(BF16)
    o_ref[...] = _dot(s, w_ref[...].astype(BF16)) + b_ref[...]


def _adaln(c_all, ada_w, ada_b):
    depth, d, n = ada_w.shape
    rows = c_all.shape[0]
    tn = 1024 if n % 1024 == 0 else n
    return pl.pallas_call(
        _adaln_kernel,
        out_shape=jax.ShapeDtypeStruct((depth, rows, n), F32),
        grid=(depth, n // tn),
        in_specs=[
            pl.BlockSpec((rows, d), lambda l, j: (0, 0)),
            pl.BlockSpec((None, d, tn), lambda l, j: (l, 0, j)),
            pl.BlockSpec((None, 1, tn), lambda l, j: (l, 0, j)),
        ],
        out_specs=pl.BlockSpec((None, rows, tn), lambda l, j: (l, 0, j)),
        compiler_params=_cparams("parallel", "parallel"),
        name="adaln",
    )(c_all, ada_w, ada_b.reshape(depth, 1, n))


def _modulated_norm(x, g, sc, sh):
    ms = jnp.mean(x * x, axis=-1, keepdims=True)
    return x * lax.rsqrt(ms + NORM_EPS) * g * (1.0 + sc) + sh


def _normmm_kernel(x_ref, g_ref, sc_ref, sh_ref, w_ref, o_ref, h_ref):
    @pl.when(pl.program_id(1) == 0)
    def _():
        h_ref[...] = _modulated_norm(x_ref[...], g_ref[...], sc_ref[...], sh_ref[...]).astype(BF16)

    o_ref[...] = _dot(h_ref[...], w_ref[...]).astype(o_ref.dtype)


def _norm_in_proj(xs, gain, mod4, w, geo):
    rows, d = xs.shape
    n = w.shape[1]
    tm, tn = geo.tm, W_IN_TN
    bidx = geo.batch_of_tile(tm)
    return pl.pallas_call(
        _normmm_kernel,
        out_shape=jax.ShapeDtypeStruct((rows, n), BF16),
        grid=(rows // tm, n // tn),
        in_specs=[
            pl.BlockSpec((tm, d), lambda i, j: (i, 0)),
            pl.BlockSpec((1, d), lambda i, j: (0, 0)),
            pl.BlockSpec((None, None, 1, d), lambda i, j: (bidx(i), 1, 0, 0)),
            pl.BlockSpec((None, None, 1, d), lambda i, j: (bidx(i), 0, 0, 0)),
            pl.BlockSpec((d, tn), lambda i, j: (0, j)),
        ],
        out_specs=pl.BlockSpec((tm, tn), lambda i, j: (i, j)),
        scratch_shapes=[pltpu.VMEM((tm, d), BF16)],
        compiler_params=_cparams("parallel", "arbitrary"),
        name="norm_in_proj",
    )(xs, gain.reshape(1, d), mod4, mod4, w)


def _pack_halves(x):
    half = x.shape[1] // 2
    xb = x.astype(BF16).astype(F32)
    lo = lax.shift_right_logical(lax.bitcast_convert_type(xb[:, :half], jnp.uint32), jnp.uint32(16))
    hi = lax.bitcast_convert_type(xb[:, half:], jnp.uint32) & jnp.uint32(0xFFFF0000)
    return lo | hi


def _unpack_halves(words):
    lo = lax.bitcast_convert_type(lax.shift_left(words, jnp.uint32(16)), F32)
    hi = lax.bitcast_convert_type(words & jnp.uint32(0xFFFF0000), F32)
    return lo, hi


def _norm_router_kernel(x_ref, g_ref, sc_ref, sh_ref, wh_ref, wl_ref, rb_ref, h_ref, sel_ref, cnt_ref):
    @pl.when(pl.program_id(0) == 0)
    def _():
        cnt_ref[...] = jnp.zeros_like(cnt_ref)

    h = _modulated_norm(x_ref[...], g_ref[...], sc_ref[...], sh_ref[...])
    h_ref[...] = _pack_halves(h)
    hi = h.astype(BF16)
    lo = (h - hi.astype(F32)).astype(BF16)
    logits = _dot(hi, wh_ref[...]) + _dot(lo, wh_ref[...]) + _dot(hi, wl_ref[...]) + rb_ref[...]
    lane = lax.broadcasted_iota(jnp.int32, logits.shape, 1).astype(F32)
    work = logits
    vals, firsts = [], []
    for _ in range(TOP_K):
        m = jnp.max(work, axis=-1, keepdims=True)
        first = jnp.min(jnp.where(work == m, lane, float(ROUTER_PAD)), axis=-1, keepdims=True)
        vals.append(m)
        firsts.append(first)
        work = jnp.where(lane == first, 2.0 * NEG_BIG, work)
    exps = [jnp.exp(v - vals[0]) for v in vals]
    denom = exps[0] + exps[1] + exps[2] + exps[3]
    table = jnp.zeros_like(logits)
    picked = jnp.zeros_like(logits)
    for k in range(TOP_K):
        table = jnp.where(lane == float(k), firsts[k], table)
        table = jnp.where(lane == float(TOP_K + k), exps[k] / denom, table)
        picked = picked + jnp.where(lane == firsts[k], 1.0, 0.0)
    sel_ref[...] = table
    cnt_ref[...] += jnp.sum(picked, axis=0, keepdims=True)


def _norm_router(xs, gain, mod4, rw_hi, rw_lo, rb, geo, n_rows):
    d = xs.shape[1]
    tm = geo.tm
    bidx = geo.batch_of_tile(tm)
    return pl.pallas_call(
        _norm_router_kernel,
        out_shape=(jax.ShapeDtypeStruct((n_rows, d // 2), jnp.uint32),
                   jax.ShapeDtypeStruct((n_rows, ROUTER_PAD), F32),
                   jax.ShapeDtypeStruct((1, ROUTER_PAD), F32)),
        grid=(n_rows // tm,),
        in_specs=[
            pl.BlockSpec((tm, d), lambda i: (i, 0)),
            pl.BlockSpec((1, d), lambda i: (0, 0)),
            pl.BlockSpec((None, None, 1, d), lambda i: (bidx(i), 4, 0, 0)),
            pl.BlockSpec((None, None, 1, d), lambda i: (bidx(i), 3, 0, 0)),
            pl.BlockSpec((d, ROUTER_PAD), lambda i: (0, 0)),
            pl.BlockSpec((d, ROUTER_PAD), lambda i: (0, 0)),
            pl.BlockSpec((1, ROUTER_PAD), lambda i: (0, 0)),
        ],
        out_specs=(pl.BlockSpec((tm, d // 2), lambda i: (i, 0)),
                   pl.BlockSpec((tm, ROUTER_PAD), lambda i: (i, 0)),
                   pl.BlockSpec((1, ROUTER_PAD), lambda i: (0, 0))),
        compiler_params=_cparams("arbitrary"),
        name="norm_router",
    )(xs, gain.reshape(1, d), mod4, mod4, rw_hi, rw_lo, rb)


def _prep_kernel(zk_ref, zv_ref, zl_ref, zr_ref, pk_ref, pv_ref, pr_ref, nk_ref, nv_ref, nr_ref,
                 conv_ref, w0_ref, wl_ref, a0_ref, al_ref, kk_ref, ka_ref, hsum_ref,
                 v_out, r_out, kkn_out, lw_out, kd_out, bb_out, *, t, lat_tiles, lat_per, ctx_per):
    i = pl.program_id(0)
    is_lat = i < lat_tiles
    per = jnp.where(is_lat, lat_per, ctx_per)
    j = jnp.where(is_lat, i, i - lat_tiles) % per
    first = j == 0
    last = j == per - 1
    row = lax.broadcasted_iota(jnp.int32, (t, 1), 0)

    def conv(z_ref, p_ref, n_ref, which):
        z = z_ref[...].astype(F32)
        prev_row = p_ref[...].astype(F32)[BF16_SUBLANES - 1:BF16_SUBLANES, :]
        next_row = n_ref[...].astype(F32)[0:1, :]
        prev_row = jnp.where(first, 0.0, prev_row)
        next_row = jnp.where(last, 0.0, next_row)
        zm = jnp.where(row == 0, prev_row, pltpu.roll(z, 1, 0))
        zp = jnp.where(row == t - 1, next_row, pltpu.roll(z, t - 1, 0))
        w = conv_ref[which]
        return zm * w[0:1] + z * w[1:2] + zp * w[2:3]

    k = conv(zk_ref, pk_ref, nk_ref, 0)
    v = conv(zv_ref, pv_ref, nv_ref, 1)
    r = conv(zr_ref, pr_ref, nr_ref, 2)
    v_out[...] = v.astype(BF16)
    r_out[...] = r.astype(BF16)
    kkv = k * kk_ref[...]
    ss = _split_dot(kkv * kkv, hsum_ref[...])
    kkn = kkv * lax.rsqrt(jnp.maximum(ss, 1e-24))
    kkn_out[...] = kkn.astype(BF16)
    zl = zl_ref[...]
    for d in range(2):
        wd = zl[:, d * LORA_PAD:(d + 1) * LORA_PAD].astype(F32)
        w_raw = w0_ref[d] + _dot(jnp.tanh(wd).astype(BF16), wl_ref[d])
        lw_out[d] = -_sigmoid(w_raw) * float(np.exp(-0.5))
        ad = zl[:, (2 + d) * LORA_PAD:(3 + d) * LORA_PAD]
        a = _sigmoid(a0_ref[d] + _dot(ad, al_ref[d]))
        kd_out[d] = (k * (1.0 + (a - 1.0) * ka_ref[...])).astype(BF16)
        bb_out[d] = (kkn * a).astype(BF16)


def _rwkv_prepare(z, p, geo):
    rows = z.shape[0]
    t = geo.tp
    base = geo.col0
    hb = t // BF16_SUBLANES
    n_halo = rows // BF16_SUBLANES

    def zcol(off):
        cb = (base + off) // D_RWKV
        return pl.BlockSpec((t, D_RWKV), lambda i: (i, cb))

    def prev(off):
        cb = (base + off) // D_RWKV
        return pl.BlockSpec((BF16_SUBLANES, D_RWKV), lambda i: (jnp.maximum(i * hb - 1, 0), cb))

    def nxt(off):
        cb = (base + off) // D_RWKV
        return pl.BlockSpec((BF16_SUBLANES, D_RWKV),
                            lambda i: (jnp.minimum((i + 1) * hb, n_halo - 1), cb))

    def full(a):
        nd = a.ndim
        return pl.BlockSpec(a.shape, lambda i: (0,) * nd)

    params = (p['conv'], p['w0'], p['w_lora'], p['a0'], p['a_lora'], p['k_k'], p['k_a'], geo.hsum)
    row_spec = pl.BlockSpec((t, D_RWKV), lambda i: (i, 0))
    dir_spec = pl.BlockSpec((2, t, D_RWKV), lambda i: (0, i, 0))
    kern = functools.partial(_prep_kernel, t=t, lat_tiles=geo.r_lat // t, lat_per=geo.s // t,
                             ctx_per=geo.lc // t)
    return pl.pallas_call(
        kern,
        out_shape=(jax.ShapeDtypeStruct((rows, D_RWKV), BF16),
                   jax.ShapeDtypeStruct((rows, D_RWKV), BF16),
                   jax.ShapeDtypeStruct((rows, D_RWKV), BF16),
                   jax.ShapeDtypeStruct((2, rows, D_RWKV), F32),
                   jax.ShapeDtypeStruct((2, rows, D_RWKV), BF16),
                   jax.ShapeDtypeStruct((2, rows, D_RWKV), BF16)),
        grid=(rows // t,),
        in_specs=[zcol(OFF_K), zcol(OFF_V), zcol(OFF_LORA), zcol(OFF_R),
                  prev(OFF_K), prev(OFF_V), prev(OFF_R), nxt(OFF_K), nxt(OFF_V), nxt(OFF_R)]
                 + [full(a) for a in params],
        out_specs=(row_spec, row_spec, row_spec, dir_spec, dir_spec, dir_spec),
        compiler_params=_cparams("parallel"),
        name="rwkv_prepare",
    )(z, z, z, z, z, z, z, z, z, z, *params)


def _wkv_kernel(vf_ref, rf_ref, kf_ref, lwf_ref, kdf_ref, bbf_ref,
                vb_ref, rb_ref, kb_ref, lwb_ref, kdb_ref, bbb_ref, yf_ref, yb_ref, s_ref, *, c, n_sub):
    @pl.when(pl.program_id(1) == 0)
    def _():
        s_ref[...] = jnp.zeros_like(s_ref)

    c2 = 2 * c
    n_pairs = RWKV_HEADS // 2
    ti = lax.broadcasted_iota(jnp.int32, (c, c), 0)
    ii = lax.broadcasted_iota(jnp.int32, (c, c), 1)
    rt = lax.broadcasted_iota(jnp.int32, (c2, c2), 0)
    ci = lax.broadcasted_iota(jnp.int32, (c2, c2), 1)
    same = (rt // c) == (ci // c)
    tri, strict, incl = [], [], []
    for sign in (1, -1):
        tri.append(jnp.where((ti - ii) * sign >= 0, 1.0, 0.0).astype(BF16))
        before = (rt % c - ci % c) * sign
        strict.append(same & (before > 0))
        incl.append(same & (before >= 0))
    head0 = lax.broadcasted_iota(jnp.int32, (c, PAIR), 1) < RWKV_HEAD_DIM
    eye = (lax.broadcasted_iota(jnp.int32, (PAIR, PAIR), 0)
           == lax.broadcasted_iota(jnp.int32, (PAIR, PAIR), 1)).astype(F32)

    def stack(x):
        return jnp.concatenate([jnp.where(head0, x, 0.0), jnp.where(head0, 0.0, x)], axis=0).astype(BF16)

    refs = ((vf_ref, rf_ref, kf_ref, lwf_ref, kdf_ref, bbf_ref, yf_ref),
            (vb_ref, rb_ref, kb_ref, lwb_ref, kdb_ref, bbb_ref, yb_ref))
    chains = [(u, d, p) for u in range(n_sub) for d in range(2) for p in range(n_pairs)]
    rws = [slice(u * c, (u + 1) * c) if d == 0 else slice((n_sub - 1 - u) * c, (n_sub - u) * c)
           for u, d, _ in chains]
    sls = [slice(p * PAIR, (p + 1) * PAIR) for _, _, p in chains]
    dirs = [d for _, d, _ in chains]
    n = len(chains)

    lws = [refs[d][3][rw, sl] for d, rw, sl in zip(dirs, rws, sls)]
    cums = []
    for d, lw in zip(dirs, lws):
        hi = lw.astype(BF16)
        lo = (lw - hi.astype(F32)).astype(BF16)
        cums.append(_dot(tri[d], hi) + _dot(tri[d], lo))
    g_end = [jnp.exp(jnp.sum(lw, axis=0, keepdims=True)) for lw in lws]
    a_s, b_s, k_s, r_s, v_s = [], [], [], [], []
    for d, rw, sl, lw, cum in zip(dirs, rws, sls, lws, cums):
        v_ref, r_ref, kk_ref, _, kd_ref, bb_ref, _ = refs[d]
        g_inv = jnp.exp(-cum)
        a_s.append(stack(-kk_ref[rw, sl].astype(F32) * jnp.exp(cum - lw)))
        b_s.append(stack(bb_ref[rw, sl].astype(F32) * g_inv))
        k_s.append(stack(kd_ref[rw, sl].astype(F32) * g_inv))
        r_s.append(stack(r_ref[rw, sl].astype(F32) * jnp.exp(cum)))
        v_s.append(stack(v_ref[rw, sl].astype(F32)))
    gs = [_dot_nt(jnp.concatenate([a_s[i], r_s[i]], axis=0), jnp.concatenate([b_s[i], k_s[i]], axis=0))
          for i in range(n)]
    l_ab = [jnp.where(strict[d], g[:c2, :c2], 0.0).astype(BF16) for d, g in zip(dirs, gs)]
    l_ak = [jnp.where(strict[d], g[:c2, c2:], 0.0).astype(BF16) for d, g in zip(dirs, gs)]
    m_rb = [jnp.where(incl[d], g[c2:, :c2], 0.0).astype(BF16) for d, g in zip(dirs, gs)]
    m_rk = [jnp.where(incl[d], g[c2:, c2:], 0.0).astype(BF16) for d, g in zip(dirs, gs)]

    eye2 = (rt == ci).astype(F32)
    ts = [eye2 + l.astype(F32) for l in l_ab]
    lps = l_ab
    for _ in range(int(np.log2(c)) - 1):
        lps = [_dot(lp, lp).astype(BF16) for lp in lps]
        ts = [tm + _dot(lp, tm.astype(BF16)) for tm, lp in zip(ts, lps)]
    xbs = [_dot(ts[i].astype(BF16),
                jnp.concatenate([a_s[i], _dot(l_ak[i], v_s[i]).astype(BF16)], axis=1)).astype(BF16)
           for i in range(n)]
    mxs = [_dot(m_rb[i], xbs[i]) for i in range(n)]
    gqs = [(r_s[i].astype(F32) + mxs[i][:, :PAIR]).astype(BF16) for i in range(n)]
    yqs = [mxs[i][:, PAIR:] + _dot(m_rk[i], v_s[i]) for i in range(n)]
    wbs = [_dot_tn(xbs[i], b_s[i]) for i in range(n)]
    vks = [_dot_tn(v_s[i], k_s[i]) for i in range(n)]
    p_mats = [((eye + wbs[i][:PAIR]) * g_end[i]).astype(BF16) for i in range(n)]
    q_mats = [(wbs[i][PAIR:] + vks[i]) * g_end[i] for i in range(n)]
    for i, ((_, d, p), rw, sl) in enumerate(zip(chains, rws, sls)):
        s0b = s_ref[d * n_pairs + p].astype(BF16)
        ys = _dot_nt(gqs[i], s0b) + yqs[i]
        refs[d][6][rw, sl] = ys[:c] + ys[c:]
        s_ref[d * n_pairs + p] = _dot(s0b, p_mats[i]) + q_mats[i]


def _wkv_scan(v, r, kkn, lw, kd, bb, geo):
    rows = v.shape[0]
    c = geo.chunk
    n_sub = max(m for m in (1, 2, 4) if (geo.lc // c) % m == 0 and (geo.s // c) % m == 0)
    cb = n_sub * c
    n_ctx, n_lat = geo.lc // cb, geo.s // cb
    lat_blocks = geo.r_lat // cb

    def rowblk(d):
        def f(b, s):
            ctx_j = s if d == 0 else n_ctx - 1 - s
            lat_j = s - n_ctx if d == 0 else n_lat - 1 - (s - n_ctx)
            return jnp.where(s < n_ctx, lat_blocks + b * n_ctx + ctx_j, b * n_lat + lat_j)
        return f

    def shared(d):
        f = rowblk(d)
        return pl.BlockSpec((cb, D_RWKV), lambda b, s: (f(b, s), 0))

    def perdir(d):
        f = rowblk(d)
        return pl.BlockSpec((None, cb, D_RWKV), lambda b, s: (d, f(b, s), 0))

    in_specs = []
    for d in range(2):
        in_specs += [shared(d), shared(d), shared(d), perdir(d), perdir(d), perdir(d)]
    return pl.pallas_call(
        functools.partial(_wkv_kernel, c=c, n_sub=n_sub),
        out_shape=(jax.ShapeDtypeStruct((rows, D_RWKV), F32),) * 2,
        grid=(geo.b, n_ctx + n_lat),
        in_specs=in_specs,
        out_specs=(shared(0), shared(1)),
        scratch_shapes=[pltpu.VMEM((RWKV_HEADS, PAIR, PAIR), F32)],
        compiler_params=_cparams("parallel", "arbitrary"),
        name="wkv_scan",
    )(v, r, kkn, lw, kd, bb, v, r, kkn, lw, kd, bb)


def _rwkv_out_kernel(yf_ref, yb_ref, r_ref, v_ref, kd_ref, gd_ref, lg_ref, lb_ref, rk_ref, gl_ref,
                     hsum_ref, o_ref):
    hsum = hsum_ref[...]
    inv_n = 1.0 / RWKV_HEAD_DIM
    y = yf_ref[...] + yb_ref[...]
    mu = _split_dot(y, hsum) * inv_n
    dlt = y - mu
    var = _split_dot(dlt * dlt, hsum) * inv_n
    yn = dlt * lax.rsqrt(var + GN_EPS) * lg_ref[...] + lb_ref[...]
    k_bonus = 0.5 * (kd_ref[0].astype(F32) + kd_ref[1].astype(F32))
    rsum = _split_dot(r_ref[...].astype(F32) * k_bonus * rk_ref[...], hsum)
    bonus = rsum * v_ref[...].astype(F32)
    gate = _dot(_sigmoid(gd_ref[...].astype(F32)).astype(BF16), gl_ref[...])
    o_ref[...] = ((yn + bonus) * gate).astype(BF16)


def _rwkv_output(yf, yb, r, v, kd, z, p, geo):
    rows = r.shape[0]
    t = geo.tp
    gcb = (geo.col0 + OFF_GD) // GATE_LORA
    row_spec = pl.BlockSpec((t, D_RWKV), lambda i: (i, 0))
    dir_spec = pl.BlockSpec((2, t, D_RWKV), lambda i: (0, i, 0))
    vec = pl.BlockSpec((1, D_RWKV), lambda i: (0, 0))
    return pl.pallas_call(
        _rwkv_out_kernel,
        out_shape=jax.ShapeDtypeStruct((rows, D_RWKV), BF16),
        grid=(rows // t,),
        in_specs=[row_spec, row_spec, row_spec, row_spec, dir_spec,
                  pl.BlockSpec((t, GATE_LORA), lambda i: (i, gcb)),
                  vec, vec, vec,
                  pl.BlockSpec((GATE_LORA, D_RWKV), lambda i: (0, 0)),
                  pl.BlockSpec((D_RWKV, D_RWKV), lambda i: (0, 0))],
        out_specs=row_spec,
        compiler_params=_cparams("parallel"),
        name="rwkv_output",
    )(yf, yb, r, v, kd, z, p['lnx_g'], p['lnx_b'], p['r_k'], p['g_lora'], geo.hsum)


def _attn_prep_kernel(q_ref, k_ref, cos_ref, sin_ref, qg_ref, kg_ref, qo_ref, ko_ref):
    cos = cos_ref[...]
    sin = sin_ref[...]
    lane = lax.broadcasted_iota(jnp.int32, cos.shape, 1)
    low_half = (lane % (2 * AXIS_FREQS)) < AXIS_FREQS

    def norm_rope(x, g, scale):
        x = x.astype(F32)
        xn = x * lax.rsqrt(jnp.mean(x * x, axis=-1, keepdims=True) + NORM_EPS) * g
        partner = jnp.where(low_half, pltpu.roll(xn, ATTN_HEAD_DIM - AXIS_FREQS, 1),
                            pltpu.roll(xn, AXIS_FREQS, 1))
        return ((xn * cos + partner * sin) * scale).astype(BF16)

    for h in range(ATTN_HEADS):
        sl = slice(h * ATTN_HEAD_DIM, (h + 1) * ATTN_HEAD_DIM)
        qo_ref[:, sl] = norm_rope(q_ref[:, sl], qg_ref[...], ATTN_SCALE)
    for h in range(ATTN_KV_HEADS):
        sl = slice(h * ATTN_HEAD_DIM, (h + 1) * ATTN_HEAD_DIM)
        ko_ref[:, sl] = norm_rope(k_ref[:, sl], kg_ref[...], 1.0)


def _attn_prep(z, cos_t, sin_t, p, geo):
    rows = z.shape[0]
    t = geo.tp
    qcb = (geo.col0 + OFF_Q) // D_ATTN
    kcb = (geo.col0 + OFF_KA) // D_ATTN_KV
    lat_tiles, lat_per = geo.r_lat // t, geo.s // t

    def tab(i):
        return (jnp.where(i < lat_tiles, i % lat_per, lat_per), 0)

    vec = pl.BlockSpec((1, ATTN_HEAD_DIM), lambda i: (0, 0))
    return pl.pallas_call(
        _attn_prep_kernel,
        out_shape=(jax.ShapeDtypeStruct((rows, D_ATTN), BF16),
                   jax.ShapeDtypeStruct((rows, D_ATTN_KV), BF16)),
        grid=(rows // t,),
        in_specs=[pl.BlockSpec((t, D_ATTN), lambda i: (i, qcb)),
                  pl.BlockSpec((t, D_ATTN_KV), lambda i: (i, kcb)),
                  pl.BlockSpec((t, ATTN_HEAD_DIM), tab),
                  pl.BlockSpec((t, ATTN_HEAD_DIM), tab),
                  vec, vec],
        out_specs=(pl.BlockSpec((t, D_ATTN), lambda i: (i, 0)),
                   pl.BlockSpec((t, D_ATTN_KV), lambda i: (i, 0))),
        compiler_params=_cparams("parallel"),
        name="attn_prep",
    )(z, z, cos_t, sin_t, p['q_norm_g'], p['k_norm_g'])


def _attn_kernel(*refs, n_seg):
    q_ref = refs[0]
    k_refs = refs[1:1 + n_seg]
    v_refs = refs[1 + n_seg:1 + 2 * n_seg]
    o_ref = refs[1 + 2 * n_seg]
    def qk(g):
        q = q_ref[:, g * ATTN_HEAD_DIM:(g + 1) * ATTN_HEAD_DIM]
        return [_dot_nt(q, k_ref[...]) for k_ref in k_refs]

    nxt = qk(0)
    for g in range(GQA_GROUP):
        sl = slice(g * ATTN_HEAD_DIM, (g + 1) * ATTN_HEAD_DIM)
        scores = nxt
        if g + 1 < GQA_GROUP:
            nxt = qk(g + 1)
        m = scores[0].max(axis=-1, keepdims=True)
        for s in scores[1:]:
            m = jnp.maximum(m, s.max(axis=-1, keepdims=True))
        denom = None
        acc = None
        for s, v_ref in zip(scores, v_refs):
            e = jnp.exp(s - m)
            es = e.sum(axis=-1, keepdims=True)
            pv = _dot(e.astype(BF16), v_ref[...])
            denom = es if denom is None else denom + es
            acc = pv if acc is None else acc + pv
        o_ref[:, sl] = (acc / denom).astype(BF16)


def _attention(qn, kn, z, geo, latent):
    gw = GQA_GROUP * ATTN_HEAD_DIM
    vcb = (geo.col0 + OFF_VA) // ATTN_HEAD_DIM
    ctx_blk0 = geo.r_lat // geo.lc
    k_ctx = pl.BlockSpec((geo.lc, ATTN_HEAD_DIM), lambda b, h, i: (ctx_blk0 + b, h))
    v_ctx = pl.BlockSpec((geo.lc, ATTN_HEAD_DIM), lambda b, h, i: (ctx_blk0 + b, vcb + h))
    if latent:
        tq = geo.tq
        per = geo.s // tq
        q_spec = pl.BlockSpec((tq, gw), lambda b, h, i: (b * per + i, h))
        k_lat = pl.BlockSpec((geo.s, ATTN_HEAD_DIM), lambda b, h, i: (b, h))
        v_lat = pl.BlockSpec((geo.s, ATTN_HEAD_DIM), lambda b, h, i: (b, vcb + h))
        in_specs = [q_spec, k_ctx, k_lat, v_ctx, v_lat]
        args = (qn, kn, kn, z, z)
        n_seg, out_rows = 2, geo.r_lat
        o_spec = q_spec
    else:
        tq, per = geo.lc, 1
        q_spec = pl.BlockSpec((tq, gw), lambda b, h, i: (ctx_blk0 + b, h))
        in_specs = [q_spec, k_ctx, v_ctx]
        args = (qn, kn, z)
        n_seg, out_rows = 1, geo.r_ctx
        o_spec = pl.BlockSpec((tq, gw), lambda b, h, i: (b, h))
    return pl.pallas_call(
        functools.partial(_attn_kernel, n_seg=n_seg),
        out_shape=jax.ShapeDtypeStruct((out_rows, D_ATTN), BF16),
        grid=(geo.b, ATTN_KV_HEADS, per),
        in_specs=in_specs,
        out_specs=o_spec,
        compiler_params=_cparams("parallel", "parallel", "parallel"),
        name="attention_lat" if latent else "attention_ctx",
    )(*args)


def _mm_kernel(x_ref, w_ref, o_ref):
    o_ref[...] = _dot(x_ref[...], w_ref[...]).astype(o_ref.dtype)


def _fourier_channels(z, wcs, geo):
    rows = z.shape[0]
    t = geo.tp
    fcb = (geo.col0 + OFF_FOUR) // D_FOURIER
    return pl.pallas_call(
        _mm_kernel,
        out_shape=jax.ShapeDtypeStruct((rows, 2 * D_FOURIER), BF16),
        grid=(rows // t,),
        in_specs=[pl.BlockSpec((t, D_FOURIER), lambda i: (i, fcb)),
                  pl.BlockSpec((D_FOURIER, 2 * D_FOURIER), lambda i: (0, 0))],
        out_specs=pl.BlockSpec((t, 2 * D_FOURIER), lambda i: (i, 0)),
        compiler_params=_cparams("parallel"),
        name="fourier_channels",
    )(z, wcs)


def _dft_kernel(cl_ref, sl_ref, xc_ref, xs_ref, o_ref):
    o_ref[...] = (_dot(cl_ref[...], xc_ref[...]) - _dot(sl_ref[...], xs_ref[...])).astype(BF16)


def _fourier_positions(xcs, cl, sl, n, blk0, nb, tmf):
    per = n // tmf
    return pl.pallas_call(
        _dft_kernel,
        out_shape=jax.ShapeDtypeStruct((nb * n, D_FOURIER), BF16),
        grid=(per, nb),
        in_specs=[pl.BlockSpec((tmf, n), lambda i, b: (i, 0)),
                  pl.BlockSpec((tmf, n), lambda i, b: (i, 0)),
                  pl.BlockSpec((n, D_FOURIER), lambda i, b: (blk0 + b, 0)),
                  pl.BlockSpec((n, D_FOURIER), lambda i, b: (blk0 + b, 1))],
        out_specs=pl.BlockSpec((tmf, D_FOURIER), lambda i, b: (b * per + i, 0)),
        compiler_params=_cparams("parallel", "parallel"),
        name="fourier_positions",
    )(cl, sl, xcs, xcs)


def _merge_kernel(rw_ref, fo_ref, at_ref, ga_ref, gf_ref, gc_ref, wr_ref, wf_ref, wa_ref, o_ref):
    y = _sigmoid(ga_ref[...].astype(F32)) * _dot(rw_ref[...], wr_ref[...])
    y += _sigmoid(gf_ref[...].astype(F32)) * _dot(fo_ref[...], wf_ref[...])
    y += _sigmoid(gc_ref[...].astype(F32)) * _dot(at_ref[...], wa_ref[...])
    o_ref[...] = y.astype(BF16)


def _merge(rw, fo, at, z, p, geo, n_rows):
    d = geo.d
    tm = geo.tm
    tn = min(1024, d)
    nj = d // tn
    return pl.pallas_call(
        _merge_kernel,
        out_shape=jax.ShapeDtypeStruct((n_rows, d), BF16),
        grid=(nj, n_rows // tm),
        in_specs=[pl.BlockSpec((tm, D_RWKV), lambda j, i: (i, 0)),
                  pl.BlockSpec((tm, D_FOURIER), lambda j, i: (i, 0)),
                  pl.BlockSpec((tm, D_ATTN), lambda j, i: (i, 0)),
                  pl.BlockSpec((tm, tn), lambda j, i: (i, j)),
                  pl.BlockSpec((tm, tn), lambda j, i: (i, nj + j)),
                  pl.BlockSpec((tm, tn), lambda j, i: (i, 2 * nj + j)),
                  pl.BlockSpec((D_RWKV, tn), lambda j, i: (0, j)),
                  pl.BlockSpec((D_FOURIER, tn), lambda j, i: (0, j)),
                  pl.BlockSpec((D_ATTN, tn), lambda j, i: (0, j))],
        out_specs=pl.BlockSpec((tm, tn), lambda j, i: (i, j)),
        compiler_params=_cparams("parallel", "parallel"),
        name="merge",
    )(rw, fo, at, z, z, z, p['w_br_rwkv'], p['w_br_fourier'], p['w_br_attn'])


def _outres_kernel(y_ref, w_ref, x_ref, gt_ref, o_ref):
    o_ref[...] = x_ref[...] + gt_ref[...] * _dot(y_ref[...], w_ref[...])


def _out_proj_residual(y, w, xs, mod4, geo, n_rows):
    d = geo.d
    tm = geo.tm
    tn = min(1024, d)
    bidx = geo.batch_of_tile(tm)
    return pl.pallas_call(
        _outres_kernel,
        out_shape=jax.ShapeDtypeStruct((n_rows, d), F32),
        grid=(d // tn, n_rows // tm),
        in_specs=[pl.BlockSpec((tm, d), lambda j, i: (i, 0)),
                  pl.BlockSpec((d, tn), lambda j, i: (0, j)),
                  pl.BlockSpec((tm, tn), lambda j, i: (i, j)),
                  pl.BlockSpec((None, None, 1, tn), lambda j, i: (bidx(i), 2, 0, j))],
        out_specs=pl.BlockSpec((tm, tn), lambda j, i: (i, j)),
        compiler_params=_cparams("parallel", "parallel"),
        name="out_proj_residual",
    )(y, w, xs, mod4)


def _expert_ffn(h_lo, h_hi, wgu_ref, bgu_ref, wd_ref, bd_ref, f):
    half = wgu_ref.shape[0] // 2
    gu = _dot(h_lo, wgu_ref[:half, :]) + _dot(h_hi, wgu_ref[half:, :]) + bgu_ref[...]
    gate = jnp.minimum(gu[:, :f], SWIGLU_LIMIT)
    up = jnp.clip(gu[:, f:], -SWIGLU_LIMIT, SWIGLU_LIMIT)
    act = (up + 1.0) * gate * _sigmoid(SWIGLU_ALPHA * gate)
    return _dot(act.astype(BF16), wd_ref[...]) + bd_ref[...]


def _route(sel, cnt, n, t, n_exp):
    p = TOP_K * n
    max_used = p // t + n_exp
    assert max_used % 2 == 0
    n_steps = max_used // 2 + 1
    n_tiles = 2 * n_steps
    unused_key = 2 * n_exp
    counts = cnt[0, :n_exp].astype(jnp.int32)
    ptiles = (counts + t - 1) // t
    n_used = jnp.sum(ptiles)
    pad = ptiles * t - counts
    experts = jnp.arange(n_exp, dtype=jnp.int32)
    last_e = jnp.max(jnp.where(counts > 0, experts, 0))

    e_real = sel[:, :TOP_K].astype(jnp.int32).reshape(p)
    w_real = sel[:, TOP_K:2 * TOP_K].reshape(p)
    pair = jnp.arange(p, dtype=jnp.int32)
    e_cand = jnp.repeat(experts, t)
    c_cand = jnp.tile(jnp.arange(t, dtype=jnp.int32), n_exp)
    key_cand = jnp.where(c_cand < jnp.repeat(pad, t), 2 * e_cand + 1, unused_key)
    n_extra = n_tiles * t - p - n_exp * t
    dest_bits = int(p + (n_exp + 2) * t - 1).bit_length()
    assert (unused_key + 1) << dest_bits < 2 ** 31
    words = jnp.concatenate([
        ((2 * e_real) << dest_bits) | ((pair % TOP_K) * n + pair // TOP_K),
        (key_cand << dest_bits) | (p + e_cand * t + c_cand),
        jnp.full((n_extra,), unused_key << dest_bits, jnp.int32)])
    ws = jnp.concatenate([w_real, jnp.zeros((n_exp * t + n_extra,), F32)])
    word_s, w_s = lax.sort((words, ws), num_keys=1, is_stable=True)

    lane = jnp.arange(t, dtype=jnp.int32)[None, :]
    spare = p + n_exp * t + lane
    word_s = word_s.reshape(n_tiles, t)
    key_s, dest_s = word_s >> dest_bits, word_s & ((1 << dest_bits) - 1)
    dest = jnp.where(key_s == unused_key, spare, dest_s)
    tok_s = jnp.where(key_s % 2 == 0, dest_s % n, 0)
    tok_s = jnp.where(key_s == unused_key, 0, tok_s)
    te = jnp.where(jnp.arange(n_tiles) < n_used, jnp.minimum(key_s[:, 0] // 2, n_exp - 1), last_e)
    dest_prev = jnp.concatenate([spare, dest[1:-1:2]], axis=0)
    return (te.astype(jnp.int32), n_used.reshape(1).astype(jnp.int32), tok_s.reshape(n_tiles, 1, t),
            dest.reshape(n_tiles, 1, t), dest_prev.reshape(n_steps, 1, t), w_s.reshape(n_tiles * t, 1))


def _moe_routed_kernel(te_ref, nu_ref, toka_ref, tokb_ref, tokn_ref, dsta_ref, dstp_ref, wa_ref, wb_ref,
                       h_hbm, wgu_a, bgu_a, wd_a, bd_a, wgu_b, bgu_b, wd_b, bd_b, o_hbm,
                       ga, gb, oa, ob, gsem, ssem, *, f, t, dump0, n_dump):
    j = pl.program_id(0)
    n_used = nu_ref[0]

    def gather(idx_ref, buf, sem):
        @pl.when(n_used > 0)
        def _():
            for r in range(t):
                pltpu.make_async_copy(h_hbm.at[pl.ds(idx_ref[0, 0, r], 1)], buf.at[pl.ds(r, 1)], sem).start()

    def scatter(buf, idx_ref, sem):
        @pl.when(n_used > 0)
        def _():
            for r in range(t):
                pltpu.make_async_copy(buf.at[pl.ds(r, 1)], o_hbm.at[pl.ds(idx_ref[0, 0, r], 1)], sem).start()

    def wait_rows_in(buf, sem):
        pltpu.make_async_copy(h_hbm.at[pl.ds(0, t)], buf, sem).wait()

    def wait_rows_out(buf, sem):
        pltpu.make_async_copy(buf, o_hbm.at[pl.ds(0, t)], sem).wait()

    def experts(buf, wgu, bgu, wd, bd, w_ref):
        lo, hi = _unpack_halves(buf[...])
        y = _expert_ffn(lo.astype(BF16), hi.astype(BF16), wgu, bgu, wd, bd, f)
        return _pack_halves(y * w_ref[...])

    @pl.when(j == 0)
    def _():
        gather(toka_ref, ga, gsem.at[0])
        ob[...] = jnp.zeros_like(ob)
        fills = [pltpu.make_async_copy(ob, o_hbm.at[pl.ds(dump0 + e * t, t)], ssem.at[1])
                 for e in range(n_dump - 1)]
        for c in fills:
            c.start()
        for c in fills:
            c.wait()
        pltpu.make_async_copy(ob, o_hbm.at[pl.ds(dump0 + (n_dump - 1) * t, t)], ssem.at[0]).start()

    @pl.when(2 * j <= n_used)
    def _():
        wait_rows_in(ga, gsem.at[0])
        scatter(ob, dstp_ref, ssem.at[1])
        gather(tokb_ref, gb, gsem.at[1])
        ya = experts(ga, wgu_a, bgu_a, wd_a, bd_a, wa_ref)
        wait_rows_out(oa, ssem.at[0])
        oa[...] = ya
        wait_rows_in(gb, gsem.at[1])
        scatter(oa, dsta_ref, ssem.at[0])
        gather(tokn_ref, ga, gsem.at[0])
        yb = experts(gb, wgu_b, bgu_b, wd_b, bd_b, wb_ref)
        wait_rows_out(ob, ssem.at[1])
        ob[...] = yb

    @pl.when(j == pl.num_programs(0) - 1)
    def _():
        wait_rows_in(ga, gsem.at[0])
        wait_rows_out(oa, ssem.at[0])


def _moe_routed(hp, sel, cnt, wgu, bgu, wd, bd, geo, n_rows):
    d = geo.d
    n_exp, _, f2 = wgu.shape
    f = f2 // 2
    t = geo.t_moe
    te, n_used, tok, dest, dest_prev, w = _route(sel, cnt, n_rows, t, n_exp)
    n_tiles, n_steps = tok.shape[0], dest_prev.shape[0]
    n_dump = n_exp + 2
    smem = functools.partial(pl.BlockSpec, (1, 1, t), memory_space=pltpu.SMEM)

    def weights(which):
        return [pl.BlockSpec((None, d, f2), lambda j, te, nu: (te[2 * j + which], 0, 0)),
                pl.BlockSpec((None, 1, f2), lambda j, te, nu: (te[2 * j + which], 0, 0)),
                pl.BlockSpec((None, f, d), lambda j, te, nu: (te[2 * j + which], 0, 0)),
                pl.BlockSpec((None, 1, d), lambda j, te, nu: (te[2 * j + which], 0, 0))]

    grid_spec = pltpu.PrefetchScalarGridSpec(
        num_scalar_prefetch=2,
        grid=(n_steps,),
        in_specs=[smem(lambda j, te, nu: (2 * j, 0, 0)),
                  smem(lambda j, te, nu: (2 * j + 1, 0, 0)),
                  smem(lambda j, te, nu: (jnp.minimum(2 * j + 2, n_tiles - 1), 0, 0)),
                  smem(lambda j, te, nu: (2 * j, 0, 0)),
                  smem(lambda j, te, nu: (j, 0, 0)),
                  pl.BlockSpec((t, 1), lambda j, te, nu: (2 * j, 0)),
                  pl.BlockSpec((t, 1), lambda j, te, nu: (2 * j + 1, 0)),
                  pl.BlockSpec(memory_space=pl.ANY)] + weights(0) + weights(1),
        out_specs=pl.BlockSpec(memory_space=pl.ANY),
        scratch_shapes=[pltpu.VMEM((t, d // 2), jnp.uint32)] * 4
                       + [pltpu.SemaphoreType.DMA((2,)), pltpu.SemaphoreType.DMA((2,))])
    return pl.pallas_call(
        functools.partial(_moe_routed_kernel, f=f, t=t, dump0=TOP_K * n_rows, n_dump=n_dump),
        out_shape=jax.ShapeDtypeStruct((TOP_K * n_rows + n_dump * t, d // 2), jnp.uint32),
        grid_spec=grid_spec,
        compiler_params=pltpu.CompilerParams(dimension_semantics=("arbitrary",),
                                             vmem_limit_bytes=VMEM_LIMIT, disable_bounds_checks=True),
        name="moe_routed",
    )(te, n_used, tok, tok, tok, dest, dest_prev, w, w, hp, wgu, bgu, wd, bd, wgu, bgu, wd, bd)


def _combine_kernel(x_ref, gt_ref, y0_ref, y1_ref, y2_ref, y3_ref, o_ref):
    half = x_ref.shape[1] // 2
    parts = [_unpack_halves(y[...]) for y in (y0_ref, y1_ref, y2_ref, y3_ref)]
    lo = (parts[0][0] + parts[1][0]) + (parts[2][0] + parts[3][0])
    hi = (parts[0][1] + parts[1][1]) + (parts[2][1] + parts[3][1])
    o_ref[:, :half] = x_ref[:, :half] + gt_ref[:, :half] * lo
    o_ref[:, half:] = x_ref[:, half:] + gt_ref[:, half:] * hi


def _moe_combine(y4, xs, mod4, geo, n_rows):
    d = geo.d
    tm = geo.t_moe
    per = n_rows // tm
    bidx = geo.batch_of_tile(tm)

    def part(k):
        return pl.BlockSpec((tm, d // 2), lambda i: (k * per + i, 0))

    return pl.pallas_call(
        _combine_kernel,
        out_shape=jax.ShapeDtypeStruct((n_rows, d), F32),
        grid=(per,),
        in_specs=[pl.BlockSpec((tm, d), lambda i: (i, 0)),
                  pl.BlockSpec((None, None, 1, d), lambda i: (bidx(i), 5, 0, 0)),
                  part(0), part(1), part(2), part(3)],
        out_specs=pl.BlockSpec((tm, d), lambda i: (i, 0)),
        compiler_params=_cparams("parallel"),
        name="moe_combine",
    )(xs, mod4, y4, y4, y4, y4)


class _Geometry:
    def __init__(self, b, s, lc, d):
        self.b, self.s, self.lc, self.d = b, s, lc, d
        self.r_lat, self.r_ctx = b * s, b * lc
        self.rows = self.r_lat + self.r_ctx
        self.col0 = 3 * d
        assert self.col0 % 1024 == 0, "gate columns must end on a 1024-column boundary"
        self.tm = min(1024, s, self.r_ctx)
        self.tp = min(256, s, lc)
        self.tq = min(512, s)
        self.chunk = min(WKV_CHUNK, lc, s)
        self.t_moe = min(256, self.tm)
        for t in (self.tm,):
            assert s % t == 0 and self.r_ctx % t == 0
        assert s % self.tp == 0 and lc % self.tp == 0 and self.tp % BF16_SUBLANES == 0
        assert s % self.chunk == 0 and lc % self.chunk == 0 and s % GRID_W == 0
        self.hsum = jnp.asarray(np.kron(np.eye(RWKV_HEADS), np.ones((RWKV_HEAD_DIM,) * 2)), BF16)

    def batch_of_tile(self, tm):
        per, nb = self.s // tm, self.b
        return lambda i: jnp.minimum(i // per, nb)


def _dft_tables(n):
    j = np.arange(n, dtype=np.int64)
    ang = 2.0 * np.pi * ((j[:, None] * j[None, :]) % n).astype(np.float64) / n
    return np.cos(ang) / np.sqrt(n), np.sin(ang) / np.sqrt(n)


def _rope_tables(s, t):
    pos = np.arange(s)
    inv_freq = ROPE_THETA ** (-np.arange(AXIS_FREQS, dtype=np.float32) / AXIS_FREQS)
    ang_r = (pos // GRID_W).astype(np.float32)[:, None] * inv_freq.astype(np.float32)
    ang_c = (pos % GRID_W).astype(np.float32)[:, None] * inv_freq.astype(np.float32)
    cr, sr, cc, sc = np.cos(ang_r), np.sin(ang_r), np.cos(ang_c), np.sin(ang_c)
    cos = np.concatenate([cr, cr, cc, cc], axis=1)
    sin = np.concatenate([-sr, sr, -sc, sc], axis=1)
    cos = np.concatenate([cos, np.ones((t, ATTN_HEAD_DIM))], axis=0)
    sin = np.concatenate([sin, np.zeros((t, ATTN_HEAD_DIM))], axis=0)
    return jnp.asarray(cos, F32), jnp.asarray(sin, F32)


def _permute_w_in(w_in, d):
    o = np.cumsum([0, D_RWKV, D_RWKV, LORA, LORA, LORA, LORA, D_ATTN_KV, D_ATTN_KV, D_RWKV, GATE_LORA,
                   D_ATTN, D_FOURIER, 3 * d])
    w_in = w_in.astype(BF16)
    seg = lambda i: w_in[..., o[i]:o[i + 1]]
    pad = lambda a: jnp.pad(a, ((0, 0), (0, 0), (0, LORA_PAD - LORA)))
    parts = [seg(12), seg(0), seg(1), pad(seg(2)), pad(seg(3)), pad(seg(4)), pad(seg(5)), seg(8), seg(10),
             seg(11), seg(6), seg(7), seg(9)]
    w = jnp.concatenate(parts, axis=-1)
    n = w.shape[-1]
    n_pad = -(-n // W_IN_TN) * W_IN_TN
    return jnp.pad(w, ((0, 0), (0, 0), (0, n_pad - n)))


def _split_hi_lo(w):
    hi = w.astype(BF16)
    return hi, (w - hi.astype(F32)).astype(BF16)


def kernel(x, c, ctx, c_ctx, ada_w, ada_b, norm1_g, norm2_g, w_in, rwkv_conv, w0, w_lora, a0, a_lora, g_lora, k_k, k_a, r_k, lnx_g, lnx_b, q_norm_g, k_norm_g, w_br_rwkv, w_br_fourier, w_br_attn, w_out, router_w, router_b, exp_w_gu, exp_b_gu, exp_w_down, exp_b_down):
    b, s, d = x.shape
    lc = ctx.shape[1]
    depth = w_in.shape[0]
    n_exp = router_w.shape[-1]
    geo = _Geometry(b, s, lc, d)

    w_in_p = _permute_w_in(w_in, d)
    lora_pad = ((0, 0), (0, 0), (0, LORA_PAD - LORA), (0, 0))
    w_lora_p = jnp.pad(w_lora, lora_pad).astype(BF16)
    a_lora_p = jnp.pad(a_lora, lora_pad).astype(BF16)
    g_lora_b = g_lora.astype(BF16)
    wbr_r, wbr_f, wbr_a, w_out_b = (w.astype(BF16) for w in (w_br_rwkv, w_br_fourier, w_br_attn, w_out))
    rw_p = jnp.pad(router_w, ((0, 0), (0, 0), (0, ROUTER_PAD - n_exp)))
    rw_hi, rw_lo = _split_hi_lo(rw_p)
    rb_p = jnp.pad(router_b, ((0, 0), (0, ROUTER_PAD - n_exp)), constant_values=NEG_BIG)
    wgu_b, wd_b = exp_w_gu.astype(BF16), exp_w_down.astype(BF16)
    cos_t, sin_t = _rope_tables(s, geo.tp)
    cc, sc = _dft_tables(FOURIER_GROUP_DIM)
    groups = D_FOURIER // FOURIER_GROUP_DIM
    wcs = jnp.asarray(np.concatenate([np.kron(np.eye(groups), cc), np.kron(np.eye(groups), sc)], axis=1), BF16)
    cl_lat, sl_lat = (jnp.asarray(m, BF16) for m in _dft_tables(s))
    cl_ctx, sl_ctx = (jnp.asarray(m, BF16) for m in _dft_tables(lc))

    mod_rows = -(-(b + 1) // BF16_SUBLANES) * BF16_SUBLANES
    c_all = jnp.concatenate([c, c_ctx[None], jnp.zeros((mod_rows - b - 1, d), F32)], axis=0)
    mod = _adaln(c_all, ada_w, ada_b).reshape(depth, mod_rows, 6, 1, d)

    xs = jnp.concatenate([x.reshape(b * s, d), ctx.reshape(b * lc, d)], axis=0)
    for l in range(depth):
        last = l == depth - 1
        mod4 = mod[l]
        p = dict(conv=rwkv_conv[l], w0=w0[l][:, None, :], w_lora=w_lora_p[l], a0=a0[l][:, None, :],
                 a_lora=a_lora_p[l], k_k=k_k[l][None], k_a=k_a[l][None], r_k=r_k[l][None],
                 lnx_g=lnx_g[l][None], lnx_b=lnx_b[l][None], g_lora=g_lora_b[l],
                 q_norm_g=q_norm_g[l][None], k_norm_g=k_norm_g[l][None],
                 w_br_rwkv=wbr_r[l], w_br_fourier=wbr_f[l], w_br_attn=wbr_a[l])
        n_rows = geo.r_lat if last else geo.rows

        z = _norm_in_proj(xs, norm1_g[l], mod4, w_in_p[l], geo)

        v_c, r_c, kkn, lw, kd, bb = _rwkv_prepare(z, p, geo)
        yf, yb = _wkv_scan(v_c, r_c, kkn, lw, kd, bb, geo)
        rw = _rwkv_output(yf, yb, r_c, v_c, kd, z, p, geo)

        qn, kn = _attn_prep(z, cos_t, sin_t, p, geo)
        att = _attention(qn, kn, z, geo, latent=True)

        xcs = _fourier_channels(z, wcs, geo)
        fo = _fourier_positions(xcs, cl_lat, sl_lat, s, 0, b, min(1024, s))
        if not last:
            att = jnp.concatenate([att, _attention(qn, kn, z, geo, latent=False)], axis=0)
            fo_c = _fourier_positions(xcs, cl_ctx, sl_ctx, lc, geo.r_lat // lc, b, lc)
            fo = jnp.concatenate([fo, fo_c], axis=0)

        ym = _merge(rw, fo, att, z, p, geo, n_rows)
        xs = _out_proj_residual(ym, w_out_b[l], xs, mod4, geo, n_rows)

        hp, sel, cnt = _norm_router(xs, norm2_g[l], mod4, rw_hi[l], rw_lo[l], rb_p[l][None], geo, n_rows)
        y4 = _moe_routed(hp, sel, cnt, wgu_b[l], exp_b_gu[l][:, None, :], wd_b[l], exp_b_down[l][:, None, :],
                         geo, n_rows)
        xs = _moe_combine(y4, xs, mod4, geo, n_rows)
    return xs[:geo.r_lat].reshape(b, s, d)
```

```python
import functools

import numpy as np
import jax
import jax.numpy as jnp
from jax import lax
from jax.experimental import pallas as pl
from jax.experimental.pallas import tpu as pltpu

F32 = jnp.float32
BF16 = jnp.bfloat16

NORM_EPS = 1e-6
GN_EPS = 64e-5
RWKV_HEADS = 8
RWKV_HEAD_DIM = 64
D_RWKV = RWKV_HEADS * RWKV_HEAD_DIM
LORA = 96
GATE_LORA = 256
D_FOURIER = 512
FOURIER_GROUP_DIM = 128
ATTN_HEADS = 8
ATTN_KV_HEADS = 2
ATTN_HEAD_DIM = 128
GQA_GROUP = ATTN_HEADS // ATTN_KV_HEADS
D_ATTN = ATTN_HEADS * ATTN_HEAD_DIM
D_ATTN_KV = ATTN_KV_HEADS * ATTN_HEAD_DIM
ATTN_SCALE = ATTN_HEAD_DIM ** -0.5
GRID_W = 64
ROPE_THETA = 10000.0
AXIS_FREQS = ATTN_HEAD_DIM // 4
TOP_K = 4
SWIGLU_LIMIT = 7.0
SWIGLU_ALPHA = 1.702

LANES = 128
BF16_SUBLANES = 16
VMEM_LIMIT = 56 * 1024 * 1024

LORA_PAD = LANES
W_IN_TN = 1536
WKV_CHUNK = 64
PAIR = 2 * RWKV_HEAD_DIM
ROUTER_PAD = LANES
NEG_BIG = -1e30

OFF_K, OFF_V, OFF_LORA, OFF_R, OFF_Q, OFF_FOUR, OFF_KA, OFF_VA, OFF_GD, OFF_END = (
    0, 512, 1024, 1536, 2048, 3072, 3584, 3840, 4096, 4352)


def _cparams(*sem):
    return pltpu.CompilerParams(dimension_semantics=sem, vmem_limit_bytes=VMEM_LIMIT)


def _dot(a, b):
    return jnp.dot(a, b, preferred_element_type=F32)


def _dot_nt(a, b):
    return lax.dot_general(a, b, (((1,), (1,)), ((), ())), preferred_element_type=F32)


def _dot_tn(a, b):
    return lax.dot_general(a, b, (((0,), (0,)), ((), ())), preferred_element_type=F32)


def _split_dot(x, g):
    hi = x.astype(BF16)
    lo = (x - hi.astype(F32)).astype(BF16)
    return _dot(hi, g) + _dot(lo, g)


def _sigmoid(x):
    return 1.0 / (1.0 + jnp.exp(-x))


def _adaln_kernel(c_ref, w_ref, b_ref, o_ref):
    c = c_ref[...]
    s = (c * _sigmoid(c)).astype(BF16)
    o_ref[...] = _dot(s, w_ref[...].astype(BF16)) + b_ref[...]


def _adaln(c_all, ada_w, ada_b):
    depth, d, n = ada_w.shape
    rows = c_all.shape[0]
    tn = 1024 if n % 1024 == 0 else n
    return pl.pallas_call(
        _adaln_kernel,
        out_shape=jax.ShapeDtypeStruct((depth, rows, n), F32),
        grid=(depth, n // tn),
        in_specs=[
            pl.BlockSpec((rows, d), lambda l, j: (0, 0)),
            pl.BlockSpec((None, d, tn), lambda l, j: (l, 0, j)),
            pl.BlockSpec((None, 1, tn), lambda l, j: (l, 0, j)),
        ],
        out_specs=pl.BlockSpec((None, rows, tn), lambda l, j: (l, 0, j)),
        compiler_params=_cparams("parallel", "parallel"),
        name="adaln",
    )(c_all, ada_w, ada_b.reshape(depth, 1, n))


def _modulated_norm(x, g, sc, sh):
    ms = jnp.mean(x * x, axis=-1, keepdims=True)
    return x * lax.rsqrt(ms + NORM_EPS) * g * (1.0 + sc) + sh


def _normmm_kernel(x_ref, g_ref, sc_ref, sh_ref, w_ref, o_ref, h_ref):
    @pl.when(pl.program_id(1) == 0)
    def _():
        h_ref[...] = _modulated_norm(x_ref[...], g_ref[...], sc_ref[...], sh_ref[...]).astype(BF16)

    o_ref[...] = _dot(h_ref[...], w_ref[...]).astype(o_ref.dtype)


def _norm_in_proj(xs, gain, mod4, w, geo):
    rows, d = xs.shape
    n = w.shape[1]
    tm, tn = geo.tm, W_IN_TN
    bidx = geo.batch_of_tile(tm)
    return pl.pallas_call(
        _normmm_kernel,
        out_shape=jax.ShapeDtypeStruct((rows, n), BF16),
        grid=(rows // tm, n // tn),
        in_specs=[
            pl.BlockSpec((tm, d), lambda i, j: (i, 0)),
            pl.BlockSpec((1, d), lambda i, j: (0, 0)),
            pl.BlockSpec((None, None, 1, d), lambda i, j: (bidx(i), 1, 0, 0)),
            pl.BlockSpec((None, None, 1, d), lambda i, j: (bidx(i), 0, 0, 0)),
            pl.BlockSpec((d, tn), lambda i, j: (0, j)),
        ],
        out_specs=pl.BlockSpec((tm, tn), lambda i, j: (i, j)),
        scratch_shapes=[pltpu.VMEM((tm, d), BF16)],
        compiler_params=_cparams("parallel", "arbitrary"),
        name="norm_in_proj",
    )(xs, gain.reshape(1, d), mod4, mod4, w)


def _pack_halves(x):
    half = x.shape[1] // 2
    xb = x.astype(BF16).astype(F32)
    lo = lax.shift_right_logical(lax.bitcast_convert_type(xb[:, :half], jnp.uint32), jnp.uint32(16))
    hi = lax.bitcast_convert_type(xb[:, half:], jnp.uint32) & jnp.uint32(0xFFFF0000)
    return lo | hi


def _unpack_halves(words):
    lo = lax.bitcast_convert_type(lax.shift_left(words, jnp.uint32(16)), F32)
    hi = lax.bitcast_convert_type(words & jnp.uint32(0xFFFF0000), F32)
    return lo, hi


def _norm_router_kernel(x_ref, g_ref, sc_ref, sh_ref, wh_ref, wl_ref, rb_ref, h_ref, sel_ref, cnt_ref):
    @pl.when(pl.program_id(0) == 0)
    def _():
        cnt_ref[...] = jnp.zeros_like(cnt_ref)

    h = _modulated_norm(x_ref[...], g_ref[...], sc_ref[...], sh_ref[...])
    h_ref[...] = _pack_halves(h)
    hi = h.astype(BF16)
    lo = (h - hi.astype(F32)).astype(BF16)
    logits = _dot(hi, wh_ref[...]) + _dot(lo, wh_ref[...]) + _dot(hi, wl_ref[...]) + rb_ref[...]
    lane = lax.broadcasted_iota(jnp.int32, logits.shape, 1).astype(F32)
    work = logits
    vals, firsts = [], []
    for _ in range(TOP_K):
        m = jnp.max(work, axis=-1, keepdims=True)
        first = jnp.min(jnp.where(work == m, lane, float(ROUTER_PAD)), axis=-1, keepdims=True)
        vals.append(m)
        firsts.append(first)
        work = jnp.where(lane == first, 2.0 * NEG_BIG, work)
    exps = [jnp.exp(v - vals[0]) for v in vals]
    denom = exps[0] + exps[1] + exps[2] + exps[3]
    table = jnp.zeros_like(logits)
    picked = jnp.zeros_like(logits)
    for k in range(TOP_K):
        table = jnp.where(lane == float(k), firsts[k], table)
        table = jnp.where(lane == float(TOP_K + k), exps[k] / denom, table)
        picked = picked + jnp.where(lane == firsts[k], 1.0, 0.0)
    sel_ref[...] = table
    cnt_ref[...] += jnp.sum(picked, axis=0, keepdims=True)


def _norm_router(xs, gain, mod4, rw_hi, rw_lo, rb, geo, n_rows):
    d = xs.shape[1]
    tm = geo.tm
    bidx = geo.batch_of_tile(tm)
    return pl.pallas_call(
        _norm_router_kernel,
        out_shape=(jax.ShapeDtypeStruct((n_rows, d // 2), jnp.uint32),
                   jax.ShapeDtypeStruct((n_rows, ROUTER_PAD), F32),
                   jax.ShapeDtypeStruct((1, ROUTER_PAD), F32)),
        grid=(n_rows // tm,),
        in_specs=[
            pl.BlockSpec((tm, d), lambda i: (i, 0)),
            pl.BlockSpec((1, d), lambda i: (0, 0)),
            pl.BlockSpec((None, None, 1, d), lambda i: (bidx(i), 4, 0, 0)),
            pl.BlockSpec((None, None, 1, d), lambda i: (bidx(i), 3, 0, 0)),
            pl.BlockSpec((d, ROUTER_PAD), lambda i: (0, 0)),
            pl.BlockSpec((d, ROUTER_PAD), lambda i: (0, 0)),
            pl.BlockSpec((1, ROUTER_PAD), lambda i: (0, 0)),
        ],
        out_specs=(pl.BlockSpec((tm, d // 2), lambda i: (i, 0)),
                   pl.BlockSpec((tm, ROUTER_PAD), lambda i: (i, 0)),
                   pl.BlockSpec((1, ROUTER_PAD), lambda i: (0, 0))),
        compiler_params=_cparams("arbitrary"),
        name="norm_router",
    )(xs, gain.reshape(1, d), mod4, mod4, rw_hi, rw_lo, rb)


def _prep_kernel(zk_ref, zv_ref, zl_ref, zr_ref, pk_ref, pv_ref, pr_ref, nk_ref, nv_ref, nr_ref,
                 conv_ref, w0_ref, wl_ref, a0_ref, al_ref, kk_ref, ka_ref, hsum_ref,
                 v_out, r_out, kkn_out, lw_out, kd_out, bb_out, *, t, lat_tiles, lat_per, ctx_per):
    i = pl.program_id(0)
    is_lat = i < lat_tiles
    per = jnp.where(is_lat, lat_per, ctx_per)
    j = jnp.where(is_lat, i, i - lat_tiles) % per
    first = j == 0
    last = j == per - 1
    row = lax.broadcasted_iota(jnp.int32, (t, 1), 0)

    def conv(z_ref, p_ref, n_ref, which):
        z = z_ref[...].astype(F32)
        prev_row = p_ref[...].astype(F32)[BF16_SUBLANES - 1:BF16_SUBLANES, :]
        next_row = n_ref[...].astype(F32)[0:1, :]
        prev_row = jnp.where(first, 0.0, prev_row)
        next_row = jnp.where(last, 0.0, next_row)
        zm = jnp.where(row == 0, prev_row, pltpu.roll(z, 1, 0))
        zp = jnp.where(row == t - 1, next_row, pltpu.roll(z, t - 1, 0))
        w = conv_ref[which]
        return zm * w[0:1] + z * w[1:2] + zp * w[2:3]

    k = conv(zk_ref, pk_ref, nk_ref, 0)
    v = conv(zv_ref, pv_ref, nv_ref, 1)
    r = conv(zr_ref, pr_ref, nr_ref, 2)
    v_out[...] = v.astype(BF16)
    r_out[...] = r.astype(BF16)
    kkv = k * kk_ref[...]
    ss = _split_dot(kkv * kkv, hsum_ref[...])
    kkn = kkv * lax.rsqrt(jnp.maximum(ss, 1e-24))
    kkn_out[...] = kkn.astype(BF16)
    zl = zl_ref[...]
    for d in range(2):
        wd = zl[:, d * LORA_PAD:(d + 1) * LORA_PAD].astype(F32)
        w_raw = w0_ref[d] + _dot(jnp.tanh(wd).astype(BF16), wl_ref[d])
        lw_out[d] = -_sigmoid(w_raw) * float(np.exp(-0.5))
        ad = zl[:, (2 + d) * LORA_PAD:(3 + d) * LORA_PAD]
        a = _sigmoid(a0_ref[d] + _dot(ad, al_ref[d]))
        kd_out[d] = (k * (1.0 + (a - 1.0) * ka_ref[...])).astype(BF16)
        bb_out[d] = (kkn * a).astype(BF16)


def _rwkv_prepare(z, p, geo):
    rows = z.shape[0]
    t = geo.tp
    base = geo.col0
    hb = t // BF16_SUBLANES
    n_halo = rows // BF16_SUBLANES

    def zcol(off):
        cb = (base + off) // D_RWKV
        return pl.BlockSpec((t, D_RWKV), lambda i: (i, cb))

    def prev(off):
        cb = (base + off) // D_RWKV
        return pl.BlockSpec((BF16_SUBLANES, D_RWKV), lambda i: (jnp.maximum(i * hb - 1, 0), cb))

    def nxt(off):
        cb = (base + off) // D_RWKV
        return pl.BlockSpec((BF16_SUBLANES, D_RWKV),
                            lambda i: (jnp.minimum((i + 1) * hb, n_halo - 1), cb))

    def full(a):
        nd = a.ndim
        return pl.BlockSpec(a.shape, lambda i: (0,) * nd)

    params = (p['conv'], p['w0'], p['w_lora'], p['a0'], p['a_lora'], p['k_k'], p['k_a'], geo.hsum)
    row_spec = pl.BlockSpec((t, D_RWKV), lambda i: (i, 0))
    dir_spec = pl.BlockSpec((2, t, D_RWKV), lambda i: (0, i, 0))
    kern = functools.partial(_prep_kernel, t=t, lat_tiles=geo.r_lat // t, lat_per=geo.s // t,
                             ctx_per=geo.lc // t)
    return pl.pallas_call(
        kern,
        out_shape=(jax.ShapeDtypeStruct((rows, D_RWKV), BF16),
                   jax.ShapeDtypeStruct((rows, D_RWKV), BF16),
                   jax.ShapeDtypeStruct((rows, D_RWKV), BF16),
                   jax.ShapeDtypeStruct((2, rows, D_RWKV), F32),
                   jax.ShapeDtypeStruct((2, rows, D_RWKV), BF16),
                   jax.ShapeDtypeStruct((2, rows, D_RWKV), BF16)),
        grid=(rows // t,),
        in_specs=[zcol(OFF_K), zcol(OFF_V), zcol(OFF_LORA), zcol(OFF_R),
                  prev(OFF_K), prev(OFF_V), prev(OFF_R), nxt(OFF_K), nxt(OFF_V), nxt(OFF_R)]
                 + [full(a) for a in params],
        out_specs=(row_spec, row_spec, row_spec, dir_spec, dir_spec, dir_spec),
        compiler_params=_cparams("parallel"),
        name="rwkv_prepare",
    )(z, z, z, z, z, z, z, z, z, z, *params)


def _wkv_kernel(vf_ref, rf_ref, kf_ref, lwf_ref, kdf_ref, bbf_ref,
                vb_ref, rb_ref, kb_ref, lwb_ref, kdb_ref, bbb_ref, yf_ref, yb_ref, s_ref, *, c, n_sub):
    @pl.when(pl.program_id(1) == 0)
    def _():
        s_ref[...] = jnp.zeros_like(s_ref)

    c2 = 2 * c
    n_pairs = RWKV_HEADS // 2
    ti = lax.broadcasted_iota(jnp.int32, (c, c), 0)
    ii = lax.broadcasted_iota(jnp.int32, (c, c), 1)
    rt = lax.broadcasted_iota(jnp.int32, (c2, c2), 0)
    ci = lax.broadcasted_iota(jnp.int32, (c2, c2), 1)
    same = (rt // c) == (ci // c)
    tri, strict, incl = [], [], []
    for sign in (1, -1):
        tri.append(jnp.where((ti - ii) * sign >= 0, 1.0, 0.0).astype(BF16))
        before = (rt % c - ci % c) * sign
        strict.append(same & (before > 0))
        incl.append(same & (before >= 0))
    head0 = lax.broadcasted_iota(jnp.int32, (c, PAIR), 1) < RWKV_HEAD_DIM
    eye = (lax.broadcasted_iota(jnp.int32, (PAIR, PAIR), 0)
           == lax.broadcasted_iota(jnp.int32, (PAIR, PAIR), 1)).astype(F32)

    def stack(x):
        return jnp.concatenate([jnp.where(head0, x, 0.0), jnp.where(head0, 0.0, x)], axis=0).astype(BF16)

    refs = ((vf_ref, rf_ref, kf_ref, lwf_ref, kdf_ref, bbf_ref, yf_ref),
            (vb_ref, rb_ref, kb_ref, lwb_ref, kdb_ref, bbb_ref, yb_ref))
    chains = [(u, d, p) for u in range(n_sub) for d in range(2) for p in range(n_pairs)]
    rws = [slice(u * c, (u + 1) * c) if d == 0 else slice((n_sub - 1 - u) * c, (n_sub - u) * c)
           for u, d, _ in chains]
    sls = [slice(p * PAIR, (p + 1) * PAIR) for _, _, p in chains]
    dirs = [d for _, d, _ in chains]
    n = len(chains)

    lws = [refs[d][3][rw, sl] for d, rw, sl in zip(dirs, rws, sls)]
    cums = []
    for d, lw in zip(dirs, lws):
        hi = lw.astype(BF16)
        lo = (lw - hi.astype(F32)).astype(BF16)
        cums.append(_dot(tri[d], hi) + _dot(tri[d], lo))
    g_end = [jnp.exp(jnp.sum(lw, axis=0, keepdims=True)) for lw in lws]
    a_s, b_s, k_s, r_s, v_s = [], [], [], [], []
    for d, rw, sl, lw, cum in zip(dirs, rws, sls, lws, cums):
        v_ref, r_ref, kk_ref, _, kd_ref, bb_ref, _ = refs[d]
        g_inv = jnp.exp(-cum)
        a_s.append(stack(-kk_ref[rw, sl].astype(F32) * jnp.exp(cum - lw)))
        b_s.append(stack(bb_ref[rw, sl].astype(F32) * g_inv))
        k_s.append(stack(kd_ref[rw, sl].astype(F32) * g_inv))
        r_s.append(stack(r_ref[rw, sl].astype(F32) * jnp.exp(cum)))
        v_s.append(stack(v_ref[rw, sl].astype(F32)))
    gs = [_dot_nt(jnp.concatenate([a_s[i], r_s[i]], axis=0), jnp.concatenate([b_s[i], k_s[i]], axis=0))
          for i in range(n)]
    l_ab = [jnp.where(strict[d], g[:c2, :c2], 0.0).astype(BF16) for d, g in zip(dirs, gs)]
    l_ak = [jnp.where(strict[d], g[:c2, c2:], 0.0).astype(BF16) for d, g in zip(dirs, gs)]
    m_rb = [jnp.where(incl[d], g[c2:, :c2], 0.0).astype(BF16) for d, g in zip(dirs, gs)]
    m_rk = [jnp.where(incl[d], g[c2:, c2:], 0.0).astype(BF16) for d, g in zip(dirs, gs)]

    eye2 = (rt == ci).astype(F32)
    ts = [eye2 + l.astype(F32) for l in l_ab]
    lps = l_ab
    for _ in range(int(np.log2(c)) - 1):
        lps = [_dot(lp, lp).astype(BF16) for lp in lps]
        ts = [tm + _dot(lp, tm.astype(BF16)) for tm, lp in zip(ts, lps)]
    xbs = [_dot(ts[i].astype(BF16),
                jnp.concatenate([a_s[i], _dot(l_ak[i], v_s[i]).astype(BF16)], axis=1)).astype(BF16)
           for i in range(n)]
    mxs = [_dot(m_rb[i], xbs[i]) for i in range(n)]
    gqs = [(r_s[i].astype(F32) + mxs[i][:, :PAIR]).astype(BF16) for i in range(n)]
    yqs = [mxs[i][:, PAIR:] + _dot(m_rk[i], v_s[i]) for i in range(n)]
    wbs = [_dot_tn(xbs[i], b_s[i]) for i in range(n)]
    vks = [_dot_tn(v_s[i], k_s[i]) for i in range(n)]
    p_mats = [((eye + wbs[i][:PAIR]) * g_end[i]).astype(BF16) for i in range(n)]
    q_mats = [(wbs[i][PAIR:] + vks[i]) * g_end[i] for i in range(n)]
    for i, ((_, d, p), rw, sl) in enumerate(zip(chains, rws, sls)):
        s0b = s_ref[d * n_pairs + p].astype(BF16)
        ys = _dot_nt(gqs[i], s0b) + yqs[i]
        refs[d][6][rw, sl] = ys[:c] + ys[c:]
        s_ref[d * n_pairs + p] = _dot(s0b, p_mats[i]) + q_mats[i]


def _wkv_scan(v, r, kkn, lw, kd, bb, geo):
    rows = v.shape[0]
    c = geo.chunk
    n_sub = max(m for m in (1, 2, 4) if (geo.lc // c) % m == 0 and (geo.s // c) % m == 0)
    cb = n_sub * c
    n_ctx, n_lat = geo.lc // cb, geo.s // cb
    lat_blocks = geo.r_lat // cb

    def rowblk(d):
        def f(b, s):
            ctx_j = s if d == 0 else n_ctx - 1 - s
            lat_j = s - n_ctx if d == 0 else n_lat - 1 - (s - n_ctx)
            return jnp.where(s < n_ctx, lat_blocks + b * n_ctx + ctx_j, b * n_lat + lat_j)
        return f

    def shared(d):
        f = rowblk(d)
        return pl.BlockSpec((cb, D_RWKV), lambda b, s: (f(b, s), 0))

    def perdir(d):
        f = rowblk(d)
        return pl.BlockSpec((None, cb, D_RWKV), lambda b, s: (d, f(b, s), 0))

    in_specs = []
    for d in range(2):
        in_specs += [shared(d), shared(d), shared(d), perdir(d), perdir(d), perdir(d)]
    return pl.pallas_call(
        functools.partial(_wkv_kernel, c=c, n_sub=n_sub),
        out_shape=(jax.ShapeDtypeStruct((rows, D_RWKV), F32),) * 2,
        grid=(geo.b, n_ctx + n_lat),
        in_specs=in_specs,
        out_specs=(shared(0), shared(1)),
        scratch_shapes=[pltpu.VMEM((RWKV_HEADS, PAIR, PAIR), F32)],
        compiler_params=_cparams("parallel", "arbitrary"),
        name="wkv_scan",
    )(v, r, kkn, lw, kd, bb, v, r, kkn, lw, kd, bb)


def _rwkv_out_kernel(yf_ref, yb_ref, r_ref, v_ref, kd_ref, gd_ref, lg_ref, lb_ref, rk_ref, gl_ref,
                     hsum_ref, o_ref):
    hsum = hsum_ref[...]
    inv_n = 1.0 / RWKV_HEAD_DIM
    y = yf_ref[...] + yb_ref[...]
    mu = _split_dot(y, hsum) * inv_n
    dlt = y - mu
    var = _split_dot(dlt * dlt, hsum) * inv_n
    yn = dlt * lax.rsqrt(var + GN_EPS) * lg_ref[...] + lb_ref[...]
    k_bonus = 0.5 * (kd_ref[0].astype(F32) + kd_ref[1].astype(F32))
    rsum = _split_dot(r_ref[...].astype(F32) * k_bonus * rk_ref[...], hsum)
    bonus = rsum * v_ref[...].astype(F32)
    gate = _dot(_sigmoid(gd_ref[...].astype(F32)).astype(BF16), gl_ref[...])
    o_ref[...] = ((yn + bonus) * gate).astype(BF16)


def _rwkv_output(yf, yb, r, v, kd, z, p, geo):
    rows = r.shape[0]
    t = geo.tr
    gcb = (geo.col0 + OFF_GD) // GATE_LORA
    row_spec = pl.BlockSpec((t, D_RWKV), lambda i: (i, 0))
    dir_spec = pl.BlockSpec((2, t, D_RWKV), lambda i: (0, i, 0))
    vec = pl.BlockSpec((1, D_RWKV), lambda i: (0, 0))
    return pl.pallas_call(
        _rwkv_out_kernel,
        out_shape=jax.ShapeDtypeStruct((rows, D_RWKV), BF16),
        grid=(rows // t,),
        in_specs=[row_spec, row_spec, row_spec, row_spec, dir_spec,
                  pl.BlockSpec((t, GATE_LORA), lambda i: (i, gcb)),
                  vec, vec, vec,
                  pl.BlockSpec((GATE_LORA, D_RWKV), lambda i: (0, 0)),
                  pl.BlockSpec((D_RWKV, D_RWKV), lambda i: (0, 0))],
        out_specs=row_spec,
        compiler_params=_cparams("parallel"),
        name="rwkv_output",
    )(yf, yb, r, v, kd, z, p['lnx_g'], p['lnx_b'], p['r_k'], p['g_lora'], geo.hsum)


def _attn_prep_kernel(q_ref, k_ref, cos_ref, sin_ref, qg_ref, kg_ref, qo_ref, ko_ref):
    cos = cos_ref[...]
    sin = sin_ref[...]
    lane = lax.broadcasted_iota(jnp.int32, cos.shape, 1)
    low_half = (lane % (2 * AXIS_FREQS)) < AXIS_FREQS

    def norm_rope(x, g, scale):
        x = x.astype(F32)
        xn = x * lax.rsqrt(jnp.mean(x * x, axis=-1, keepdims=True) + NORM_EPS) * g
        partner = jnp.where(low_half, pltpu.roll(xn, ATTN_HEAD_DIM - AXIS_FREQS, 1),
                            pltpu.roll(xn, AXIS_FREQS, 1))
        return ((xn * cos + partner * sin) * scale).astype(BF16)

    for h in range(ATTN_HEADS):
        sl = slice(h * ATTN_HEAD_DIM, (h + 1) * ATTN_HEAD_DIM)
        qo_ref[:, sl] = norm_rope(q_ref[:, sl], qg_ref[...], ATTN_SCALE)
    for h in range(ATTN_KV_HEADS):
        sl = slice(h * ATTN_HEAD_DIM, (h + 1) * ATTN_HEAD_DIM)
        ko_ref[:, sl] = norm_rope(k_ref[:, sl], kg_ref[...], 1.0)


def _attn_prep(z, cos_t, sin_t, p, geo):
    rows = z.shape[0]
    t = geo.tr
    qcb = (geo.col0 + OFF_Q) // D_ATTN
    kcb = (geo.col0 + OFF_KA) // D_ATTN_KV
    lat_tiles, lat_per = geo.r_lat // t, geo.s // t

    def tab(i):
        return (jnp.where(i < lat_tiles, i % lat_per, lat_per), 0)

    vec = pl.BlockSpec((1, ATTN_HEAD_DIM), lambda i: (0, 0))
    return pl.pallas_call(
        _attn_prep_kernel,
        out_shape=(jax.ShapeDtypeStruct((rows, D_ATTN), BF16),
                   jax.ShapeDtypeStruct((rows, D_ATTN_KV), BF16)),
        grid=(rows // t,),
        in_specs=[pl.BlockSpec((t, D_ATTN), lambda i: (i, qcb)),
                  pl.BlockSpec((t, D_ATTN_KV), lambda i: (i, kcb)),
                  pl.BlockSpec((t, ATTN_HEAD_DIM), tab),
                  pl.BlockSpec((t, ATTN_HEAD_DIM), tab),
                  vec, vec],
        out_specs=(pl.BlockSpec((t, D_ATTN), lambda i: (i, 0)),
                   pl.BlockSpec((t, D_ATTN_KV), lambda i: (i, 0))),
        compiler_params=_cparams("parallel"),
        name="attn_prep",
    )(z, z, cos_t, sin_t, p['q_norm_g'], p['k_norm_g'])


def _attn_kernel(*refs, n_seg):
    q_ref = refs[0]
    k_refs = refs[1:1 + n_seg]
    v_refs = refs[1 + n_seg:1 + 2 * n_seg]
    o_ref = refs[1 + 2 * n_seg]
    def qk(g):
        q = q_ref[:, g * ATTN_HEAD_DIM:(g + 1) * ATTN_HEAD_DIM]
        return [_dot_nt(q, k_ref[...]) for k_ref in k_refs]

    nxt = qk(0)
    for g in range(GQA_GROUP):
        sl = slice(g * ATTN_HEAD_DIM, (g + 1) * ATTN_HEAD_DIM)
        scores = nxt
        if g + 1 < GQA_GROUP:
            nxt = qk(g + 1)
        m = scores[0].max(axis=-1, keepdims=True)
        for s in scores[1:]:
            m = jnp.maximum(m, s.max(axis=-1, keepdims=True))
        denom = None
        acc = None
        for s, v_ref in zip(scores, v_refs):
            e = jnp.exp(s - m)
            es = e.sum(axis=-1, keepdims=True)
            pv = _dot(e.astype(BF16), v_ref[...])
            denom = es if denom is None else denom + es
            acc = pv if acc is None else acc + pv
        o_ref[:, sl] = (acc / denom).astype(BF16)


def _attention(qn, kn, z, geo, latent):
    gw = GQA_GROUP * ATTN_HEAD_DIM
    vcb = (geo.col0 + OFF_VA) // ATTN_HEAD_DIM
    ctx_blk0 = geo.r_lat // geo.lc
    k_ctx = pl.BlockSpec((geo.lc, ATTN_HEAD_DIM), lambda b, h, i: (ctx_blk0 + b, h))
    v_ctx = pl.BlockSpec((geo.lc, ATTN_HEAD_DIM), lambda b, h, i: (ctx_blk0 + b, vcb + h))
    if latent:
        tq = geo.tq
        per = geo.s // tq
        q_spec = pl.BlockSpec((tq, gw), lambda b, h, i: (b * per + i, h))
        k_lat = pl.BlockSpec((geo.s, ATTN_HEAD_DIM), lambda b, h, i: (b, h))
        v_lat = pl.BlockSpec((geo.s, ATTN_HEAD_DIM), lambda b, h, i: (b, vcb + h))
        in_specs = [q_spec, k_ctx, k_lat, v_ctx, v_lat]
        args = (qn, kn, kn, z, z)
        n_seg, out_rows = 2, geo.r_lat
        o_spec = q_spec
    else:
        tq, per = geo.lc, 1
        q_spec = pl.BlockSpec((tq, gw), lambda b, h, i: (ctx_blk0 + b, h))
        in_specs = [q_spec, k_ctx, v_ctx]
        args = (qn, kn, z)
        n_seg, out_rows = 1, geo.r_ctx
        o_spec = pl.BlockSpec((tq, gw), lambda b, h, i: (b, h))
    return pl.pallas_call(
        functools.partial(_attn_kernel, n_seg=n_seg),
        out_shape=jax.ShapeDtypeStruct((out_rows, D_ATTN), BF16),
        grid=(geo.b, ATTN_KV_HEADS, per),
        in_specs=in_specs,
        out_specs=o_spec,
        compiler_params=_cparams("parallel", "parallel", "parallel"),
        name="attention_lat" if latent else "attention_ctx",
    )(*args)


def _mm_kernel(x_ref, w_ref, o_ref):
    o_ref[...] = _dot(x_ref[...], w_ref[...]).astype(o_ref.dtype)


def _fourier_channels(z, wcs, geo):
    rows = z.shape[0]
    t = geo.tr
    fcb = (geo.col0 + OFF_FOUR) // D_FOURIER
    return pl.pallas_call(
        _mm_kernel,
        out_shape=jax.ShapeDtypeStruct((rows, 2 * D_FOURIER), BF16),
        grid=(rows // t,),
        in_specs=[pl.BlockSpec((t, D_FOURIER), lambda i: (i, fcb)),
                  pl.BlockSpec((D_FOURIER, 2 * D_FOURIER), lambda i: (0, 0))],
        out_specs=pl.BlockSpec((t, 2 * D_FOURIER), lambda i: (i, 0)),
        compiler_params=_cparams("parallel"),
        name="fourier_channels",
    )(z, wcs)


def _dft_kernel(cl_ref, sl_ref, xc_ref, xs_ref, o_ref):
    o_ref[...] = (_dot(cl_ref[...], xc_ref[...]) - _dot(sl_ref[...], xs_ref[...])).astype(BF16)


def _fourier_positions(xcs, cl, sl, n, blk0, nb, tmf):
    per = n // tmf
    return pl.pallas_call(
        _dft_kernel,
        out_shape=jax.ShapeDtypeStruct((nb * n, D_FOURIER), BF16),
        grid=(per, nb),
        in_specs=[pl.BlockSpec((tmf, n), lambda i, b: (i, 0)),
                  pl.BlockSpec((tmf, n), lambda i, b: (i, 0)),
                  pl.BlockSpec((n, D_FOURIER), lambda i, b: (blk0 + b, 0)),
                  pl.BlockSpec((n, D_FOURIER), lambda i, b: (blk0 + b, 1))],
        out_specs=pl.BlockSpec((tmf, D_FOURIER), lambda i, b: (b * per + i, 0)),
        compiler_params=_cparams("parallel", "parallel"),
        name="fourier_positions",
    )(cl, sl, xcs, xcs)


def _merge_kernel(rw_ref, fo_ref, at_ref, ga_ref, gf_ref, gc_ref, wr_ref, wf_ref, wa_ref, o_ref):
    y = _sigmoid(ga_ref[...].astype(F32)) * _dot(rw_ref[...], wr_ref[...])
    y += _sigmoid(gf_ref[...].astype(F32)) * _dot(fo_ref[...], wf_ref[...])
    y += _sigmoid(gc_ref[...].astype(F32)) * _dot(at_ref[...], wa_ref[...])
    o_ref[...] = y.astype(BF16)


def _merge(rw, fo, at, z, p, geo, n_rows):
    d = geo.d
    tm = geo.tm
    tn = min(1024, d)
    nj = d // tn
    return pl.pallas_call(
        _merge_kernel,
        out_shape=jax.ShapeDtypeStruct((n_rows, d), BF16),
        grid=(nj, n_rows // tm),
        in_specs=[pl.BlockSpec((tm, D_RWKV), lambda j, i: (i, 0)),
                  pl.BlockSpec((tm, D_FOURIER), lambda j, i: (i, 0)),
                  pl.BlockSpec((tm, D_ATTN), lambda j, i: (i, 0)),
                  pl.BlockSpec((tm, tn), lambda j, i: (i, j)),
                  pl.BlockSpec((tm, tn), lambda j, i: (i, nj + j)),
                  pl.BlockSpec((tm, tn), lambda j, i: (i, 2 * nj + j)),
                  pl.BlockSpec((D_RWKV, tn), lambda j, i: (0, j)),
                  pl.BlockSpec((D_FOURIER, tn), lambda j, i: (0, j)),
                  pl.BlockSpec((D_ATTN, tn), lambda j, i: (0, j))],
        out_specs=pl.BlockSpec((tm, tn), lambda j, i: (i, j)),
        compiler_params=_cparams("parallel", "parallel"),
        name="merge",
    )(rw, fo, at, z, z, z, p['w_br_rwkv'], p['w_br_fourier'], p['w_br_attn'])


def _outres_kernel(y_ref, w_ref, x_ref, gt_ref, o_ref):
    o_ref[...] = x_ref[...] + gt_ref[...] * _dot(y_ref[...], w_ref[...])


def _out_proj_residual(y, w, xs, mod4, geo, n_rows):
    d = geo.d
    tm = geo.tm
    tn = min(1024, d)
    bidx = geo.batch_of_tile(tm)
    return pl.pallas_call(
        _outres_kernel,
        out_shape=jax.ShapeDtypeStruct((n_rows, d), F32),
        grid=(d // tn, n_rows // tm),
        in_specs=[pl.BlockSpec((tm, d), lambda j, i: (i, 0)),
                  pl.BlockSpec((d, tn), lambda j, i: (0, j)),
                  pl.BlockSpec((tm, tn), lambda j, i: (i, j)),
                  pl.BlockSpec((None, None, 1, tn), lambda j, i: (bidx(i), 2, 0, j))],
        out_specs=pl.BlockSpec((tm, tn), lambda j, i: (i, j)),
        compiler_params=_cparams("parallel", "parallel"),
        name="out_proj_residual",
    )(y, w, xs, mod4)


def _expert_ffn(h_lo, h_hi, wgu_ref, bgu_ref, wd_ref, bd_ref, f):
    half = wgu_ref.shape[0] // 2
    gu = _dot(h_lo, wgu_ref[:half, :]) + _dot(h_hi, wgu_ref[half:, :]) + bgu_ref[...]
    gate = jnp.minimum(gu[:, :f], SWIGLU_LIMIT)
    up = jnp.clip(gu[:, f:], -SWIGLU_LIMIT, SWIGLU_LIMIT)
    act = (up + 1.0) * gate * _sigmoid(SWIGLU_ALPHA * gate)
    return _dot(act.astype(BF16), wd_ref[...]) + bd_ref[...]


def _route(sel, cnt, n, t, n_exp):
    p = TOP_K * n
    max_used = p // t + n_exp
    assert max_used % 2 == 0
    n_steps = max_used // 2 + 1
    n_tiles = 2 * n_steps
    unused_key = 2 * n_exp
    counts = cnt[0, :n_exp].astype(jnp.int32)
    ptiles = (counts + t - 1) // t
    n_used = jnp.sum(ptiles)
    pad = ptiles * t - counts
    experts = jnp.arange(n_exp, dtype=jnp.int32)
    last_e = jnp.max(jnp.where(counts > 0, experts, 0))

    e_real = sel[:, :TOP_K].astype(jnp.int32).reshape(p)
    w_real = sel[:, TOP_K:2 * TOP_K].reshape(p)
    pair = jnp.arange(p, dtype=jnp.int32)
    e_cand = jnp.repeat(experts, t)
    c_cand = jnp.tile(jnp.arange(t, dtype=jnp.int32), n_exp)
    key_cand = jnp.where(c_cand < jnp.repeat(pad, t), 2 * e_cand + 1, unused_key)
    n_extra = n_tiles * t - p - n_exp * t
    dest_bits = int(p + (n_exp + 2) * t - 1).bit_length()
    assert (unused_key + 1) << dest_bits < 2 ** 31
    words = jnp.concatenate([
        ((2 * e_real) << dest_bits) | ((pair % TOP_K) * n + pair // TOP_K),
        (key_cand << dest_bits) | (p + e_cand * t + c_cand),
        jnp.full((n_extra,), unused_key << dest_bits, jnp.int32)])
    ws = jnp.concatenate([w_real, jnp.zeros((n_exp * t + n_extra,), F32)])
    word_s, w_s = lax.sort((words, ws), num_keys=1, is_stable=True)

    lane = jnp.arange(t, dtype=jnp.int32)[None, :]
    spare = p + n_exp * t + lane
    word_s = word_s.reshape(n_tiles, t)
    key_s, dest_s = word_s >> dest_bits, word_s & ((1 << dest_bits) - 1)
    dest = jnp.where(key_s == unused_key, spare, dest_s)
    tok_s = jnp.where(key_s % 2 == 0, dest_s % n, 0)
    tok_s = jnp.where(key_s == unused_key, 0, tok_s)
    te = jnp.where(jnp.arange(n_tiles) < n_used, jnp.minimum(key_s[:, 0] // 2, n_exp - 1), last_e)
    dest_prev = jnp.concatenate([spare, dest[1:-1:2]], axis=0)
    return (te.astype(jnp.int32), n_used.reshape(1).astype(jnp.int32), tok_s.reshape(n_tiles, 1, t),
            dest.reshape(n_tiles, 1, t), dest_prev.reshape(n_steps, 1, t), w_s.reshape(n_tiles * t, 1))


def _moe_routed_kernel(te_ref, nu_ref, toka_ref, tokb_ref, tokn_ref, dsta_ref, dstp_ref, wa_ref, wb_ref,
                       h_hbm, wgu_a, bgu_a, wd_a, bd_a, wgu_b, bgu_b, wd_b, bd_b, o_hbm,
                       ga, gb, oa, ob, gsem, ssem, *, f, t, dump0, n_dump):
    j = pl.program_id(0)
    n_used = nu_ref[0]

    def gather(idx_ref, buf, sem):
        @pl.when(n_used > 0)
        def _():
            for r in range(t):
                pltpu.make_async_copy(h_hbm.at[pl.ds(idx_ref[0, 0, r], 1)], buf.at[pl.ds(r, 1)], sem).start()

    def scatter(buf, idx_ref, sem):
        @pl.when(n_used > 0)
        def _():
            for r in range(t):
                pltpu.make_async_copy(buf.at[pl.ds(r, 1)], o_hbm.at[pl.ds(idx_ref[0, 0, r], 1)], sem).start()

    def wait_rows_in(buf, sem):
        pltpu.make_async_copy(h_hbm.at[pl.ds(0, t)], buf, sem).wait()

    def wait_rows_out(buf, sem):
        pltpu.make_async_copy(buf, o_hbm.at[pl.ds(0, t)], sem).wait()

    def experts(buf, wgu, bgu, wd, bd, w_ref):
        lo, hi = _unpack_halves(buf[...])
        y = _expert_ffn(lo.astype(BF16), hi.astype(BF16), wgu, bgu, wd, bd, f)
        return _pack_halves(y * w_ref[...])

    @pl.when(j == 0)
    def _():
        gather(toka_ref, ga, gsem.at[0])
        ob[...] = jnp.zeros_like(ob)
        fills = [pltpu.make_async_copy(ob, o_hbm.at[pl.ds(dump0 + e * t, t)], ssem.at[1])
                 for e in range(n_dump - 1)]
        for c in fills:
            c.start()
        for c in fills:
            c.wait()
        pltpu.make_async_copy(ob, o_hbm.at[pl.ds(dump0 + (n_dump - 1) * t, t)], ssem.at[0]).start()

    @pl.when(2 * j <= n_used)
    def _():
        wait_rows_in(ga, gsem.at[0])
        scatter(ob, dstp_ref, ssem.at[1])
        gather(tokb_ref, gb, gsem.at[1])
        ya = experts(ga, wgu_a, bgu_a, wd_a, bd_a, wa_ref)
        wait_rows_out(oa, ssem.at[0])
        oa[...] = ya
        wait_rows_in(gb, gsem.at[1])
        scatter(oa, dsta_ref, ssem.at[0])
        gather(tokn_ref, ga, gsem.at[0])
        yb = experts(gb, wgu_b, bgu_b, wd_b, bd_b, wb_ref)
        wait_rows_out(ob, ssem.at[1])
        ob[...] = yb

    @pl.when(j == pl.num_programs(0) - 1)
    def _():
        wait_rows_in(ga, gsem.at[0])
        wait_rows_out(oa, ssem.at[0])


def _moe_routed(hp, sel, cnt, wgu, bgu, wd, bd, geo, n_rows):
    d = geo.d
    n_exp, _, f2 = wgu.shape
    f = f2 // 2
    t = geo.t_moe
    te, n_used, tok, dest, dest_prev, w = _route(sel, cnt, n_rows, t, n_exp)
    n_tiles, n_steps = tok.shape[0], dest_prev.shape[0]
    n_dump = n_exp + 2
    smem = functools.partial(pl.BlockSpec, (1, 1, t), memory_space=pltpu.SMEM)

    def weights(which):
        return [pl.BlockSpec((None, d, f2), lambda j, te, nu: (te[2 * j + which], 0, 0)),
                pl.BlockSpec((None, 1, f2), lambda j, te, nu: (te[2 * j + which], 0, 0)),
                pl.BlockSpec((None, f, d), lambda j, te, nu: (te[2 * j + which], 0, 0)),
                pl.BlockSpec((None, 1, d), lambda j, te, nu: (te[2 * j + which], 0, 0))]

    grid_spec = pltpu.PrefetchScalarGridSpec(
        num_scalar_prefetch=2,
        grid=(n_steps,),
        in_specs=[smem(lambda j, te, nu: (2 * j, 0, 0)),
                  smem(lambda j, te, nu: (2 * j + 1, 0, 0)),
                  smem(lambda j, te, nu: (jnp.minimum(2 * j + 2, n_tiles - 1), 0, 0)),
                  smem(lambda j, te, nu: (2 * j, 0, 0)),
                  smem(lambda j, te, nu: (j, 0, 0)),
                  pl.BlockSpec((t, 1), lambda j, te, nu: (2 * j, 0)),
                  pl.BlockSpec((t, 1), lambda j, te, nu: (2 * j + 1, 0)),
                  pl.BlockSpec(memory_space=pl.ANY)] + weights(0) + weights(1),
        out_specs=pl.BlockSpec(memory_space=pl.ANY),
        scratch_shapes=[pltpu.VMEM((t, d // 2), jnp.uint32)] * 4
                       + [pltpu.SemaphoreType.DMA((2,)), pltpu.SemaphoreType.DMA((2,))])
    return pl.pallas_call(
        functools.partial(_moe_routed_kernel, f=f, t=t, dump0=TOP_K * n_rows, n_dump=n_dump),
        out_shape=jax.ShapeDtypeStruct((TOP_K * n_rows + n_dump * t, d // 2), jnp.uint32),
        grid_spec=grid_spec,
        compiler_params=pltpu.CompilerParams(dimension_semantics=("arbitrary",),
                                             vmem_limit_bytes=VMEM_LIMIT, disable_bounds_checks=True),
        name="moe_routed",
    )(te, n_used, tok, tok, tok, dest, dest_prev, w, w, hp, wgu, bgu, wd, bd, wgu, bgu, wd, bd)


def _combine_kernel(x_ref, gt_ref, y0_ref, y1_ref, y2_ref, y3_ref, o_ref):
    half = x_ref.shape[1] // 2
    parts = [_unpack_halves(y[...]) for y in (y0_ref, y1_ref, y2_ref, y3_ref)]
    lo = (parts[0][0] + parts[1][0]) + (parts[2][0] + parts[3][0])
    hi = (parts[0][1] + parts[1][1]) + (parts[2][1] + parts[3][1])
    o_ref[:, :half] = x_ref[:, :half] + gt_ref[:, :half] * lo
    o_ref[:, half:] = x_ref[:, half:] + gt_ref[:, half:] * hi


def _moe_combine(y4, xs, mod4, geo, n_rows):
    d = geo.d
    tm = geo.tr
    per = n_rows // tm
    bidx = geo.batch_of_tile(tm)

    def part(k):
        return pl.BlockSpec((tm, d // 2), lambda i: (k * per + i, 0))

    return pl.pallas_call(
        _combine_kernel,
        out_shape=jax.ShapeDtypeStruct((n_rows, d), F32),
        grid=(per,),
        in_specs=[pl.BlockSpec((tm, d), lambda i: (i, 0)),
                  pl.BlockSpec((None, None, 1, d), lambda i: (bidx(i), 5, 0, 0)),
                  part(0), part(1), part(2), part(3)],
        out_specs=pl.BlockSpec((tm, d), lambda i: (i, 0)),
        compiler_params=_cparams("parallel"),
        name="moe_combine",
    )(xs, mod4, y4, y4, y4, y4)


class _Geometry:
    def __init__(self, b, s, lc, d):
        self.b, self.s, self.lc, self.d = b, s, lc, d
        self.r_lat, self.r_ctx = b * s, b * lc
        self.rows = self.r_lat + self.r_ctx
        self.col0 = 3 * d
        assert self.col0 % 1024 == 0, "gate columns must end on a 1024-column boundary"
        self.tm = min(1024, s, self.r_ctx)
        self.tp = min(256, s, lc)
        self.tr = min(512, s, self.r_ctx)
        self.tq = min(512, s)
        self.chunk = min(WKV_CHUNK, lc, s)
        self.t_moe = min(256, self.tm)
        for t in (self.tm,):
            assert s % t == 0 and self.r_ctx % t == 0
        assert s % self.tp == 0 and lc % self.tp == 0 and self.tp % BF16_SUBLANES == 0
        assert s % self.tr == 0 and self.r_ctx % self.tr == 0
        assert s % self.chunk == 0 and lc % self.chunk == 0 and s % GRID_W == 0
        self.hsum = jnp.asarray(np.kron(np.eye(RWKV_HEADS), np.ones((RWKV_HEAD_DIM,) * 2)), BF16)

    def batch_of_tile(self, tm):
        per, nb = self.s // tm, self.b
        return lambda i: jnp.minimum(i // per, nb)


def _dft_tables(n):
    j = np.arange(n, dtype=np.int64)
    ang = 2.0 * np.pi * ((j[:, None] * j[None, :]) % n).astype(np.float64) / n
    return np.cos(ang) / np.sqrt(n), np.sin(ang) / np.sqrt(n)


def _rope_tables(s, t):
    pos = np.arange(s)
    inv_freq = ROPE_THETA ** (-np.arange(AXIS_FREQS, dtype=np.float32) / AXIS_FREQS)
    ang_r = (pos // GRID_W).astype(np.float32)[:, None] * inv_freq.astype(np.float32)
    ang_c = (pos % GRID_W).astype(np.float32)[:, None] * inv_freq.astype(np.float32)
    cr, sr, cc, sc = np.cos(ang_r), np.sin(ang_r), np.cos(ang_c), np.sin(ang_c)
    cos = np.concatenate([cr, cr, cc, cc], axis=1)
    sin = np.concatenate([-sr, sr, -sc, sc], axis=1)
    cos = np.concatenate([cos, np.ones((t, ATTN_HEAD_DIM))], axis=0)
    sin = np.concatenate([sin, np.zeros((t, ATTN_HEAD_DIM))], axis=0)
    return jnp.asarray(cos, F32), jnp.asarray(sin, F32)


def _permute_w_in(w_in, d):
    o = np.cumsum([0, D_RWKV, D_RWKV, LORA, LORA, LORA, LORA, D_ATTN_KV, D_ATTN_KV, D_RWKV, GATE_LORA,
                   D_ATTN, D_FOURIER, 3 * d])
    w_in = w_in.astype(BF16)
    seg = lambda i: w_in[..., o[i]:o[i + 1]]
    pad = lambda a: jnp.pad(a, ((0, 0), (0, 0), (0, LORA_PAD - LORA)))
    parts = [seg(12), seg(0), seg(1), pad(seg(2)), pad(seg(3)), pad(seg(4)), pad(seg(5)), seg(8), seg(10),
             seg(11), seg(6), seg(7), seg(9)]
    w = jnp.concatenate(parts, axis=-1)
    n = w.shape[-1]
    n_pad = -(-n // W_IN_TN) * W_IN_TN
    return jnp.pad(w, ((0, 0), (0, 0), (0, n_pad - n)))


def _split_hi_lo(w):
    hi = w.astype(BF16)
    return hi, (w - hi.astype(F32)).astype(BF16)


def kernel(x, c, ctx, c_ctx, ada_w, ada_b, norm1_g, norm2_g, w_in, rwkv_conv, w0, w_lora, a0, a_lora, g_lora, k_k, k_a, r_k, lnx_g, lnx_b, q_norm_g, k_norm_g, w_br_rwkv, w_br_fourier, w_br_attn, w_out, router_w, router_b, exp_w_gu, exp_b_gu, exp_w_down, exp_b_down):
    b, s, d = x.shape
    lc = ctx.shape[1]
    depth = w_in.shape[0]
    n_exp = router_w.shape[-1]
    geo = _Geometry(b, s, lc, d)

    w_in_p = _permute_w_in(w_in, d)
    lora_pad = ((0, 0), (0, 0), (0, LORA_PAD - LORA), (0, 0))
    w_lora_p = jnp.pad(w_lora, lora_pad).astype(BF16)
    a_lora_p = jnp.pad(a_lora, lora_pad).astype(BF16)
    g_lora_b = g_lora.astype(BF16)
    wbr_r, wbr_f, wbr_a, w_out_b = (w.astype(BF16) for w in (w_br_rwkv, w_br_fourier, w_br_attn, w_out))
    rw_p = jnp.pad(router_w, ((0, 0), (0, 0), (0, ROUTER_PAD - n_exp)))
    rw_hi, rw_lo = _split_hi_lo(rw_p)
    rb_p = jnp.pad(router_b, ((0, 0), (0, ROUTER_PAD - n_exp)), constant_values=NEG_BIG)
    wgu_b, wd_b = exp_w_gu.astype(BF16), exp_w_down.astype(BF16)
    cos_t, sin_t = _rope_tables(s, geo.tr)
    cc, sc = _dft_tables(FOURIER_GROUP_DIM)
    groups = D_FOURIER // FOURIER_GROUP_DIM
    wcs = jnp.asarray(np.concatenate([np.kron(np.eye(groups), cc), np.kron(np.eye(groups), sc)], axis=1), BF16)
    cl_lat, sl_lat = (jnp.asarray(m, BF16) for m in _dft_tables(s))
    cl_ctx, sl_ctx = (jnp.asarray(m, BF16) for m in _dft_tables(lc))

    mod_rows = -(-(b + 1) // BF16_SUBLANES) * BF16_SUBLANES
    c_all = jnp.concatenate([c, c_ctx[None], jnp.zeros((mod_rows - b - 1, d), F32)], axis=0)
    mod = _adaln(c_all, ada_w, ada_b).reshape(depth, mod_rows, 6, 1, d)

    xs = jnp.concatenate([x.reshape(b * s, d), ctx.reshape(b * lc, d)], axis=0)
    for l in range(depth):
        last = l == depth - 1
        mod4 = mod[l]
        p = dict(conv=rwkv_conv[l], w0=w0[l][:, None, :], w_lora=w_lora_p[l], a0=a0[l][:, None, :],
                 a_lora=a_lora_p[l], k_k=k_k[l][None], k_a=k_a[l][None], r_k=r_k[l][None],
                 lnx_g=lnx_g[l][None], lnx_b=lnx_b[l][None], g_lora=g_lora_b[l],
                 q_norm_g=q_norm_g[l][None], k_norm_g=k_norm_g[l][None],
                 w_br_rwkv=wbr_r[l], w_br_fourier=wbr_f[l], w_br_attn=wbr_a[l])
        n_rows = geo.r_lat if last else geo.rows

        z = _norm_in_proj(xs, norm1_g[l], mod4, w_in_p[l], geo)

        v_c, r_c, kkn, lw, kd, bb = _rwkv_prepare(z, p, geo)
        yf, yb = _wkv_scan(v_c, r_c, kkn, lw, kd, bb, geo)
        rw = _rwkv_output(yf, yb, r_c, v_c, kd, z, p, geo)

        qn, kn = _attn_prep(z, cos_t, sin_t, p, geo)
        att = _attention(qn, kn, z, geo, latent=True)

        xcs = _fourier_channels(z, wcs, geo)
        fo = _fourier_positions(xcs, cl_lat, sl_lat, s, 0, b, min(1024, s))
        if not last:
            att = jnp.concatenate([att, _attention(qn, kn, z, geo, latent=False)], axis=0)
            fo_c = _fourier_positions(xcs, cl_ctx, sl_ctx, lc, geo.r_lat // lc, b, lc)
            fo = jnp.concatenate([fo, fo_c], axis=0)

        ym = _merge(rw, fo, att, z, p, geo, n_rows)
        xs = _out_proj_residual(ym, w_out_b[l], xs, mod4, geo, n_rows)

        hp, sel, cnt = _norm_router(xs, norm2_g[l], mod4, rw_hi[l], rw_lo[l], rb_p[l][None], geo, n_rows)
        y4 = _moe_routed(hp, sel, cnt, wgu_b[l], exp_b_gu[l][:, None, :], wd_b[l], exp_b_down[l][:, None, :],
                         geo, n_rows)
        xs = _moe_combine(y4, xs, mod4, geo, n_rows)
    return xs[:geo.r_lat].reshape(b, s, d)
```

```python
import functools

import numpy as np
import jax
import jax.numpy as jnp
from jax import lax
from jax.experimental import pallas as pl
from jax.experimental.pallas import tpu as pltpu

F32 = jnp.float32
BF16 = jnp.bfloat16

NORM_EPS = 1e-6
GN_EPS = 64e-5
RWKV_HEADS = 8
RWKV_HEAD_DIM = 64
D_RWKV = RWKV_HEADS * RWKV_HEAD_DIM
LORA = 96
GATE_LORA = 256
D_FOURIER = 512
FOURIER_GROUP_DIM = 128
ATTN_HEADS = 8
ATTN_KV_HEADS = 2
ATTN_HEAD_DIM = 128
GQA_GROUP = ATTN_HEADS // ATTN_KV_HEADS
D_ATTN = ATTN_HEADS * ATTN_HEAD_DIM
D_ATTN_KV = ATTN_KV_HEADS * ATTN_HEAD_DIM
ATTN_SCALE = ATTN_HEAD_DIM ** -0.5
GRID_W = 64
ROPE_THETA = 10000.0
AXIS_FREQS = ATTN_HEAD_DIM // 4
TOP_K = 4
SWIGLU_LIMIT = 7.0
SWIGLU_ALPHA = 1.702

LANES = 128
BF16_SUBLANES = 16
VMEM_LIMIT = 56 * 1024 * 1024

LORA_PAD = LANES
W_IN_TN = 1536
WKV_CHUNK = 64
PAIR = 2 * RWKV_HEAD_DIM
ROUTER_PAD = LANES
NEG_BIG = -1e30

OFF_K, OFF_V, OFF_LORA, OFF_R, OFF_Q, OFF_FOUR, OFF_KA, OFF_VA, OFF_GD, OFF_END = (
    0, 512, 1024, 1536, 2048, 3072, 3584, 3840, 4096, 4352)


def _cparams(*sem):
    return pltpu.CompilerParams(dimension_semantics=sem, vmem_limit_bytes=VMEM_LIMIT)


def _dot(a, b):
    return jnp.dot(a, b, preferred_element_type=F32)


def _dot_nt(a, b):
    return lax.dot_general(a, b, (((1,), (1,)), ((), ())), preferred_element_type=F32)


def _dot_tn(a, b):
    return lax.dot_general(a, b, (((0,), (0,)), ((), ())), preferred_element_type=F32)


def _split_dot(x, g):
    hi = x.astype(BF16)
    lo = (x - hi.astype(F32)).astype(BF16)
    return _dot(hi, g) + _dot(lo, g)


def _sigmoid(x):
    return 1.0 / (1.0 + jnp.exp(-x))


def _adaln_kernel(c_ref, w_ref, b_ref, o_ref):
    c = c_ref[...]
    s = (c * _sigmoid(c)).astype(BF16)
    o_ref[...] = _dot(s, w_ref[...].astype(BF16)) + b_ref[...]


def _adaln(c_all, ada_w, ada_b):
    depth, d, n = ada_w.shape
    rows = c_all.shape[0]
    tn = 1024 if n % 1024 == 0 else n
    return pl.pallas_call(
        _adaln_kernel,
        out_shape=jax.ShapeDtypeStruct((depth, rows, n), F32),
        grid=(depth, n // tn),
        in_specs=[
            pl.BlockSpec((rows, d), lambda l, j: (0, 0)),
            pl.BlockSpec((None, d, tn), lambda l, j: (l, 0, j)),
            pl.BlockSpec((None, 1, tn), lambda l, j: (l, 0, j)),
        ],
        out_specs=pl.BlockSpec((None, rows, tn), lambda l, j: (l, 0, j)),
        compiler_params=_cparams("parallel", "parallel"),
        name="adaln",
    )(c_all, ada_w, ada_b.reshape(depth, 1, n))


def _modulated_norm(x, g, sc, sh):
    ms = jnp.mean(x * x, axis=-1, keepdims=True)
    return x * lax.rsqrt(ms + NORM_EPS) * g * (1.0 + sc) + sh


def _normmm_kernel(x_ref, g_ref, sc_ref, sh_ref, w_ref, o_ref, h_ref):
    @pl.when(pl.program_id(1) == 0)
    def _():
        h_ref[...] = _modulated_norm(x_ref[...], g_ref[...], sc_ref[...], sh_ref[...]).astype(BF16)

    o_ref[...] = _dot(h_ref[...], w_ref[...]).astype(o_ref.dtype)


def _norm_in_proj(xs, gain, mod4, w, geo):
    rows, d = xs.shape
    n = w.shape[1]
    tm, tn = geo.tm, W_IN_TN
    bidx = geo.batch_of_tile(tm)
    return pl.pallas_call(
        _normmm_kernel,
        out_shape=jax.ShapeDtypeStruct((rows, n), BF16),
        grid=(rows // tm, n // tn),
        in_specs=[
            pl.BlockSpec((tm, d), lambda i, j: (i, 0)),
            pl.BlockSpec((1, d), lambda i, j: (0, 0)),
            pl.BlockSpec((None, None, 1, d), lambda i, j: (bidx(i), 1, 0, 0)),
            pl.BlockSpec((None, None, 1, d), lambda i, j: (bidx(i), 0, 0, 0)),
            pl.BlockSpec((d, tn), lambda i, j: (0, j)),
        ],
        out_specs=pl.BlockSpec((tm, tn), lambda i, j: (i, j)),
        scratch_shapes=[pltpu.VMEM((tm, d), BF16)],
        compiler_params=_cparams("parallel", "arbitrary"),
        name="norm_in_proj",
    )(xs, gain.reshape(1, d), mod4, mod4, w)


def _pack_halves(x):
    half = x.shape[1] // 2
    xb = x.astype(BF16).astype(F32)
    lo = lax.shift_right_logical(lax.bitcast_convert_type(xb[:, :half], jnp.uint32), jnp.uint32(16))
    hi = lax.bitcast_convert_type(xb[:, half:], jnp.uint32) & jnp.uint32(0xFFFF0000)
    return lo | hi


def _unpack_halves(words):
    lo = lax.bitcast_convert_type(lax.shift_left(words, jnp.uint32(16)), F32)
    hi = lax.bitcast_convert_type(words & jnp.uint32(0xFFFF0000), F32)
    return lo, hi


def _norm_router_kernel(x_ref, g_ref, sc_ref, sh_ref, wh_ref, wl_ref, rb_ref, h_ref, sel_ref, cnt_ref):
    @pl.when(pl.program_id(0) == 0)
    def _():
        cnt_ref[...] = jnp.zeros_like(cnt_ref)

    h = _modulated_norm(x_ref[...], g_ref[...], sc_ref[...], sh_ref[...])
    h_ref[...] = _pack_halves(h)
    hi = h.astype(BF16)
    lo = (h - hi.astype(F32)).astype(BF16)
    logits = _dot(hi, wh_ref[...]) + _dot(lo, wh_ref[...]) + _dot(hi, wl_ref[...]) + rb_ref[...]
    lane = lax.broadcasted_iota(jnp.int32, logits.shape, 1).astype(F32)
    work = logits
    vals, firsts = [], []
    for _ in range(TOP_K):
        m = jnp.max(work, axis=-1, keepdims=True)
        first = jnp.min(jnp.where(work == m, lane, float(ROUTER_PAD)), axis=-1, keepdims=True)
        vals.append(m)
        firsts.append(first)
        work = jnp.where(lane == first, 2.0 * NEG_BIG, work)
    exps = [jnp.exp(v - vals[0]) for v in vals]
    denom = exps[0] + exps[1] + exps[2] + exps[3]
    table = jnp.zeros_like(logits)
    picked = jnp.zeros_like(logits)
    for k in range(TOP_K):
        table = jnp.where(lane == float(k), firsts[k], table)
        table = jnp.where(lane == float(TOP_K + k), exps[k] / denom, table)
        picked = picked + jnp.where(lane == firsts[k], 1.0, 0.0)
    sel_ref[...] = table
    cnt_ref[...] += jnp.sum(picked, axis=0, keepdims=True)


def _norm_router(xs, gain, mod4, rw_hi, rw_lo, rb, geo, n_rows):
    d = xs.shape[1]
    tm = geo.tm
    bidx = geo.batch_of_tile(tm)
    return pl.pallas_call(
        _norm_router_kernel,
        out_shape=(jax.ShapeDtypeStruct((n_rows, d // 2), jnp.uint32),
                   jax.ShapeDtypeStruct((n_rows, ROUTER_PAD), F32),
                   jax.ShapeDtypeStruct((1, ROUTER_PAD), F32)),
        grid=(n_rows // tm,),
        in_specs=[
            pl.BlockSpec((tm, d), lambda i: (i, 0)),
            pl.BlockSpec((1, d), lambda i: (0, 0)),
            pl.BlockSpec((None, None, 1, d), lambda i: (bidx(i), 4, 0, 0)),
            pl.BlockSpec((None, None, 1, d), lambda i: (bidx(i), 3, 0, 0)),
            pl.BlockSpec((d, ROUTER_PAD), lambda i: (0, 0)),
            pl.BlockSpec((d, ROUTER_PAD), lambda i: (0, 0)),
            pl.BlockSpec((1, ROUTER_PAD), lambda i: (0, 0)),
        ],
        out_specs=(pl.BlockSpec((tm, d // 2), lambda i: (i, 0)),
                   pl.BlockSpec((tm, ROUTER_PAD), lambda i: (i, 0)),
                   pl.BlockSpec((1, ROUTER_PAD), lambda i: (0, 0))),
        compiler_params=_cparams("arbitrary"),
        name="norm_router",
    )(xs, gain.reshape(1, d), mod4, mod4, rw_hi, rw_lo, rb)


def _prep_kernel(zk_ref, zv_ref, zl_ref, zr_ref, pk_ref, pv_ref, pr_ref, nk_ref, nv_ref, nr_ref,
                 conv_ref, w0_ref, wl_ref, a0_ref, al_ref, kk_ref, ka_ref, hsum_ref,
                 v_out, r_out, kkn_out, lw_out, kd_out, bb_out, *, t, lat_tiles, lat_per, ctx_per):
    i = pl.program_id(0)
    is_lat = i < lat_tiles
    per = jnp.where(is_lat, lat_per, ctx_per)
    j = jnp.where(is_lat, i, i - lat_tiles) % per
    first = j == 0
    last = j == per - 1
    row = lax.broadcasted_iota(jnp.int32, (t, 1), 0)

    def conv(z_ref, p_ref, n_ref, which):
        z = z_ref[...].astype(F32)
        prev_row = p_ref[...].astype(F32)[BF16_SUBLANES - 1:BF16_SUBLANES, :]
        next_row = n_ref[...].astype(F32)[0:1, :]
        prev_row = jnp.where(first, 0.0, prev_row)
        next_row = jnp.where(last, 0.0, next_row)
        zm = jnp.where(row == 0, prev_row, pltpu.roll(z, 1, 0))
        zp = jnp.where(row == t - 1, next_row, pltpu.roll(z, t - 1, 0))
        w = conv_ref[which]
        return zm * w[0:1] + z * w[1:2] + zp * w[2:3]

    k = conv(zk_ref, pk_ref, nk_ref, 0)
    v = conv(zv_ref, pv_ref, nv_ref, 1)
    r = conv(zr_ref, pr_ref, nr_ref, 2)
    v_out[...] = v.astype(BF16)
    r_out[...] = r.astype(BF16)
    kkv = k * kk_ref[...]
    ss = _split_dot(kkv * kkv, hsum_ref[...])
    kkn = kkv * lax.rsqrt(jnp.maximum(ss, 1e-24))
    kkn_out[...] = kkn.astype(BF16)
    zl = zl_ref[...]
    for d in range(2):
        wd = zl[:, d * LORA_PAD:(d + 1) * LORA_PAD].astype(F32)
        w_raw = w0_ref[d] + _dot(jnp.tanh(wd).astype(BF16), wl_ref[d])
        lw_out[d] = -_sigmoid(w_raw) * float(np.exp(-0.5))
        ad = zl[:, (2 + d) * LORA_PAD:(3 + d) * LORA_PAD]
        a = _sigmoid(a0_ref[d] + _dot(ad, al_ref[d]))
        kd_out[d] = (k * (1.0 + (a - 1.0) * ka_ref[...])).astype(BF16)
        bb_out[d] = (kkn * a).astype(BF16)


def _rwkv_prepare(z, p, geo):
    rows = z.shape[0]
    t = geo.tp
    base = geo.col0
    hb = t // BF16_SUBLANES
    n_halo = rows // BF16_SUBLANES

    def zcol(off):
        cb = (base + off) // D_RWKV
        return pl.BlockSpec((t, D_RWKV), lambda i: (i, cb))

    def prev(off):
        cb = (base + off) // D_RWKV
        return pl.BlockSpec((BF16_SUBLANES, D_RWKV), lambda i: (jnp.maximum(i * hb - 1, 0), cb))

    def nxt(off):
        cb = (base + off) // D_RWKV
        return pl.BlockSpec((BF16_SUBLANES, D_RWKV),
                            lambda i: (jnp.minimum((i + 1) * hb, n_halo - 1), cb))

    def full(a):
        nd = a.ndim
        return pl.BlockSpec(a.shape, lambda i: (0,) * nd)

    params = (p['conv'], p['w0'], p['w_lora'], p['a0'], p['a_lora'], p['k_k'], p['k_a'], geo.hsum)
    row_spec = pl.BlockSpec((t, D_RWKV), lambda i: (i, 0))
    dir_spec = pl.BlockSpec((2, t, D_RWKV), lambda i: (0, i, 0))
    kern = functools.partial(_prep_kernel, t=t, lat_tiles=geo.r_lat // t, lat_per=geo.s // t,
                             ctx_per=geo.lc // t)
    return pl.pallas_call(
        kern,
        out_shape=(jax.ShapeDtypeStruct((rows, D_RWKV), BF16),
                   jax.ShapeDtypeStruct((rows, D_RWKV), BF16),
                   jax.ShapeDtypeStruct((rows, D_RWKV), BF16),
                   jax.ShapeDtypeStruct((2, rows, D_RWKV), F32),
                   jax.ShapeDtypeStruct((2, rows, D_RWKV), BF16),
                   jax.ShapeDtypeStruct((2, rows, D_RWKV), BF16)),
        grid=(rows // t,),
        in_specs=[zcol(OFF_K), zcol(OFF_V), zcol(OFF_LORA), zcol(OFF_R),
                  prev(OFF_K), prev(OFF_V), prev(OFF_R), nxt(OFF_K), nxt(OFF_V), nxt(OFF_R)]
                 + [full(a) for a in params],
        out_specs=(row_spec, row_spec, row_spec, dir_spec, dir_spec, dir_spec),
        compiler_params=_cparams("parallel"),
        name="rwkv_prepare",
    )(z, z, z, z, z, z, z, z, z, z, *params)


def _wkv_kernel(vf_ref, rf_ref, kf_ref, lwf_ref, kdf_ref, bbf_ref,
                vb_ref, rb_ref, kb_ref, lwb_ref, kdb_ref, bbb_ref, yf_ref, yb_ref, s_ref, *, c, n_sub):
    @pl.when(pl.program_id(1) == 0)
    def _():
        s_ref[...] = jnp.zeros_like(s_ref)

    c2 = 2 * c
    n_pairs = RWKV_HEADS // 2
    ti = lax.broadcasted_iota(jnp.int32, (c, c), 0)
    ii = lax.broadcasted_iota(jnp.int32, (c, c), 1)
    rt = lax.broadcasted_iota(jnp.int32, (c2, c2), 0)
    ci = lax.broadcasted_iota(jnp.int32, (c2, c2), 1)
    same = (rt // c) == (ci // c)
    tri, strict, incl = [], [], []
    for sign in (1, -1):
        tri.append(jnp.where((ti - ii) * sign >= 0, 1.0, 0.0).astype(BF16))
        before = (rt % c - ci % c) * sign
        strict.append(same & (before > 0))
        incl.append(same & (before >= 0))
    head0 = lax.broadcasted_iota(jnp.int32, (c, PAIR), 1) < RWKV_HEAD_DIM
    eye = (lax.broadcasted_iota(jnp.int32, (PAIR, PAIR), 0)
           == lax.broadcasted_iota(jnp.int32, (PAIR, PAIR), 1)).astype(F32)

    def stack(x):
        return jnp.concatenate([jnp.where(head0, x, 0.0), jnp.where(head0, 0.0, x)], axis=0).astype(BF16)

    refs = ((vf_ref, rf_ref, kf_ref, lwf_ref, kdf_ref, bbf_ref, yf_ref),
            (vb_ref, rb_ref, kb_ref, lwb_ref, kdb_ref, bbb_ref, yb_ref))
    chains = [(u, d, p) for u in range(n_sub) for d in range(2) for p in range(n_pairs)]
    rws = [slice(u * c, (u + 1) * c) if d == 0 else slice((n_sub - 1 - u) * c, (n_sub - u) * c)
           for u, d, _ in chains]
    sls = [slice(p * PAIR, (p + 1) * PAIR) for _, _, p in chains]
    dirs = [d for _, d, _ in chains]
    n = len(chains)

    lws = [refs[d][3][rw, sl] for d, rw, sl in zip(dirs, rws, sls)]
    cums = []
    for d, lw in zip(dirs, lws):
        hi = lw.astype(BF16)
        lo = (lw - hi.astype(F32)).astype(BF16)
        cums.append(_dot(tri[d], hi) + _dot(tri[d], lo))
    g_end = [jnp.exp(jnp.sum(lw, axis=0, keepdims=True)) for lw in lws]
    a_s, b_s, k_s, r_s, v_s = [], [], [], [], []
    for d, rw, sl, lw, cum in zip(dirs, rws, sls, lws, cums):
        v_ref, r_ref, kk_ref, _, kd_ref, bb_ref, _ = refs[d]
        g_inv = jnp.exp(-cum)
        a_s.append(stack(-kk_ref[rw, sl].astype(F32) * jnp.exp(cum - lw)))
        b_s.append(stack(bb_ref[rw, sl].astype(F32) * g_inv))
        k_s.append(stack(kd_ref[rw, sl].astype(F32) * g_inv))
        r_s.append(stack(r_ref[rw, sl].astype(F32) * jnp.exp(cum)))
        v_s.append(stack(v_ref[rw, sl].astype(F32)))
    gs = [_dot_nt(jnp.concatenate([a_s[i], r_s[i]], axis=0), jnp.concatenate([b_s[i], k_s[i]], axis=0))
          for i in range(n)]
    l_ab = [jnp.where(strict[d], g[:c2, :c2], 0.0).astype(BF16) for d, g in zip(dirs, gs)]
    l_ak = [jnp.where(strict[d], g[:c2, c2:], 0.0).astype(BF16) for d, g in zip(dirs, gs)]
    m_rb = [jnp.where(incl[d], g[c2:, :c2], 0.0).astype(BF16) for d, g in zip(dirs, gs)]
    m_rk = [jnp.where(incl[d], g[c2:, c2:], 0.0).astype(BF16) for d, g in zip(dirs, gs)]

    eye2 = (rt == ci).astype(F32)
    ts = [eye2 + l.astype(F32) for l in l_ab]
    lps = l_ab
    for _ in range(int(np.log2(c)) - 1):
        lps = [_dot(lp, lp).astype(BF16) for lp in lps]
        ts = [tm + _dot(lp, tm.astype(BF16)) for tm, lp in zip(ts, lps)]
    xbs = [_dot(ts[i].astype(BF16),
                jnp.concatenate([a_s[i], _dot(l_ak[i], v_s[i]).astype(BF16)], axis=1)).astype(BF16)
           for i in range(n)]
    mxs = [_dot(m_rb[i], xbs[i]) for i in range(n)]
    gqs = [(r_s[i].astype(F32) + mxs[i][:, :PAIR]).astype(BF16) for i in range(n)]
    yqs = [mxs[i][:, PAIR:] + _dot(m_rk[i], v_s[i]) for i in range(n)]
    wbs = [_dot_tn(xbs[i], b_s[i]) for i in range(n)]
    vks = [_dot_tn(v_s[i], k_s[i]) for i in range(n)]
    p_mats = [((eye + wbs[i][:PAIR]) * g_end[i]).astype(BF16) for i in range(n)]
    q_mats = [(wbs[i][PAIR:] + vks[i]) * g_end[i] for i in range(n)]
    for i, ((_, d, p), rw, sl) in enumerate(zip(chains, rws, sls)):
        s0b = s_ref[d * n_pairs + p].astype(BF16)
        ys = _dot_nt(gqs[i], s0b) + yqs[i]
        refs[d][6][rw, sl] = ys[:c] + ys[c:]
        s_ref[d * n_pairs + p] = _dot(s0b, p_mats[i]) + q_mats[i]


def _wkv_scan(v, r, kkn, lw, kd, bb, geo):
    rows = v.shape[0]
    c = geo.chunk
    n_sub = max(m for m in (1, 2, 4) if (geo.lc // c) % m == 0 and (geo.s // c) % m == 0)
    cb = n_sub * c
    n_ctx, n_lat = geo.lc // cb, geo.s // cb
    lat_blocks = geo.r_lat // cb

    def rowblk(d):
        def f(b, s):
            ctx_j = s if d == 0 else n_ctx - 1 - s
            lat_j = s - n_ctx if d == 0 else n_lat - 1 - (s - n_ctx)
            return jnp.where(s < n_ctx, lat_blocks + b * n_ctx + ctx_j, b * n_lat + lat_j)
        return f

    def shared(d):
        f = rowblk(d)
        return pl.BlockSpec((cb, D_RWKV), lambda b, s: (f(b, s), 0))

    def perdir(d):
        f = rowblk(d)
        return pl.BlockSpec((None, cb, D_RWKV), lambda b, s: (d, f(b, s), 0))

    in_specs = []
    for d in range(2):
        in_specs += [shared(d), shared(d), shared(d), perdir(d), perdir(d), perdir(d)]
    return pl.pallas_call(
        functools.partial(_wkv_kernel, c=c, n_sub=n_sub),
        out_shape=(jax.ShapeDtypeStruct((rows, D_RWKV), F32),) * 2,
        grid=(geo.b, n_ctx + n_lat),
        in_specs=in_specs,
        out_specs=(shared(0), shared(1)),
        scratch_shapes=[pltpu.VMEM((RWKV_HEADS, PAIR, PAIR), F32)],
        compiler_params=_cparams("parallel", "arbitrary"),
        name="wkv_scan",
    )(v, r, kkn, lw, kd, bb, v, r, kkn, lw, kd, bb)


def _rwkv_out_kernel(yf_ref, yb_ref, r_ref, v_ref, kd_ref, gd_ref, lg_ref, lb_ref, rk_ref, gl_ref,
                     hsum_ref, o_ref):
    hsum = hsum_ref[...]
    inv_n = 1.0 / RWKV_HEAD_DIM
    y = yf_ref[...] + yb_ref[...]
    mu = _split_dot(y, hsum) * inv_n
    dlt = y - mu
    var = _split_dot(dlt * dlt, hsum) * inv_n
    yn = dlt * lax.rsqrt(var + GN_EPS) * lg_ref[...] + lb_ref[...]
    k_bonus = 0.5 * (kd_ref[0].astype(F32) + kd_ref[1].astype(F32))
    rsum = _split_dot(r_ref[...].astype(F32) * k_bonus * rk_ref[...], hsum)
    bonus = rsum * v_ref[...].astype(F32)
    gate = _dot(_sigmoid(gd_ref[...].astype(F32)).astype(BF16), gl_ref[...])
    o_ref[...] = ((yn + bonus) * gate).astype(BF16)


def _rwkv_output(yf, yb, r, v, kd, z, p, geo):
    rows = r.shape[0]
    t = geo.tr
    gcb = (geo.col0 + OFF_GD) // GATE_LORA
    row_spec = pl.BlockSpec((t, D_RWKV), lambda i: (i, 0))
    dir_spec = pl.BlockSpec((2, t, D_RWKV), lambda i: (0, i, 0))
    vec = pl.BlockSpec((1, D_RWKV), lambda i: (0, 0))
    return pl.pallas_call(
        _rwkv_out_kernel,
        out_shape=jax.ShapeDtypeStruct((rows, D_RWKV), BF16),
        grid=(rows // t,),
        in_specs=[row_spec, row_spec, row_spec, row_spec, dir_spec,
                  pl.BlockSpec((t, GATE_LORA), lambda i: (i, gcb)),
                  vec, vec, vec,
                  pl.BlockSpec((GATE_LORA, D_RWKV), lambda i: (0, 0)),
                  pl.BlockSpec((D_RWKV, D_RWKV), lambda i: (0, 0))],
        out_specs=row_spec,
        compiler_params=_cparams("parallel"),
        name="rwkv_output",
    )(yf, yb, r, v, kd, z, p['lnx_g'], p['lnx_b'], p['r_k'], p['g_lora'], geo.hsum)


def _attn_prep_kernel(q_ref, k_ref, cos_ref, sin_ref, qg_ref, kg_ref, qo_ref, ko_ref):
    cos = cos_ref[...]
    sin = sin_ref[...]
    lane = lax.broadcasted_iota(jnp.int32, cos.shape, 1)
    low_half = (lane % (2 * AXIS_FREQS)) < AXIS_FREQS

    def norm_rope(x, g, scale):
        x = x.astype(F32)
        xn = x * lax.rsqrt(jnp.mean(x * x, axis=-1, keepdims=True) + NORM_EPS) * g
        partner = jnp.where(low_half, pltpu.roll(xn, ATTN_HEAD_DIM - AXIS_FREQS, 1),
                            pltpu.roll(xn, AXIS_FREQS, 1))
        return ((xn * cos + partner * sin) * scale).astype(BF16)

    for h in range(ATTN_HEADS):
        sl = slice(h * ATTN_HEAD_DIM, (h + 1) * ATTN_HEAD_DIM)
        qo_ref[:, sl] = norm_rope(q_ref[:, sl], qg_ref[...], ATTN_SCALE)
    for h in range(ATTN_KV_HEADS):
        sl = slice(h * ATTN_HEAD_DIM, (h + 1) * ATTN_HEAD_DIM)
        ko_ref[:, sl] = norm_rope(k_ref[:, sl], kg_ref[...], 1.0)


def _attn_prep(z, cos_t, sin_t, p, geo):
    rows = z.shape[0]
    t = geo.tr
    qcb = (geo.col0 + OFF_Q) // D_ATTN
    kcb = (geo.col0 + OFF_KA) // D_ATTN_KV
    lat_tiles, lat_per = geo.r_lat // t, geo.s // t

    def tab(i):
        return (jnp.where(i < lat_tiles, i % lat_per, lat_per), 0)

    vec = pl.BlockSpec((1, ATTN_HEAD_DIM), lambda i: (0, 0))
    return pl.pallas_call(
        _attn_prep_kernel,
        out_shape=(jax.ShapeDtypeStruct((rows, D_ATTN), BF16),
                   jax.ShapeDtypeStruct((rows, D_ATTN_KV), BF16)),
        grid=(rows // t,),
        in_specs=[pl.BlockSpec((t, D_ATTN), lambda i: (i, qcb)),
                  pl.BlockSpec((t, D_ATTN_KV), lambda i: (i, kcb)),
                  pl.BlockSpec((t, ATTN_HEAD_DIM), tab),
                  pl.BlockSpec((t, ATTN_HEAD_DIM), tab),
                  vec, vec],
        out_specs=(pl.BlockSpec((t, D_ATTN), lambda i: (i, 0)),
                   pl.BlockSpec((t, D_ATTN_KV), lambda i: (i, 0))),
        compiler_params=_cparams("parallel"),
        name="attn_prep",
    )(z, z, cos_t, sin_t, p['q_norm_g'], p['k_norm_g'])


def _attn_kernel(*refs, n_seg):
    q_ref = refs[0]
    k_refs = refs[1:1 + n_seg]
    v_refs = refs[1 + n_seg:1 + 2 * n_seg]
    o_ref = refs[1 + 2 * n_seg]
    def qk(g):
        q = q_ref[:, g * ATTN_HEAD_DIM:(g + 1) * ATTN_HEAD_DIM]
        return [_dot_nt(q, k_ref[...]) for k_ref in k_refs]

    nxt = qk(0)
    for g in range(GQA_GROUP):
        sl = slice(g * ATTN_HEAD_DIM, (g + 1) * ATTN_HEAD_DIM)
        scores = nxt
        if g + 1 < GQA_GROUP:
            nxt = qk(g + 1)
        m = scores[0].max(axis=-1, keepdims=True)
        for s in scores[1:]:
            m = jnp.maximum(m, s.max(axis=-1, keepdims=True))
        denom = None
        acc = None
        for s, v_ref in zip(scores, v_refs):
            e = jnp.exp(s - m)
            es = e.sum(axis=-1, keepdims=True)
            pv = _dot(e.astype(BF16), v_ref[...])
            denom = es if denom is None else denom + es
            acc = pv if acc is None else acc + pv
        o_ref[:, sl] = (acc / denom).astype(BF16)


def _attention(qn, kn, z, geo, latent):
    gw = GQA_GROUP * ATTN_HEAD_DIM
    vcb = (geo.col0 + OFF_VA) // ATTN_HEAD_DIM
    ctx_blk0 = geo.r_lat // geo.lc
    k_ctx = pl.BlockSpec((geo.lc, ATTN_HEAD_DIM), lambda b, h, i: (ctx_blk0 + b, h))
    v_ctx = pl.BlockSpec((geo.lc, ATTN_HEAD_DIM), lambda b, h, i: (ctx_blk0 + b, vcb + h))
    if latent:
        tq = geo.tq
        per = geo.s // tq
        q_spec = pl.BlockSpec((tq, gw), lambda b, h, i: (b * per + i, h))
        k_lat = pl.BlockSpec((geo.s, ATTN_HEAD_DIM), lambda b, h, i: (b, h))
        v_lat = pl.BlockSpec((geo.s, ATTN_HEAD_DIM), lambda b, h, i: (b, vcb + h))
        in_specs = [q_spec, k_ctx, k_lat, v_ctx, v_lat]
        args = (qn, kn, kn, z, z)
        n_seg, out_rows = 2, geo.r_lat
        o_spec = q_spec
    else:
        tq, per = geo.lc, 1
        q_spec = pl.BlockSpec((tq, gw), lambda b, h, i: (ctx_blk0 + b, h))
        in_specs = [q_spec, k_ctx, v_ctx]
        args = (qn, kn, z)
        n_seg, out_rows = 1, geo.r_ctx
        o_spec = pl.BlockSpec((tq, gw), lambda b, h, i: (b, h))
    return pl.pallas_call(
        functools.partial(_attn_kernel, n_seg=n_seg),
        out_shape=jax.ShapeDtypeStruct((out_rows, D_ATTN), BF16),
        grid=(geo.b, ATTN_KV_HEADS, per),
        in_specs=in_specs,
        out_specs=o_spec,
        compiler_params=_cparams("parallel", "parallel", "parallel"),
        name="attention_lat" if latent else "attention_ctx",
    )(*args)


def _mm_kernel(x_ref, w_ref, o_ref):
    o_ref[...] = _dot(x_ref[...], w_ref[...]).astype(o_ref.dtype)


def _fourier_channels(z, wcs, geo):
    rows = z.shape[0]
    t = geo.tr
    fcb = (geo.col0 + OFF_FOUR) // D_FOURIER
    return pl.pallas_call(
        _mm_kernel,
        out_shape=jax.ShapeDtypeStruct((rows, 2 * D_FOURIER), BF16),
        grid=(rows // t,),
        in_specs=[pl.BlockSpec((t, D_FOURIER), lambda i: (i, fcb)),
                  pl.BlockSpec((D_FOURIER, 2 * D_FOURIER), lambda i: (0, 0))],
        out_specs=pl.BlockSpec((t, 2 * D_FOURIER), lambda i: (i, 0)),
        compiler_params=_cparams("parallel"),
        name="fourier_channels",
    )(z, wcs)


def _dft_kernel(cl_ref, sl_ref, xc_ref, xs_ref, o_ref):
    o_ref[...] = (_dot(cl_ref[...], xc_ref[...]) - _dot(sl_ref[...], xs_ref[...])).astype(BF16)


def _fourier_positions(xcs, cl, sl, n, blk0, nb, tmf):
    per = n // tmf
    return pl.pallas_call(
        _dft_kernel,
        out_shape=jax.ShapeDtypeStruct((nb * n, D_FOURIER), BF16),
        grid=(per, nb),
        in_specs=[pl.BlockSpec((tmf, n), lambda i, b: (i, 0)),
                  pl.BlockSpec((tmf, n), lambda i, b: (i, 0)),
                  pl.BlockSpec((n, D_FOURIER), lambda i, b: (blk0 + b, 0)),
                  pl.BlockSpec((n, D_FOURIER), lambda i, b: (blk0 + b, 1))],
        out_specs=pl.BlockSpec((tmf, D_FOURIER), lambda i, b: (b * per + i, 0)),
        compiler_params=_cparams("parallel", "parallel"),
        name="fourier_positions",
    )(cl, sl, xcs, xcs)


def _merge_kernel(rw_ref, fo_ref, at_ref, ga_ref, gf_ref, gc_ref, wr_ref, wf_ref, wa_ref, o_ref):
    y = _sigmoid(ga_ref[...].astype(F32)) * _dot(rw_ref[...], wr_ref[...])
    y += _sigmoid(gf_ref[...].astype(F32)) * _dot(fo_ref[...], wf_ref[...])
    y += _sigmoid(gc_ref[...].astype(F32)) * _dot(at_ref[...], wa_ref[...])
    o_ref[...] = y.astype(BF16)


def _merge(rw, fo, at, z, p, geo, n_rows):
    d = geo.d
    tm = geo.tm
    tn = min(1024, d)
    nj = d // tn
    return pl.pallas_call(
        _merge_kernel,
        out_shape=jax.ShapeDtypeStruct((n_rows, d), BF16),
        grid=(nj, n_rows // tm),
        in_specs=[pl.BlockSpec((tm, D_RWKV), lambda j, i: (i, 0)),
                  pl.BlockSpec((tm, D_FOURIER), lambda j, i: (i, 0)),
                  pl.BlockSpec((tm, D_ATTN), lambda j, i: (i, 0)),
                  pl.BlockSpec((tm, tn), lambda j, i: (i, j)),
                  pl.BlockSpec((tm, tn), lambda j, i: (i, nj + j)),
                  pl.BlockSpec((tm, tn), lambda j, i: (i, 2 * nj + j)),
                  pl.BlockSpec((D_RWKV, tn), lambda j, i: (0, j)),
                  pl.BlockSpec((D_FOURIER, tn), lambda j, i: (0, j)),
                  pl.BlockSpec((D_ATTN, tn), lambda j, i: (0, j))],
        out_specs=pl.BlockSpec((tm, tn), lambda j, i: (i, j)),
        compiler_params=_cparams("parallel", "parallel"),
        name="merge",
    )(rw, fo, at, z, z, z, p['w_br_rwkv'], p['w_br_fourier'], p['w_br_attn'])


def _outres_kernel(y_ref, w_ref, x_ref, gt_ref, o_ref):
    o_ref[...] = x_ref[...] + gt_ref[...] * _dot(y_ref[...], w_ref[...])


def _out_proj_residual(y, w, xs, mod4, geo, n_rows):
    d = geo.d
    tm = geo.tm
    tn = min(1024, d)
    bidx = geo.batch_of_tile(tm)
    return pl.pallas_call(
        _outres_kernel,
        out_shape=jax.ShapeDtypeStruct((n_rows, d), F32),
        grid=(d // tn, n_rows // tm),
        in_specs=[pl.BlockSpec((tm, d), lambda j, i: (i, 0)),
                  pl.BlockSpec((d, tn), lambda j, i: (0, j)),
                  pl.BlockSpec((tm, tn), lambda j, i: (i, j)),
                  pl.BlockSpec((None, None, 1, tn), lambda j, i: (bidx(i), 2, 0, j))],
        out_specs=pl.BlockSpec((tm, tn), lambda j, i: (i, j)),
        compiler_params=_cparams("parallel", "parallel"),
        name="out_proj_residual",
    )(y, w, xs, mod4)


def _expert_ffn(h_lo, h_hi, wgu_ref, bgu_ref, wd_ref, bd_ref, f):
    half = wgu_ref.shape[0] // 2
    gu = _dot(h_lo, wgu_ref[:half, :]) + _dot(h_hi, wgu_ref[half:, :]) + bgu_ref[...]
    gate = jnp.minimum(gu[:, :f], SWIGLU_LIMIT)
    up = jnp.clip(gu[:, f:], -SWIGLU_LIMIT, SWIGLU_LIMIT)
    act = (up + 1.0) * gate * _sigmoid(SWIGLU_ALPHA * gate)
    return _dot(act.astype(BF16), wd_ref[...]) + bd_ref[...]


def _route(sel, cnt, n, t, n_exp):
    p = TOP_K * n
    max_used = p // t + n_exp
    assert max_used % 2 == 0
    n_steps = max_used // 2 + 1
    n_tiles = 2 * n_steps
    unused_key = 2 * n_exp
    counts = cnt[0, :n_exp].astype(jnp.int32)
    ptiles = (counts + t - 1) // t
    n_used = jnp.sum(ptiles)
    pad = ptiles * t - counts
    experts = jnp.arange(n_exp, dtype=jnp.int32)
    last_e = jnp.max(jnp.where(counts > 0, experts, 0))

    e_real = sel[:, :TOP_K].astype(jnp.int32).reshape(p)
    w_real = sel[:, TOP_K:2 * TOP_K].reshape(p)
    pair = jnp.arange(p, dtype=jnp.int32)
    e_cand = jnp.repeat(experts, t)
    c_cand = jnp.tile(jnp.arange(t, dtype=jnp.int32), n_exp)
    key_cand = jnp.where(c_cand < jnp.repeat(pad, t), 2 * e_cand + 1, unused_key)
    n_extra = n_tiles * t - p - n_exp * t
    dest_bits = int(p + (n_exp + 2) * t - 1).bit_length()
    assert (unused_key + 1) << dest_bits < 2 ** 31
    words = jnp.concatenate([
        ((2 * e_real) << dest_bits) | ((pair % TOP_K) * n + pair // TOP_K),
        (key_cand << dest_bits) | (p + e_cand * t + c_cand),
        jnp.full((n_extra,), unused_key << dest_bits, jnp.int32)])
    ws = jnp.concatenate([w_real, jnp.zeros((n_exp * t + n_extra,), F32)])
    word_s, w_s = lax.sort((words, ws), num_keys=1, is_stable=True)

    lane = jnp.arange(t, dtype=jnp.int32)[None, :]
    spare = p + n_exp * t + lane
    word_s = word_s.reshape(n_tiles, t)
    key_s, dest_s = word_s >> dest_bits, word_s & ((1 << dest_bits) - 1)
    dest = jnp.where(key_s == unused_key, spare, dest_s)
    tok_s = jnp.where(key_s % 2 == 0, dest_s % n, 0)
    tok_s = jnp.where(key_s == unused_key, 0, tok_s)
    te = jnp.where(jnp.arange(n_tiles) < n_used, jnp.minimum(key_s[:, 0] // 2, n_exp - 1), last_e)
    dest_prev = jnp.concatenate([spare, dest[1:-1:2]], axis=0)
    return (te.astype(jnp.int32), n_used.reshape(1).astype(jnp.int32), tok_s.reshape(n_tiles, 1, t),
            dest.reshape(n_tiles, 1, t), dest_prev.reshape(n_steps, 1, t), w_s.reshape(n_tiles * t, 1))


def _moe_routed_kernel(te_ref, nu_ref, toka_ref, tokb_ref, tokn_ref, dsta_ref, dstp_ref, wa_ref, wb_ref,
                       h_hbm, wgu_a, bgu_a, wd_a, bd_a, wgu_b, bgu_b, wd_b, bd_b, o_hbm,
                       ga, gb, oa, ob, gsem, ssem, *, f, t, dump0, n_dump):
    j = pl.program_id(0)
    n_used = nu_ref[0]

    def gather(idx_ref, buf, sem):
        @pl.when(n_used > 0)
        def _():
            for r in range(t):
                pltpu.make_async_copy(h_hbm.at[pl.ds(idx_ref[0, 0, r], 1)], buf.at[pl.ds(r, 1)],
                                      sem).start(priority=r % 2)

    def scatter(buf, idx_ref, sem):
        @pl.when(n_used > 0)
        def _():
            for r in range(t):
                pltpu.make_async_copy(buf.at[pl.ds(r, 1)], o_hbm.at[pl.ds(idx_ref[0, 0, r], 1)],
                                      sem).start(priority=r % 2)

    def wait_rows_in(buf, sem):
        pltpu.make_async_copy(h_hbm.at[pl.ds(0, t)], buf, sem).wait()

    def wait_rows_out(buf, sem):
        pltpu.make_async_copy(buf, o_hbm.at[pl.ds(0, t)], sem).wait()

    def experts(buf, wgu, bgu, wd, bd, w_ref):
        lo, hi = _unpack_halves(buf[...])
        y = _expert_ffn(lo.astype(BF16), hi.astype(BF16), wgu, bgu, wd, bd, f)
        return _pack_halves(y * w_ref[...])

    @pl.when(j == 0)
    def _():
        gather(toka_ref, ga, gsem.at[0])
        ob[...] = jnp.zeros_like(ob)
        fills = [pltpu.make_async_copy(ob, o_hbm.at[pl.ds(dump0 + e * t, t)], ssem.at[1])
                 for e in range(n_dump - 1)]
        for c in fills:
            c.start()
        for c in fills:
            c.wait()
        pltpu.make_async_copy(ob, o_hbm.at[pl.ds(dump0 + (n_dump - 1) * t, t)], ssem.at[0]).start()

    @pl.when(2 * j <= n_used)
    def _():
        wait_rows_in(ga, gsem.at[0])
        scatter(ob, dstp_ref, ssem.at[1])
        gather(tokb_ref, gb, gsem.at[1])
        ya = experts(ga, wgu_a, bgu_a, wd_a, bd_a, wa_ref)
        wait_rows_out(oa, ssem.at[0])
        oa[...] = ya
        wait_rows_in(gb, gsem.at[1])
        scatter(oa, dsta_ref, ssem.at[0])
        gather(tokn_ref, ga, gsem.at[0])
        yb = experts(gb, wgu_b, bgu_b, wd_b, bd_b, wb_ref)
        wait_rows_out(ob, ssem.at[1])
        ob[...] = yb

    @pl.when(j == pl.num_programs(0) - 1)
    def _():
        wait_rows_in(ga, gsem.at[0])
        wait_rows_out(oa, ssem.at[0])


def _moe_routed(hp, sel, cnt, wgu, bgu, wd, bd, geo, n_rows):
    d = geo.d
    n_exp, _, f2 = wgu.shape
    f = f2 // 2
    t = geo.t_moe
    te, n_used, tok, dest, dest_prev, w = _route(sel, cnt, n_rows, t, n_exp)
    n_tiles, n_steps = tok.shape[0], dest_prev.shape[0]
    n_dump = n_exp + 2
    smem = functools.partial(pl.BlockSpec, (1, 1, t), memory_space=pltpu.SMEM)

    def weights(which):
        return [pl.BlockSpec((None, d, f2), lambda j, te, nu: (te[2 * j + which], 0, 0)),
                pl.BlockSpec((None, 1, f2), lambda j, te, nu: (te[2 * j + which], 0, 0)),
                pl.BlockSpec((None, f, d), lambda j, te, nu: (te[2 * j + which], 0, 0)),
                pl.BlockSpec((None, 1, d), lambda j, te, nu: (te[2 * j + which], 0, 0))]

    grid_spec = pltpu.PrefetchScalarGridSpec(
        num_scalar_prefetch=2,
        grid=(n_steps,),
        in_specs=[smem(lambda j, te, nu: (2 * j, 0, 0)),
                  smem(lambda j, te, nu: (2 * j + 1, 0, 0)),
                  smem(lambda j, te, nu: (jnp.minimum(2 * j + 2, n_tiles - 1), 0, 0)),
                  smem(lambda j, te, nu: (2 * j, 0, 0)),
                  smem(lambda j, te, nu: (j, 0, 0)),
                  pl.BlockSpec((t, 1), lambda j, te, nu: (2 * j, 0)),
                  pl.BlockSpec((t, 1), lambda j, te, nu: (2 * j + 1, 0)),
                  pl.BlockSpec(memory_space=pl.ANY)] + weights(0) + weights(1),
        out_specs=pl.BlockSpec(memory_space=pl.ANY),
        scratch_shapes=[pltpu.VMEM((t, d // 2), jnp.uint32)] * 4
                       + [pltpu.SemaphoreType.DMA((2,)), pltpu.SemaphoreType.DMA((2,))])
    return pl.pallas_call(
        functools.partial(_moe_routed_kernel, f=f, t=t, dump0=TOP_K * n_rows, n_dump=n_dump),
        out_shape=jax.ShapeDtypeStruct((TOP_K * n_rows + n_dump * t, d // 2), jnp.uint32),
        grid_spec=grid_spec,
        compiler_params=pltpu.CompilerParams(dimension_semantics=("arbitrary",),
                                             vmem_limit_bytes=VMEM_LIMIT, disable_bounds_checks=True),
        name="moe_routed",
    )(te, n_used, tok, tok, tok, dest, dest_prev, w, w, hp, wgu, bgu, wd, bd, wgu, bgu, wd, bd)


def _combine_kernel(x_ref, gt_ref, y0_ref, y1_ref, y2_ref, y3_ref, o_ref):
    half = x_ref.shape[1] // 2
    parts = [_unpack_halves(y[...]) for y in (y0_ref, y1_ref, y2_ref, y3_ref)]
    lo = (parts[0][0] + parts[1][0]) + (parts[2][0] + parts[3][0])
    hi = (parts[0][1] + parts[1][1]) + (parts[2][1] + parts[3][1])
    o_ref[:, :half] = x_ref[:, :half] + gt_ref[:, :half] * lo
    o_ref[:, half:] = x_ref[:, half:] + gt_ref[:, half:] * hi


def _moe_combine(y4, xs, mod4, geo, n_rows):
    d = geo.d
    tm = geo.tr
    per = n_rows // tm
    bidx = geo.batch_of_tile(tm)

    def part(k):
        return pl.BlockSpec((tm, d // 2), lambda i: (k * per + i, 0))

    return pl.pallas_call(
        _combine_kernel,
        out_shape=jax.ShapeDtypeStruct((n_rows, d), F32),
        grid=(per,),
        in_specs=[pl.BlockSpec((tm, d), lambda i: (i, 0)),
                  pl.BlockSpec((None, None, 1, d), lambda i: (bidx(i), 5, 0, 0)),
                  part(0), part(1), part(2), part(3)],
        out_specs=pl.BlockSpec((tm, d), lambda i: (i, 0)),
        compiler_params=_cparams("parallel"),
        name="moe_combine",
    )(xs, mod4, y4, y4, y4, y4)


class _Geometry:
    def __init__(self, b, s, lc, d):
        self.b, self.s, self.lc, self.d = b, s, lc, d
        self.r_lat, self.r_ctx = b * s, b * lc
        self.rows = self.r_lat + self.r_ctx
        self.col0 = 3 * d
        assert self.col0 % 1024 == 0, "gate columns must end on a 1024-column boundary"
        self.tm = min(1024, s, self.r_ctx)
        self.tp = min(256, s, lc)
        self.tr = min(512, s, self.r_ctx)
        self.tq = min(512, s)
        self.chunk = min(WKV_CHUNK, lc, s)
        self.t_moe = min(256, self.tm)
        for t in (self.tm,):
            assert s % t == 0 and self.r_ctx % t == 0
        assert s % self.tp == 0 and lc % self.tp == 0 and self.tp % BF16_SUBLANES == 0
        assert s % self.tr == 0 and self.r_ctx % self.tr == 0
        assert s % self.chunk == 0 and lc % self.chunk == 0 and s % GRID_W == 0
        self.hsum = jnp.asarray(np.kron(np.eye(RWKV_HEADS), np.ones((RWKV_HEAD_DIM,) * 2)), BF16)

    def batch_of_tile(self, tm):
        per, nb = self.s // tm, self.b
        return lambda i: jnp.minimum(i // per, nb)


def _dft_tables(n):
    j = np.arange(n, dtype=np.int64)
    ang = 2.0 * np.pi * ((j[:, None] * j[None, :]) % n).astype(np.float64) / n
    return np.cos(ang) / np.sqrt(n), np.sin(ang) / np.sqrt(n)


def _rope_tables(s, t):
    pos = np.arange(s)
    inv_freq = ROPE_THETA ** (-np.arange(AXIS_FREQS, dtype=np.float32) / AXIS_FREQS)
    ang_r = (pos // GRID_W).astype(np.float32)[:, None] * inv_freq.astype(np.float32)
    ang_c = (pos % GRID_W).astype(np.float32)[:, None] * inv_freq.astype(np.float32)
    cr, sr, cc, sc = np.cos(ang_r), np.sin(ang_r), np.cos(ang_c), np.sin(ang_c)
    cos = np.concatenate([cr, cr, cc, cc], axis=1)
    sin = np.concatenate([-sr, sr, -sc, sc], axis=1)
    cos = np.concatenate([cos, np.ones((t, ATTN_HEAD_DIM))], axis=0)
    sin = np.concatenate([sin, np.zeros((t, ATTN_HEAD_DIM))], axis=0)
    return jnp.asarray(cos, F32), jnp.asarray(sin, F32)


def _permute_w_in(w_in, d):
    o = np.cumsum([0, D_RWKV, D_RWKV, LORA, LORA, LORA, LORA, D_ATTN_KV, D_ATTN_KV, D_RWKV, GATE_LORA,
                   D_ATTN, D_FOURIER, 3 * d])
    w_in = w_in.astype(BF16)
    seg = lambda i: w_in[..., o[i]:o[i + 1]]
    pad = lambda a: jnp.pad(a, ((0, 0), (0, 0), (0, LORA_PAD - LORA)))
    parts = [seg(12), seg(0), seg(1), pad(seg(2)), pad(seg(3)), pad(seg(4)), pad(seg(5)), seg(8), seg(10),
             seg(11), seg(6), seg(7), seg(9)]
    w = jnp.concatenate(parts, axis=-1)
    n = w.shape[-1]
    n_pad = -(-n // W_IN_TN) * W_IN_TN
    return jnp.pad(w, ((0, 0), (0, 0), (0, n_pad - n)))


def _split_hi_lo(w):
    hi = w.astype(BF16)
    return hi, (w - hi.astype(F32)).astype(BF16)


def kernel(x, c, ctx, c_ctx, ada_w, ada_b, norm1_g, norm2_g, w_in, rwkv_conv, w0, w_lora, a0, a_lora, g_lora, k_k, k_a, r_k, lnx_g, lnx_b, q_norm_g, k_norm_g, w_br_rwkv, w_br_fourier, w_br_attn, w_out, router_w, router_b, exp_w_gu, exp_b_gu, exp_w_down, exp_b_down):
    b, s, d = x.shape
    lc = ctx.shape[1]
    depth = w_in.shape[0]
    n_exp = router_w.shape[-1]
    geo = _Geometry(b, s, lc, d)

    w_in_p = _permute_w_in(w_in, d)
    lora_pad = ((0, 0), (0, 0), (0, LORA_PAD - LORA), (0, 0))
    w_lora_p = jnp.pad(w_lora, lora_pad).astype(BF16)
    a_lora_p = jnp.pad(a_lora, lora_pad).astype(BF16)
    g_lora_b = g_lora.astype(BF16)
    wbr_r, wbr_f, wbr_a, w_out_b = (w.astype(BF16) for w in (w_br_rwkv, w_br_fourier, w_br_attn, w_out))
    rw_p = jnp.pad(router_w, ((0, 0), (0, 0), (0, ROUTER_PAD - n_exp)))
    rw_hi, rw_lo = _split_hi_lo(rw_p)
    rb_p = jnp.pad(router_b, ((0, 0), (0, ROUTER_PAD - n_exp)), constant_values=NEG_BIG)
    wgu_b, wd_b = exp_w_gu.astype(BF16), exp_w_down.astype(BF16)
    cos_t, sin_t = _rope_tables(s, geo.tr)
    cc, sc = _dft_tables(FOURIER_GROUP_DIM)
    groups = D_FOURIER // FOURIER_GROUP_DIM
    wcs = jnp.asarray(np.concatenate([np.kron(np.eye(groups), cc), np.kron(np.eye(groups), sc)], axis=1), BF16)
    cl_lat, sl_lat = (jnp.asarray(m, BF16) for m in _dft_tables(s))
    cl_ctx, sl_ctx = (jnp.asarray(m, BF16) for m in _dft_tables(lc))

    mod_rows = -(-(b + 1) // BF16_SUBLANES) * BF16_SUBLANES
    c_all = jnp.concatenate([c, c_ctx[None], jnp.zeros((mod_rows - b - 1, d), F32)], axis=0)
    mod = _adaln(c_all, ada_w, ada_b).reshape(depth, mod_rows, 6, 1, d)

    xs = jnp.concatenate([x.reshape(b * s, d), ctx.reshape(b * lc, d)], axis=0)
    for l in range(depth):
        last = l == depth - 1
        mod4 = mod[l]
        p = dict(conv=rwkv_conv[l], w0=w0[l][:, None, :], w_lora=w_lora_p[l], a0=a0[l][:, None, :],
                 a_lora=a_lora_p[l], k_k=k_k[l][None], k_a=k_a[l][None], r_k=r_k[l][None],
                 lnx_g=lnx_g[l][None], lnx_b=lnx_b[l][None], g_lora=g_lora_b[l],
                 q_norm_g=q_norm_g[l][None], k_norm_g=k_norm_g[l][None],
                 w_br_rwkv=wbr_r[l], w_br_fourier=wbr_f[l], w_br_attn=wbr_a[l])
        n_rows = geo.r_lat if last else geo.rows

        z = _norm_in_proj(xs, norm1_g[l], mod4, w_in_p[l], geo)

        v_c, r_c, kkn, lw, kd, bb = _rwkv_prepare(z, p, geo)
        yf, yb = _wkv_scan(v_c, r_c, kkn, lw, kd, bb, geo)
        rw = _rwkv_output(yf, yb, r_c, v_c, kd, z, p, geo)

        qn, kn = _attn_prep(z, cos_t, sin_t, p, geo)
        att = _attention(qn, kn, z, geo, latent=True)

        xcs = _fourier_channels(z, wcs, geo)
        fo = _fourier_positions(xcs, cl_lat, sl_lat, s, 0, b, min(1024, s))
        if not last:
            att = jnp.concatenate([att, _attention(qn, kn, z, geo, latent=False)], axis=0)
            fo_c = _fourier_positions(xcs, cl_ctx, sl_ctx, lc, geo.r_lat // lc, b, lc)
            fo = jnp.concatenate([fo, fo_c], axis=0)

        ym = _merge(rw, fo, att, z, p, geo, n_rows)
        xs = _out_proj_residual(ym, w_out_b[l], xs, mod4, geo, n_rows)

        hp, sel, cnt = _norm_router(xs, norm2_g[l], mod4, rw_hi[l], rw_lo[l], rb_p[l][None], geo, n_rows)
        y4 = _moe_routed(hp, sel, cnt, wgu_b[l], exp_b_gu[l][:, None, :], wd_b[l], exp_b_down[l][:, None, :],
                         geo, n_rows)
        xs = _moe_combine(y4, xs, mod4, geo, n_rows)
    return xs[:geo.r_lat].reshape(b, s, d)
```
